```python
import jax
import jax.numpy as jnp
from jax import lax
import numpy as np

D_MODEL = 1024
BATCH = 4
SEQ = 4096
DEPTH = 2
DEC_BATCH = 32
DEC_SEQ = 32
PAST_LEN = 4096

CHUNK = 64
Q_BLOCK = 128
HEAD_DIM = 64
H_FOX = 4
H_MLA = 4
MLA_NOPE = 64
MLA_ROPE = 32
MLA_V = 64
MLA_Q_RANK = 192
MLA_KV_RANK = 128
ROPE_THETA = 10000.0
H_BAND = 4
BAND_CHUNKS = 8
BAND_HIST = BAND_CHUNKS * CHUNK
REL_CLIP = 128
H_SB = 4
N_BRANCH = 4
BRANCH_WIDTH = 256
N_EXPERTS = 16
N_GROUPS = 4
EXPERTS_PER_GROUP = N_EXPERTS // N_GROUPS
TOP_K = 2
D_EXPERT = 512
ALPHA = (2 * DEPTH) ** 0.25
BETA = (8 * DEPTH) ** -0.25
NEG_INF = -1e30
IN_SIZES = (H_FOX * HEAD_DIM, H_FOX * HEAD_DIM, H_FOX * HEAD_DIM, H_FOX,
            MLA_Q_RANK, MLA_KV_RANK, MLA_ROPE,
            H_BAND * HEAD_DIM, H_BAND * HEAD_DIM, H_BAND * HEAD_DIM,
            H_SB * HEAD_DIM, H_SB * HEAD_DIM, H_SB * HEAD_DIM)
IN_WIDTH = sum(IN_SIZES)

kernel_name = 'hybrid_streaming_encoder_step'


def rms_norm(x, g, eps=1e-6):
    xf = x.astype(jnp.float32)
    y = xf * lax.rsqrt(jnp.mean(xf * xf, axis=-1, keepdims=True) + eps)
    return (y * g.astype(jnp.float32)).astype(x.dtype)


def layer_norm(x, g, b, eps=1e-5):
    xf = x.astype(jnp.float32)
    mu = jnp.mean(xf, axis=-1, keepdims=True)
    var = jnp.mean(jnp.square(xf - mu), axis=-1, keepdims=True)
    y = (xf - mu) * lax.rsqrt(var + eps)
    return (y * g.astype(jnp.float32) + b.astype(jnp.float32)).astype(x.dtype)


def rope(x, pos):
    half = x.shape[-1] // 2
    inv = ROPE_THETA ** (-jnp.arange(half, dtype=jnp.float32) / half)
    ang = pos.astype(jnp.float32)[:, None] * inv[None, :]
    cos = jnp.cos(ang)[None, :, None, :]
    sin = jnp.sin(ang)[None, :, None, :]
    x1 = x[..., :half].astype(jnp.float32)
    x2 = x[..., half:].astype(jnp.float32)
    return jnp.concatenate([x1 * cos - x2 * sin, x1 * sin + x2 * cos], axis=-1).astype(x.dtype)


def token_rows(x, pos, w_in, b_forget, q_norm, w_q_up, kv_norm):
    B, T, _ = x.shape
    offs = np.cumsum(IN_SIZES)[:-1].tolist()
    qa, ka, va, fa, cq, ckv, kr, qc, kc, vc, qd, kd, vd = jnp.split(x @ w_in, offs, axis=-1)
    hs = lambda t: t.reshape(B, T, -1, HEAD_DIM)
    logf = jax.nn.log_sigmoid((fa + b_forget).astype(jnp.float32))
    qb = (rms_norm(cq, q_norm) @ w_q_up).reshape(B, T, H_MLA, MLA_NOPE + MLA_ROPE)
    q_nope = qb[..., :MLA_NOPE]
    q_rope = rope(qb[..., MLA_NOPE:], pos)
    latent = rms_norm(ckv, kv_norm)
    k_rope = rope(kr[:, :, None, :], pos)[:, :, 0, :]
    return ((hs(qa), hs(ka), hs(va), logf), (q_nope, q_rope, latent, k_rope),
            (hs(qc), hs(kc), hs(vc)), (hs(qd), hs(kd), hs(vd)))


def mla_expand(latent, w_kv_up):
    B, T, _ = latent.shape
    kv = (latent @ w_kv_up).reshape(B, T, H_MLA, MLA_NOPE + MLA_V)
    return kv[..., :MLA_NOPE], kv[..., MLA_NOPE:]


def fox_attend(q_pos, q, cq, k, v, ck, k_pos):
    s = jnp.einsum('bqhd,bkhd->bhqk', q, k).astype(jnp.float32) * HEAD_DIM ** -0.5
    s = s + jnp.swapaxes(cq, 1, 2)[..., :, None] - jnp.swapaxes(ck, 1, 2)[..., None, :]
    mask = k_pos[None, :] <= q_pos[:, None]
    p = jax.nn.softmax(jnp.where(mask, s, NEG_INF), axis=-1)
    return jnp.einsum('bhqk,bkhd->bqhd', p.astype(v.dtype), v)


def mla_attend(q_pos, q_nope, q_rope, k_nope, k_rope, v, k_pos):
    s = (jnp.einsum('bqhd,bkhd->bhqk', q_nope, k_nope).astype(jnp.float32)
         + jnp.einsum('bqhr,bkr->bhqk', q_rope, k_rope).astype(jnp.float32))
    s = s * (MLA_NOPE + MLA_ROPE) ** -0.5
    mask = (k_pos // CHUNK)[None, :] <= (q_pos // CHUNK)[:, None]
    p = jax.nn.softmax(jnp.where(mask, s, NEG_INF), axis=-1)
    return jnp.einsum('bhqk,bkhd->bqhd', p.astype(v.dtype), v)


def band_attend(q_pos, q, k, v, k_pos, rel_bias):
    s = jnp.einsum('...qhd,...khd->...hqk', q, k).astype(jnp.float32) * HEAD_DIM ** -0.5
    rel = q_pos[..., :, None] - k_pos[..., None, :]
    bias = jnp.moveaxis(rel_bias[:, jnp.clip(rel, -REL_CLIP, REL_CLIP) + REL_CLIP], 0, -3)
    qc = (q_pos // CHUNK)[..., :, None]
    kc = (k_pos // CHUNK)[..., None, :]
    valid = (k_pos[..., None, :] >= 0) & (kc <= qc) & (kc >= qc - BAND_CHUNKS)
    s = jnp.where(valid[..., None, :, :], s + bias.astype(jnp.float32), NEG_INF)
    p = jax.nn.softmax(s, axis=-1)
    return jnp.einsum('...hqk,...khd->...qhd', p.astype(v.dtype), v)


def sb_attend(q_pos, q, k, v, k_pos):
    z = jnp.einsum('bqhd,bkhd->bhqk', q, k).astype(jnp.float32) * HEAD_DIM ** -0.5
    mask = k_pos[None, :] < q_pos[:, None]
    log_1m = jnp.where(mask, jax.nn.log_sigmoid(-z), 0.0)
    rem = lax.cumsum(log_1m, axis=3, reverse=True) - log_1m
    a = jnp.where(mask, jnp.exp(jax.nn.log_sigmoid(z) + rem), 0.0)
    return jnp.einsum('bhqk,bkhd->bqhd', a.astype(v.dtype), v)


def sweep_query_blocks(attend_fn, q_pos, *qs):
    n_blk = q_pos.shape[0] // Q_BLOCK
    pos_b = q_pos.reshape(n_blk, Q_BLOCK)
    qs_b = tuple(jnp.swapaxes(q.reshape(q.shape[0], n_blk, Q_BLOCK, *q.shape[2:]), 0, 1) for q in qs)
    out = lax.map(lambda a: attend_fn(a[0], *a[1]), (pos_b, qs_b))
    out = jnp.swapaxes(out, 0, 1)
    return out.reshape(out.shape[0], n_blk * Q_BLOCK, *out.shape[3:])


def band_prompt(q, k, v, rel_bias):
    B, S, H, Dh = q.shape
    nc = S // CHUNK
    pad = BAND_CHUNKS * CHUNK
    kp = jnp.pad(k, ((0, 0), (pad, 0), (0, 0), (0, 0))).reshape(B, nc + BAND_CHUNKS, CHUNK, H, Dh)
    vp = jnp.pad(v, ((0, 0), (pad, 0), (0, 0), (0, 0))).reshape(B, nc + BAND_CHUNKS, CHUNK, H, Dh)
    idx = jnp.arange(nc)[:, None] + jnp.arange(BAND_CHUNKS + 1)[None, :]
    kb = kp[:, idx].reshape(B, nc, (BAND_CHUNKS + 1) * CHUNK, H, Dh)
    vb = vp[:, idx].reshape(B, nc, (BAND_CHUNKS + 1) * CHUNK, H, Dh)
    q_pos = jnp.arange(S, dtype=jnp.int32).reshape(nc, CHUNK)
    k_pos = ((idx - BAND_CHUNKS)[:, :, None] * CHUNK + jnp.arange(CHUNK)[None, None, :]).reshape(nc, -1)
    o = band_attend(q_pos, q.reshape(B, nc, CHUNK, H, Dh), kb, vb, k_pos, rel_bias)
    return o.reshape(B, S, H, Dh)


def prompt_mixers(rows, pos, w_kv_up, rel_bias):
    (qa, ka, va, logf), (qn, qr, lat, kr), (qc, kc, vc), (qd, kd, vd) = rows
    cum = lax.cumsum(logf, axis=1)
    o_a = sweep_query_blocks(lambda qp, q, cq: fox_attend(qp, q, cq, ka, va, cum, pos), pos, qa, cum)
    kn, vb = mla_expand(lat, w_kv_up)
    o_b = sweep_query_blocks(lambda qp, q1, q2: mla_attend(qp, q1, q2, kn, kr, vb, pos), pos, qn, qr)
    o_c = band_prompt(qc, kc, vc, rel_bias)
    o_d = sweep_query_blocks(lambda qp, q: sb_attend(qp, q, kd, vd, pos), pos, qd)
    n_keep = min(BAND_HIST, kc.shape[1])
    state = (ka, va, logf, lat, kr, kc[:, kc.shape[1] - n_keep:], vc[:, vc.shape[1] - n_keep:], kd, vd)
    return (o_a, o_b, o_c, o_d), state


def sample_mixers(rows, pos, caches, w_kv_up, rel_bias):
    (qa, ka, va, logf), (qn, qr, lat, kr), (qc, kc, vc), (qd, kd, vd) = rows
    c_fk, c_fv, c_fl, c_lat, c_kr, c_bk, c_bv, c_sk, c_sv = caches
    P = c_fk.shape[1]
    T = pos.shape[0]
    cat = lambda a, b: jnp.concatenate([a, b], axis=1)
    k_pos = jnp.arange(P + T, dtype=jnp.int32)
    cum = lax.cumsum(cat(c_fl.astype(jnp.float32), logf), axis=1)
    o_a = fox_attend(pos, qa, cum[:, P:], cat(c_fk, ka), cat(c_fv, va), cum, k_pos)
    kn, vb = mla_expand(cat(c_lat, lat), w_kv_up)
    o_b = mla_attend(pos, qn, qr, kn, cat(c_kr, kr), vb, k_pos)
    n_band = c_bk.shape[1]
    band_pos = jnp.arange(P - n_band, P + T, dtype=jnp.int32)
    o_c = band_attend(pos, qc, cat(c_bk, kc), cat(c_bv, vc), band_pos, rel_bias)
    o_d = sb_attend(pos, qd, cat(c_sk, kd), cat(c_sv, vd), k_pos)
    state = (ka, va, logf, lat, kr, kc, vc, kd, vd)
    return (o_a, o_b, o_c, o_d), state


def mixer_merge(x, branches, w_branch, w_gate, b_gate, w_out):
    merged = None
    for n, o in enumerate(branches):
        term = jax.nn.sigmoid(x @ w_gate[n] + b_gate[n]) * (o.reshape(o.shape[0], o.shape[1], -1) @ w_branch[n])
        merged = term if merged is None else merged + term
    return merged @ w_out


def moe(x, w_router, router_bias, w_e_gate, w_e_up, w_e_down):
    scores = jax.nn.sigmoid((x @ w_router).astype(jnp.float32))
    sel = scores + router_bias.astype(jnp.float32)
    sel_g = sel.reshape(*sel.shape[:-1], N_GROUPS, EXPERTS_PER_GROUP)
    group_score = lax.top_k(sel_g, TOP_K)[0].sum(-1)
    g = jnp.argmax(group_score, axis=-1)
    in_group = jnp.take_along_axis(sel_g, g[..., None, None], axis=-2)[..., 0, :]
    _, local = lax.top_k(in_group, TOP_K)
    idx = g[..., None] * EXPERTS_PER_GROUP + local
    w = jnp.take_along_axis(scores, idx, axis=-1)
    w = w / jnp.sum(w, axis=-1, keepdims=True)
    combine = jnp.einsum('btk,btke->bte', w, jax.nn.one_hot(idx, N_EXPERTS, dtype=jnp.float32)).astype(x.dtype)
    y = jnp.zeros_like(x)
    for e in range(N_EXPERTS):
        h = jax.nn.silu(x @ w_e_gate[e]) * (x @ w_e_up[e])
        y = y + combine[..., e:e + 1] * (h @ w_e_down[e])
    return y


def layer_tail(x, branches, w_branch, w_gate, b_gate, w_out, ln1_g, ln1_b,
               w_router, router_bias, w_e_gate, w_e_up, w_e_down, ln2_g, ln2_b):
    x = layer_norm(ALPHA * x + mixer_merge(x, branches, w_branch, w_gate, b_gate, w_out), ln1_g, ln1_b)
    return layer_norm(ALPHA * x + moe(x, w_router, router_bias, w_e_gate, w_e_up, w_e_down), ln2_g, ln2_b)


def setup_inputs(seed: int = 0) -> dict:
    key = jax.random.key(seed)
    ks = jax.random.split(key, 40)
    nrm = lambda k, shape, scale: jax.random.normal(k, shape, jnp.float32) * scale
    band_rows = min(BAND_HIST, PAST_LEN)
    L, DB, P = DEPTH, DEC_BATCH, PAST_LEN
    return {
        'x_prompt': nrm(ks[0], (BATCH, SEQ, D_MODEL), 1.0),
        'x_sample': nrm(ks[1], (DEC_BATCH, DEC_SEQ, D_MODEL), 1.0),
        'cache_fox_k': nrm(ks[2], (L, DB, P, H_FOX, HEAD_DIM), 1.0),
        'cache_fox_v': nrm(ks[3], (L, DB, P, H_FOX, HEAD_DIM), 1.0),
        'cache_fox_logf': jax.nn.log_sigmoid(4.0 + nrm(ks[4], (L, DB, P, H_FOX), 1.0)),
        'cache_mla_latent': nrm(ks[5], (L, DB, P, MLA_KV_RANK), 1.0),
        'cache_mla_krope': nrm(ks[6], (L, DB, P, MLA_ROPE), 1.0),
        'cache_band_k': nrm(ks[7], (L, DB, band_rows, H_BAND, HEAD_DIM), 1.0),
        'cache_band_v': nrm(ks[8], (L, DB, band_rows, H_BAND, HEAD_DIM), 1.0),
        'cache_sb_k': nrm(ks[9], (L, DB, P, H_SB, HEAD_DIM), 1.0),
        'cache_sb_v': nrm(ks[10], (L, DB, P, H_SB, HEAD_DIM), 1.0),
        'w_in': nrm(ks[11], (L, D_MODEL, IN_WIDTH), D_MODEL ** -0.5),
        'b_forget': 4.0 + nrm(ks[12], (L, H_FOX), 0.5),
        'mla_q_norm': 1.0 + nrm(ks[13], (L, MLA_Q_RANK), 0.02),
        'w_q_up': nrm(ks[14], (L, MLA_Q_RANK, H_MLA * (MLA_NOPE + MLA_ROPE)), MLA_Q_RANK ** -0.5),
        'mla_kv_norm': 1.0 + nrm(ks[15], (L, MLA_KV_RANK), 0.02),
        'w_kv_up': nrm(ks[16], (L, MLA_KV_RANK, H_MLA * (MLA_NOPE + MLA_V)), MLA_KV_RANK ** -0.5),
        'rel_bias': nrm(ks[17], (L, H_BAND, 2 * REL_CLIP + 1), 0.1),
        'w_branch': nrm(ks[18], (L, N_BRANCH, BRANCH_WIDTH, D_MODEL), BRANCH_WIDTH ** -0.5),
        'w_gate': nrm(ks[19], (L, N_BRANCH, D_MODEL, D_MODEL), D_MODEL ** -0.5),
        'b_gate': nrm(ks[20], (L, N_BRANCH, D_MODEL), 0.1),
        'w_out': nrm(ks[21], (L, D_MODEL, D_MODEL), BETA * D_MODEL ** -0.5),
        'ln1_g': 1.0 + nrm(ks[22], (L, D_MODEL), 0.02),
        'ln1_b': nrm(ks[23], (L, D_MODEL), 0.02),
        'w_router': nrm(ks[24], (D_MODEL, N_EXPERTS), D_MODEL ** -0.5),
        'router_bias': nrm(ks[25], (N_EXPERTS,), 0.01),
        'w_e_gate': nrm(ks[26], (L, N_EXPERTS, D_MODEL, D_EXPERT), D_MODEL ** -0.5),
        'w_e_up': nrm(ks[27], (L, N_EXPERTS, D_MODEL, D_EXPERT), D_MODEL ** -0.5),
        'w_e_down': nrm(ks[28], (L, N_EXPERTS, D_EXPERT, D_MODEL), BETA * D_EXPERT ** -0.5),
        'ln2_g': 1.0 + nrm(ks[29], (L, D_MODEL), 0.02),
        'ln2_b': nrm(ks[30], (L, D_MODEL), 0.02),
    }


def reference(x_prompt, x_sample, cache_fox_k, cache_fox_v, cache_fox_logf, cache_mla_latent,
              cache_mla_krope, cache_band_k, cache_band_v, cache_sb_k, cache_sb_v,
              w_in, b_forget, mla_q_norm, w_q_up, mla_kv_norm, w_kv_up, rel_bias,
              w_branch, w_gate, b_gate, w_out, ln1_g, ln1_b, w_router, router_bias,
              w_e_gate, w_e_up, w_e_down, ln2_g, ln2_b):
    S = x_prompt.shape[1]
    T = x_sample.shape[1]
    P = cache_fox_k.shape[2]
    pos_p = jnp.arange(S, dtype=jnp.int32)
    pos_s = P + jnp.arange(T, dtype=jnp.int32)
    xp, xs = x_prompt, x_sample
    prompt_states, sample_states = [], []
    for l in range(DEPTH):
        mw = (w_in[l], b_forget[l], mla_q_norm[l], w_q_up[l], mla_kv_norm[l])
        tail = (w_branch[l], w_gate[l], b_gate[l], w_out[l], ln1_g[l], ln1_b[l], w_router, router_bias,
                w_e_gate[l], w_e_up[l], w_e_down[l], ln2_g[l], ln2_b[l])
        br_p, st_p = prompt_mixers(token_rows(xp, pos_p, *mw), pos_p, w_kv_up[l], rel_bias[l])
        caches_l = (cache_fox_k[l], cache_fox_v[l], cache_fox_logf[l], cache_mla_latent[l], cache_mla_krope[l],
                    cache_band_k[l], cache_band_v[l], cache_sb_k[l], cache_sb_v[l])
        br_s, st_s = sample_mixers(token_rows(xs, pos_s, *mw), pos_s, caches_l, w_kv_up[l], rel_bias[l])
        xp = layer_tail(xp, br_p, *tail)
        xs = layer_tail(xs, br_s, *tail)
        prompt_states.append(st_p)
        sample_states.append(st_s)
    fk_p, fv_p, fl_p, lat_p, kr_p, bk_p, bv_p, sk_p, sv_p = [jnp.stack(z, axis=0) for z in zip(*prompt_states)]
    fk_s, fv_s, fl_s, lat_s, kr_s, bk_s, bv_s, sk_s, sv_s = [jnp.stack(z, axis=0) for z in zip(*sample_states)]
    return (xp, xs, fk_p, fv_p, fl_p, lat_p, kr_p, bk_p, bv_p, sk_p, sv_p,
            fk_s, fv_s, fl_s, lat_s, kr_s, bk_s, bv_s, sk_s, sv_s)
```

```python
import functools

import numpy as np
import jax
import jax.numpy as jnp
from jax import lax
from jax.experimental import pallas as pl
from jax.experimental.pallas import tpu as pltpu

F32 = jnp.float32
BF16 = jnp.bfloat16

D_MODEL = 1024
N_HEADS = 4
HEAD_DIM = 64
HEADS_WIDTH = N_HEADS * HEAD_DIM
CHUNK = 64
BAND_CHUNKS = 8
REL_CLIP = 128
MLA_NOPE = 64
MLA_ROPE = 32
MLA_V = 64
MLA_Q_RANK = 192
MLA_KV_RANK = 128
ROPE_THETA = 10000.0
N_BRANCH = 4
N_EXPERTS = 16
N_GROUPS = 4
EXPERTS_PER_GROUP = N_EXPERTS // N_GROUPS
D_EXPERT = 512
NEG_INF = -1e30
LANES = 128
MXU_DIM = 256
VMEM_LIMIT = 56 * 1024 * 1024

QKV_WIDTH = 9 * HEADS_WIDTH
OFF_CQ = QKV_WIDTH
OFF_CKV = OFF_CQ + 256
OFF_KR = OFF_CKV + MLA_KV_RANK
OFF_FA = OFF_KR + LANES
PROJ_WIDTH = OFF_FA + LANES
Q_CAT = 256

TOKEN_TILE = 512
MOE_TILE = 1024
ATT_TILE = 512
SB_SUB = 256
SAMPLE_KEY_TILE = 1024


def _dot(a, b):
    return jnp.dot(a, b, preferred_element_type=F32)


def _dot_nt(a, b):
    return lax.dot_general(a, b, (((1,), (1,)), ((), ())), preferred_element_type=F32)


def _const_spec(shape):
    nd = len(shape)
    return pl.BlockSpec(shape, lambda *_: (0,) * nd, pipeline_mode=pl.Buffered(1))


def _params(*sem):
    return pltpu.CompilerParams(dimension_semantics=sem, vmem_limit_bytes=VMEM_LIMIT)


def _log_sigmoid(x):
    return jnp.minimum(x, 0.0) - jnp.log1p(jnp.exp(-jnp.abs(x)))


def _softplus(x):
    return jnp.maximum(x, 0.0) + jnp.log1p(jnp.exp(-jnp.abs(x)))


def _layer_norm(h, g, b):
    mu = jnp.mean(h, axis=-1, keepdims=True)
    d = h - mu
    var = jnp.mean(d * d, axis=-1, keepdims=True)
    return d * lax.rsqrt(var + 1e-5) * g + b


def _head_of_lane(shape):
    return lax.broadcasted_iota(jnp.int32, shape, len(shape) - 1) // HEAD_DIM


def _in_proj_kernel(x_ref, w_ref, bf_ref, qn_ref, wq_ref, kvn_ref, wcat_ref, cos_ref, sin_ref,
                    qkv_ref, ka_ref, va_ref, kc_ref, vc_ref, kd_ref, vd_ref,
                    logf_ref, lat_ref, kr_ref, kcat_ref, qcat_ref):
    xb = x_ref[...].astype(BF16)
    proj = _dot(xb, w_ref[...])
    qkv_ref[...] = proj[:, :QKV_WIDTH].astype(BF16)
    for ref, col in ((ka_ref, 1), (va_ref, 2), (kc_ref, 4), (vc_ref, 5), (kd_ref, 7), (vd_ref, 8)):
        ref[...] = proj[:, col * HEADS_WIDTH:(col + 1) * HEADS_WIDTH]

    cos = cos_ref[...]
    sin = sin_ref[...]
    lane = lax.broadcasted_iota(jnp.int32, cos.shape, 1)
    first_half = (lane % MLA_ROPE) < (MLA_ROPE // 2)

    def rope(v):
        partner = jnp.where(first_half, pltpu.roll(v, LANES - MLA_ROPE // 2, 1),
                            pltpu.roll(v, MLA_ROPE // 2, 1))
        return v * cos + partner * sin

    cq = proj[:, OFF_CQ:OFF_CQ + 256]
    ms = jnp.sum(cq * cq, axis=-1, keepdims=True) * (1.0 / MLA_Q_RANK)
    cqn = cq * lax.rsqrt(ms + 1e-6) * qn_ref[...]
    qb = _dot(cqn.astype(BF16), wq_ref[...])
    q_rope = rope(qb[:, HEADS_WIDTH:])
    q_in = jnp.concatenate([qb[:, :HEADS_WIDTH], q_rope], axis=1) * ((MLA_NOPE + MLA_ROPE) ** -0.5)
    qcat_ref[...] = _dot(q_in.astype(BF16), wcat_ref[...]).astype(BF16)

    ckv = proj[:, OFF_CKV:OFF_CKV + MLA_KV_RANK]
    lat = ckv * lax.rsqrt(jnp.mean(ckv * ckv, axis=-1, keepdims=True) + 1e-6) * kvn_ref[...]
    lat_ref[...] = lat
    kr = rope(proj[:, OFF_KR:OFF_KR + LANES])
    kr_ref[...] = kr
    kcat_ref[...] = jnp.concatenate([lat, kr], axis=1).astype(BF16)

    logf_ref[...] = _log_sigmoid(proj[:, OFF_FA:OFF_FA + LANES] + bf_ref[...])


def _in_proj(x, w, bf, qn, wq, kvn, wcat, cos, sin):
    n = x.shape[0]
    tm = TOKEN_TILE
    row = lambda width: pl.BlockSpec((tm, width), lambda i: (i, 0))
    out_shape = ([jax.ShapeDtypeStruct((n, QKV_WIDTH), BF16)]
                 + [jax.ShapeDtypeStruct((n, HEADS_WIDTH), F32)] * 6
                 + [jax.ShapeDtypeStruct((n, LANES), F32)] * 3
                 + [jax.ShapeDtypeStruct((n, 2 * LANES), BF16),
                    jax.ShapeDtypeStruct((n, N_HEADS * Q_CAT), BF16)])
    out_specs = ([row(QKV_WIDTH)] + [row(HEADS_WIDTH)] * 6 + [row(LANES)] * 3
                 + [row(2 * LANES), row(N_HEADS * Q_CAT)])
    return pl.pallas_call(
        _in_proj_kernel,
        out_shape=out_shape,
        grid=(n // tm,),
        in_specs=[row(D_MODEL), _const_spec(w.shape), _const_spec(bf.shape), _const_spec(qn.shape),
                  _const_spec(wq.shape), _const_spec(kvn.shape), _const_spec(wcat.shape),
                  row(LANES), row(LANES)],
        out_specs=out_specs,
        compiler_params=_params("parallel"),
        name="in_proj",
    )(x, w, bf, qn, wq, kvn, wcat, cos, sin)


def _cumsum_kernel(x_ref, o_ref):
    r, t = x_ref.shape
    row = lax.broadcasted_iota(jnp.int32, (LANES, LANES), 0)
    col = lax.broadcasted_iota(jnp.int32, (LANES, LANES), 1)
    tri = jnp.where(row <= col, 1.0, 0.0).astype(BF16)
    carry = jnp.zeros((r, 1), F32)
    for c in range(t // LANES):
        xc = x_ref[:, c * LANES:(c + 1) * LANES]
        hi = xc.astype(BF16)
        r1 = xc - hi.astype(F32)
        mid = r1.astype(BF16)
        lo = (r1 - mid.astype(F32)).astype(BF16)
        y = _dot(hi, tri) + _dot(mid, tri) + _dot(lo, tri) + carry
        o_ref[:, c * LANES:(c + 1) * LANES] = y
        carry = y[:, LANES - 1:LANES]


def _cumsum_rows(x):
    return pl.pallas_call(
        _cumsum_kernel,
        out_shape=jax.ShapeDtypeStruct(x.shape, F32),
        name="cumsum",
    )(x)


def _online_softmax_update(s, v, m_ref, l_ref, acc_ref, h):
    m_prev = m_ref[h]
    m_new = jnp.maximum(m_prev, jnp.max(s, axis=1, keepdims=True))
    alpha = jnp.exp(m_prev - m_new)
    p = jnp.exp(s - m_new)
    l_ref[h] = alpha * l_ref[h] + jnp.sum(p, axis=1, keepdims=True)
    acc_ref[h] = alpha * acc_ref[h] + _dot(p.astype(BF16), v)
    m_ref[h] = m_new


def _init_softmax_state(m_ref, l_ref, acc_ref):
    m_ref[...] = jnp.full(m_ref.shape, NEG_INF, F32)
    l_ref[...] = jnp.zeros(l_ref.shape, F32)
    acc_ref[...] = jnp.zeros(acc_ref.shape, F32)


def _store_head_masked_q(q_ref, qs_ref, rows):
    q = q_ref[...] * (HEAD_DIM ** -0.5)
    head = _head_of_lane(q.shape)
    for h in range(N_HEADS):
        qs_ref[h * rows:(h + 1) * rows, :] = jnp.where(head == h, q, jnp.zeros_like(q))


def _merge_heads(per_head):
    head = _head_of_lane(per_head[0].shape)
    out = jnp.zeros_like(per_head[0])
    for h in range(N_HEADS):
        out = jnp.where(head == h, per_head[h], out)
    return out


def _fox_prompt_kernel(q_ref, k_ref, v_ref, cq_ref, ck_ref, o_ref, qs_ref, m_ref, l_ref, acc_ref):
    tq = q_ref.shape[0]
    qi = pl.program_id(1)
    ki = pl.program_id(2)

    @pl.when(ki == 0)
    def _():
        _init_softmax_state(m_ref, l_ref, acc_ref)
        _store_head_masked_q(q_ref, qs_ref, tq)

    def step(diagonal):
        k = k_ref[...]
        v = v_ref[...]
        cq = cq_ref[...]
        ck = ck_ref[...]
        for h in range(N_HEADS):
            s = _dot_nt(qs_ref[h * tq:(h + 1) * tq, :], k) + cq[:, h:h + 1] - ck[h:h + 1, :]
            if diagonal:
                row = lax.broadcasted_iota(jnp.int32, s.shape, 0)
                col = lax.broadcasted_iota(jnp.int32, s.shape, 1)
                s = jnp.where(col <= row, s, NEG_INF)
            _online_softmax_update(s, v, m_ref, l_ref, acc_ref, h)

    pl.when(ki < qi)(lambda: step(False))
    pl.when(ki == qi)(lambda: step(True))

    @pl.when(ki == pl.num_programs(2) - 1)
    def _():
        o_ref[...] = _merge_heads([acc_ref[h] / l_ref[h] for h in range(N_HEADS)]).astype(BF16)


def _fox_prompt(qkv, c_tok, c_head, batch, seq):
    t = ATT_TILE
    nt = seq // t
    return pl.pallas_call(
        _fox_prompt_kernel,
        out_shape=jax.ShapeDtypeStruct((batch * seq, HEADS_WIDTH), BF16),
        grid=(batch, nt, nt),
        in_specs=[pl.BlockSpec((t, HEADS_WIDTH), lambda b, i, j: (b * nt + i, 0)),
                  pl.BlockSpec((t, HEADS_WIDTH), lambda b, i, j: (b * nt + jnp.minimum(i, j), 1)),
                  pl.BlockSpec((t, HEADS_WIDTH), lambda b, i, j: (b * nt + jnp.minimum(i, j), 2)),
                  pl.BlockSpec((None, t, N_HEADS), lambda b, i, j: (b, i, 0)),
                  pl.BlockSpec((None, N_HEADS, t), lambda b, i, j: (b, 0, jnp.minimum(i, j)))],
        out_specs=pl.BlockSpec((t, HEADS_WIDTH), lambda b, i, j: (b * nt + i, 0)),
        scratch_shapes=[pltpu.VMEM((N_HEADS * t, HEADS_WIDTH), BF16),
                        pltpu.VMEM((N_HEADS, t, 1), F32), pltpu.VMEM((N_HEADS, t, 1), F32),
                        pltpu.VMEM((N_HEADS, t, HEADS_WIDTH), F32)],
        compiler_params=_params("parallel", "parallel", "arbitrary"),
        name="fox_prompt",
    )(qkv, qkv, qkv, c_tok, c_head)


def _mla_prompt_kernel(q_ref, k_ref, wv_ref, o_ref, m_ref, l_ref, acc_ref):
    qi = pl.program_id(1)
    ki = pl.program_id(2)

    @pl.when(ki == 0)
    def _():
        _init_softmax_state(m_ref, l_ref, acc_ref)

    def step(diagonal):
        k = k_ref[...]
        for h in range(N_HEADS):
            s = _dot_nt(q_ref[:, h * Q_CAT:(h + 1) * Q_CAT], k)
            if diagonal:
                row = lax.broadcasted_iota(jnp.int32, s.shape, 0)
                col = lax.broadcasted_iota(jnp.int32, s.shape, 1)
                s = jnp.where(col // CHUNK <= row // CHUNK, s, NEG_INF)
            _online_softmax_update(s, k, m_ref, l_ref, acc_ref, h)

    pl.when(ki < qi)(lambda: step(False))
    pl.when(ki == qi)(lambda: step(True))

    @pl.when(ki == pl.num_programs(2) - 1)
    def _():
        out = None
        for h in range(N_HEADS):
            term = _dot((acc_ref[h] / l_ref[h]).astype(BF16), wv_ref[h])
            out = term if out is None else out + term
        o_ref[...] = out.astype(BF16)


def _mla_prompt(qcat, kcat, wv_pad, batch, seq):
    t = ATT_TILE
    nt = seq // t
    return pl.pallas_call(
        _mla_prompt_kernel,
        out_shape=jax.ShapeDtypeStruct((batch * seq, HEADS_WIDTH), BF16),
        grid=(batch, nt, nt),
        in_specs=[pl.BlockSpec((t, N_HEADS * Q_CAT), lambda b, i, j: (b * nt + i, 0)),
                  pl.BlockSpec((t, Q_CAT), lambda b, i, j: (b * nt + jnp.minimum(i, j), 0)),
                  _const_spec(wv_pad.shape)],
        out_specs=pl.BlockSpec((t, HEADS_WIDTH), lambda b, i, j: (b * nt + i, 0)),
        scratch_shapes=[pltpu.VMEM((N_HEADS, t, 1), F32), pltpu.VMEM((N_HEADS, t, 1), F32),
                        pltpu.VMEM((N_HEADS, t, Q_CAT), F32)],
        compiler_params=_params("parallel", "parallel", "arbitrary"),
        name="mla_prompt",
    )(qcat, kcat, wv_pad)


def _strict_upper_ones(n):
    row = lax.broadcasted_iota(jnp.int32, (n, n), 0)
    col = lax.broadcasted_iota(jnp.int32, (n, n), 1)
    return jnp.where(row > col, 1.0, 0.0).astype(BF16)


def _sb_block(z, v, u, rest, mask):
    sp = _softplus(z)
    log_1m = -sp
    if mask is not None:
        log_1m = jnp.where(mask, log_1m, 0.0)
    hi = log_1m.astype(BF16)
    lo = (log_1m - hi.astype(F32)).astype(BF16)
    later = _dot(hi, u) + _dot(lo, u) + rest
    a = jnp.exp(z - sp + later)
    if mask is not None:
        a = jnp.where(mask, a, 0.0)
    return _dot(a.astype(BF16), v), rest + jnp.sum(log_1m, axis=1, keepdims=True)


def _sb_prompt_kernel(q_ref, k_ref, v_ref, o_ref, qs_ref, u_ref, rest_ref, acc_ref):
    tq = q_ref.shape[0]
    tk = k_ref.shape[0]
    ki = pl.program_id(2)
    qi = pl.program_id(1)

    @pl.when(ki == 0)
    def _():
        _store_head_masked_q(q_ref, qs_ref, tq)
        u_ref[...] = _strict_upper_ones(SB_SUB)
        rest_ref[...] = jnp.zeros(rest_ref.shape, F32)
        acc_ref[...] = jnp.zeros(acc_ref.shape, F32)

    def step(diagonal):
        u = u_ref[...]
        for sub in reversed(range(tk // SB_SUB)):
            k = k_ref[sub * SB_SUB:(sub + 1) * SB_SUB, :]
            v = v_ref[sub * SB_SUB:(sub + 1) * SB_SUB, :]
            for h in range(N_HEADS):
                z = _dot_nt(qs_ref[h * tq:(h + 1) * tq, :], k)
                mask = None
                if diagonal:
                    row = lax.broadcasted_iota(jnp.int32, z.shape, 0)
                    col = lax.broadcasted_iota(jnp.int32, z.shape, 1) + sub * SB_SUB
                    mask = col < row
                pv, rest = _sb_block(z, v, u, rest_ref[h], mask)
                acc_ref[h] += pv
                rest_ref[h] = rest

    pl.when(ki == 0)(lambda: step(True))
    pl.when(jnp.logical_and(ki > 0, ki <= qi))(lambda: step(False))

    @pl.when(ki == pl.num_programs(2) - 1)
    def _():
        o_ref[...] = _merge_heads([acc_ref[h] for h in range(N_HEADS)]).astype(BF16)


def _sb_prompt(qkv, batch, seq):
    t = ATT_TILE
    nt = seq // t
    key_blk = lambda b, i, j: b * nt + jnp.maximum(i - j, 0)
    return pl.pallas_call(
        _sb_prompt_kernel,
        out_shape=jax.ShapeDtypeStruct((batch * seq, HEADS_WIDTH), BF16),
        grid=(batch, nt, nt),
        in_specs=[pl.BlockSpec((t, HEADS_WIDTH), lambda b, i, j: (b * nt + i, 6)),
                  pl.BlockSpec((t, HEADS_WIDTH), lambda b, i, j: (key_blk(b, i, j), 7)),
                  pl.BlockSpec((t, HEADS_WIDTH), lambda b, i, j: (key_blk(b, i, j), 8))],
        out_specs=pl.BlockSpec((t, HEADS_WIDTH), lambda b, i, j: (b * nt + i, 0)),
        scratch_shapes=[pltpu.VMEM((N_HEADS * t, HEADS_WIDTH), BF16),
                        pltpu.VMEM((SB_SUB, SB_SUB), BF16),
                        pltpu.VMEM((N_HEADS, t, 1), F32),
                        pltpu.VMEM((N_HEADS, t, HEADS_WIDTH), F32)],
        compiler_params=_params("parallel", "parallel", "arbitrary"),
        name="sb_prompt",
    )(qkv, qkv, qkv)


def _band_prompt_kernel(q_ref, kp_ref, kc_ref, vp_ref, vc_ref, bias_ref, o_ref, qs_ref):
    tq = q_ref.shape[0]
    qi = pl.program_id(1)
    _store_head_masked_q(q_ref, qs_ref, tq)
    k = jnp.concatenate([kp_ref[...], kc_ref[...]], axis=0)
    v = jnp.concatenate([vp_ref[...], vc_ref[...]], axis=0)
    row_chunk = lax.broadcasted_iota(jnp.int32, (tq, 2 * tq), 0) // CHUNK
    col = lax.broadcasted_iota(jnp.int32, (tq, 2 * tq), 1)
    key_chunk = col // CHUNK - tq // CHUNK
    valid = (key_chunk <= row_chunk) & (key_chunk >= row_chunk - BAND_CHUNKS)
    valid = valid & ((col >= tq) | (qi > 0))
    outs = []
    for h in range(N_HEADS):
        s = _dot_nt(qs_ref[h * tq:(h + 1) * tq, :], k) + bias_ref[h]
        s = jnp.where(valid, s, NEG_INF)
        p = jnp.exp(s - jnp.max(s, axis=1, keepdims=True))
        denom = jnp.sum(p, axis=1, keepdims=True)
        outs.append(_dot(p.astype(BF16), v) / denom)
    o_ref[...] = _merge_heads(outs).astype(BF16)


def _band_prompt(qkv, bias_tile, batch, seq):
    t = BAND_CHUNKS * CHUNK
    nt = seq // t
    prev = lambda b, i: b * nt + jnp.maximum(i - 1, 0)
    return pl.pallas_call(
        _band_prompt_kernel,
        out_shape=jax.ShapeDtypeStruct((batch * seq, HEADS_WIDTH), BF16),
        grid=(batch, nt),
        in_specs=[pl.BlockSpec((t, HEADS_WIDTH), lambda b, i: (b * nt + i, 3)),
                  pl.BlockSpec((t, HEADS_WIDTH), lambda b, i: (prev(b, i), 4)),
                  pl.BlockSpec((t, HEADS_WIDTH), lambda b, i: (b * nt + i, 4)),
                  pl.BlockSpec((t, HEADS_WIDTH), lambda b, i: (prev(b, i), 5)),
                  pl.BlockSpec((t, HEADS_WIDTH), lambda b, i: (b * nt + i, 5)),
                  _const_spec(bias_tile.shape)],
        out_specs=pl.BlockSpec((t, HEADS_WIDTH), lambda b, i: (b * nt + i, 0)),
        scratch_shapes=[pltpu.VMEM((N_HEADS * t, HEADS_WIDTH), BF16)],
        compiler_params=_params("parallel", "parallel"),
        name="band_prompt",
    )(qkv, qkv, qkv, qkv, qkv, bias_tile)


def _rows_per_head(x, rows):
    return jnp.concatenate([jnp.broadcast_to(x[h:h + 1, :], (rows, x.shape[1])) for h in range(N_HEADS)],
                           axis=0)


def _unstack_heads(acc, rows):
    return _merge_heads([acc[h * rows:(h + 1) * rows, :] for h in range(N_HEADS)])


def _fox_sample_kernel(q_ref, kc_ref, vc_ref, kn_ref, vn_ref, cq_ref, ckc_ref, ckn_ref, o_ref,
                       qs_ref, m_ref, l_ref, acc_ref):
    t = q_ref.shape[0]
    ki = pl.program_id(1)

    @pl.when(ki == 0)
    def _():
        _init_softmax_state(m_ref, l_ref, acc_ref)
        _store_head_masked_q(q_ref, qs_ref, t)

    qs = qs_ref[...]
    cq = cq_ref[...]
    s = _dot_nt(qs, kc_ref[...].astype(BF16)) + cq - _rows_per_head(ckc_ref[...], t)
    _online_softmax_update(s, vc_ref[...].astype(BF16), m_ref, l_ref, acc_ref, 0)

    @pl.when(ki == pl.num_programs(1) - 1)
    def _():
        s = _dot_nt(qs, kn_ref[...]) + cq - _rows_per_head(ckn_ref[:, :t], t)
        row = lax.broadcasted_iota(jnp.int32, s.shape, 0) % t
        col = lax.broadcasted_iota(jnp.int32, s.shape, 1)
        s = jnp.where(col <= row, s, NEG_INF)
        _online_softmax_update(s, vn_ref[...], m_ref, l_ref, acc_ref, 0)
        o_ref[...] = _unstack_heads(acc_ref[0] / l_ref[0], t).astype(BF16)


def _fox_sample(qkv, cache_k, cache_v, cq_rows, c_head, layer, n_prompt, dec_batch, t):
    past = cache_k.shape[2]
    tk = min(SAMPLE_KEY_TILE, past)
    nk = past // tk
    new_blk = n_prompt // t
    return pl.pallas_call(
        _fox_sample_kernel,
        out_shape=jax.ShapeDtypeStruct((dec_batch * t, HEADS_WIDTH), BF16),
        grid=(dec_batch, nk),
        in_specs=[pl.BlockSpec((t, HEADS_WIDTH), lambda b, j: (new_blk + b, 0)),
                  pl.BlockSpec((None, None, tk, HEADS_WIDTH), lambda b, j: (layer, b, j, 0)),
                  pl.BlockSpec((None, None, tk, HEADS_WIDTH), lambda b, j: (layer, b, j, 0)),
                  pl.BlockSpec((t, HEADS_WIDTH), lambda b, j: (new_blk + b, 1)),
                  pl.BlockSpec((t, HEADS_WIDTH), lambda b, j: (new_blk + b, 2)),
                  pl.BlockSpec((None, N_HEADS * t, 1), lambda b, j: (b, 0, 0)),
                  pl.BlockSpec((None, N_HEADS, tk), lambda b, j: (b, 0, j)),
                  pl.BlockSpec((None, N_HEADS, LANES), lambda b, j: (b, 0, past // LANES))],
        out_specs=pl.BlockSpec((t, HEADS_WIDTH), lambda b, j: (b, 0)),
        scratch_shapes=[pltpu.VMEM((N_HEADS * t, HEADS_WIDTH), BF16),
                        pltpu.VMEM((1, N_HEADS * t, 1), F32), pltpu.VMEM((1, N_HEADS * t, 1), F32),
                        pltpu.VMEM((1, N_HEADS * t, HEADS_WIDTH), F32)],
        compiler_params=_params("parallel", "arbitrary"),
        name="fox_sample",
    )(qkv, cache_k, cache_v, qkv, qkv, cq_rows, c_head, c_head)


def _mla_sample_kernel(q_ref, latc_ref, krc_ref, kn_ref, wv_ref, o_ref,
                       qa_ref, qr_ref, m_ref, l_ref, acc_ref, *, past):
    t = q_ref.shape[0]
    ki = pl.program_id(1)

    @pl.when(ki == 0)
    def _():
        _init_softmax_state(m_ref, l_ref, acc_ref)
        for h in range(N_HEADS):
            qa_ref[h * t:(h + 1) * t, :] = q_ref[:, h * Q_CAT:h * Q_CAT + MLA_KV_RANK]
            qr_ref[h * t:(h + 1) * t, :] = q_ref[:, h * Q_CAT + MLA_KV_RANK:
                                                 h * Q_CAT + MLA_KV_RANK + MLA_ROPE]

    lat = latc_ref[...].astype(BF16)
    s = _dot_nt(qa_ref[...], lat) + _dot_nt(qr_ref[...], krc_ref[...].astype(BF16))
    _online_softmax_update(s, lat, m_ref, l_ref, acc_ref, 0)

    @pl.when(ki == pl.num_programs(1) - 1)
    def _():
        lat_n = kn_ref[:, :MLA_KV_RANK]
        s = _dot_nt(qa_ref[...], lat_n) + _dot_nt(qr_ref[...],
                                                  kn_ref[:, MLA_KV_RANK:MLA_KV_RANK + MLA_ROPE])
        q_pos = past + lax.broadcasted_iota(jnp.int32, s.shape, 0) % t
        k_pos = past + lax.broadcasted_iota(jnp.int32, s.shape, 1)
        s = jnp.where(k_pos // CHUNK <= q_pos // CHUNK, s, NEG_INF)
        _online_softmax_update(s, lat_n, m_ref, l_ref, acc_ref, 0)
        o = (acc_ref[0] / l_ref[0]).astype(BF16)
        out = None
        for h in range(N_HEADS):
            term = _dot(o[h * t:(h + 1) * t, :], wv_ref[h, :MLA_KV_RANK, :])
            out = term if out is None else out + term
        o_ref[...] = out.astype(BF16)


def _mla_sample(qcat, kcat, cache_lat, cache_kr, wv_pad, layer, n_prompt, dec_batch, t):
    past = cache_lat.shape[2]
    tk = min(SAMPLE_KEY_TILE, past)
    nk = past // tk
    new_blk = n_prompt // t
    return pl.pallas_call(
        functools.partial(_mla_sample_kernel, past=past),
        out_shape=jax.ShapeDtypeStruct((dec_batch * t, HEADS_WIDTH), BF16),
        grid=(dec_batch, nk),
        in_specs=[pl.BlockSpec((t, N_HEADS * Q_CAT), lambda b, j: (new_blk + b, 0)),
                  pl.BlockSpec((None, None, tk, MLA_KV_RANK), lambda b, j: (layer, b, j, 0)),
                  pl.BlockSpec((None, None, tk, MLA_ROPE), lambda b, j: (layer, b, j, 0)),
                  pl.BlockSpec((t, Q_CAT), lambda b, j: (new_blk + b, 0)),
                  _const_spec(wv_pad.shape)],
        out_specs=pl.BlockSpec((t, HEADS_WIDTH), lambda b, j: (b, 0)),
        scratch_shapes=[pltpu.VMEM((N_HEADS * t, MLA_KV_RANK), BF16),
                        pltpu.VMEM((N_HEADS * t, MLA_ROPE), BF16),
                        pltpu.VMEM((1, N_HEADS * t, 1), F32), pltpu.VMEM((1, N_HEADS * t, 1), F32),
                        pltpu.VMEM((1, N_HEADS * t, MLA_KV_RANK), F32)],
        compiler_params=_params("parallel", "arbitrary"),
        name="mla_sample",
    )(qcat, cache_lat, cache_kr, kcat, wv_pad)


def _band_sample_kernel(q_ref, kc_ref, vc_ref, kn_ref, vn_ref, biasc_ref, biasn_ref, o_ref, qs_ref,
                        *, past):
    t = q_ref.shape[0]
    n_band = kc_ref.shape[0]
    _store_head_masked_q(q_ref, qs_ref, t)
    qs = qs_ref[...]

    def scores(k, bias, first_pos):
        s = _dot_nt(qs, k) + bias
        q_chunk = (past + lax.broadcasted_iota(jnp.int32, s.shape, 0) % t) // CHUNK
        k_pos = first_pos + lax.broadcasted_iota(jnp.int32, s.shape, 1)
        k_chunk = k_pos // CHUNK
        valid = (k_pos >= 0) & (k_chunk <= q_chunk) & (k_chunk >= q_chunk - BAND_CHUNKS)
        return jnp.where(valid, s, NEG_INF)

    s_c = scores(kc_ref[...].astype(BF16), biasc_ref[...], past - n_band)
    s_n = scores(kn_ref[...], biasn_ref[...], past)
    m = jnp.maximum(jnp.max(s_c, axis=1, keepdims=True), jnp.max(s_n, axis=1, keepdims=True))
    p_c = jnp.exp(s_c - m)
    p_n = jnp.exp(s_n - m)
    denom = jnp.sum(p_c, axis=1, keepdims=True) + jnp.sum(p_n, axis=1, keepdims=True)
    acc = _dot(p_c.astype(BF16), vc_ref[...].astype(BF16)) + _dot(p_n.astype(BF16), vn_ref[...])
    o_ref[...] = _unstack_heads(acc / denom, t).astype(BF16)


def _band_sample(qkv, cache_k, cache_v, bias_cache, bias_new, layer, past, n_prompt, dec_batch, t):
    n_band = cache_k.shape[2]
    new_blk = n_prompt // t
    return pl.pallas_call(
        functools.partial(_band_sample_kernel, past=past),
        out_shape=jax.ShapeDtypeStruct((dec_batch * t, HEADS_WIDTH), BF16),
        grid=(dec_batch,),
        in_specs=[pl.BlockSpec((t, HEADS_WIDTH), lambda b: (new_blk + b, 3)),
                  pl.BlockSpec((None, None, n_band, HEADS_WIDTH), lambda b: (layer, b, 0, 0)),
                  pl.BlockSpec((None, None, n_band, HEADS_WIDTH), lambda b: (layer, b, 0, 0)),
                  pl.BlockSpec((t, HEADS_WIDTH), lambda b: (new_blk + b, 4)),
                  pl.BlockSpec((t, HEADS_WIDTH), lambda b: (new_blk + b, 5)),
                  _const_spec(bias_cache.shape), _const_spec(bias_new.shape)],
        out_specs=pl.BlockSpec((t, HEADS_WIDTH), lambda b: (b, 0)),
        scratch_shapes=[pltpu.VMEM((N_HEADS * t, HEADS_WIDTH), BF16)],
        compiler_params=_params("parallel"),
        name="band_sample",
    )(qkv, cache_k, cache_v, qkv, qkv, bias_cache, bias_new)


def _sb_sample_kernel(q_ref, kc_ref, vc_ref, kn_ref, vn_ref, o_ref, qs_ref, u_ref, rest_ref, acc_ref):
    t = q_ref.shape[0]
    tk = kc_ref.shape[0]
    ki = pl.program_id(1)

    @pl.when(ki == 0)
    def _():
        _store_head_masked_q(q_ref, qs_ref, t)
        u = _strict_upper_ones(SB_SUB)
        u_ref[...] = u
        z = _dot_nt(qs_ref[...], kn_ref[...])
        row = lax.broadcasted_iota(jnp.int32, z.shape, 0) % t
        col = lax.broadcasted_iota(jnp.int32, z.shape, 1)
        pv, rest = _sb_block(z, vn_ref[...], u[:t, :t], jnp.zeros((z.shape[0], 1), F32), col < row)
        acc_ref[...] = pv
        rest_ref[...] = rest

    qs = qs_ref[...]
    u = u_ref[...]
    for sub in reversed(range(tk // SB_SUB)):
        k = kc_ref[sub * SB_SUB:(sub + 1) * SB_SUB, :].astype(BF16)
        v = vc_ref[sub * SB_SUB:(sub + 1) * SB_SUB, :].astype(BF16)
        pv, rest = _sb_block(_dot_nt(qs, k), v, u, rest_ref[...], None)
        acc_ref[...] += pv
        rest_ref[...] = rest

    @pl.when(ki == pl.num_programs(1) - 1)
    def _():
        o_ref[...] = _unstack_heads(acc_ref[...], t).astype(BF16)


def _sb_sample(qkv, cache_k, cache_v, layer, n_prompt, dec_batch, t):
    past = cache_k.shape[2]
    tk = min(SAMPLE_KEY_TILE, past)
    nk = past // tk
    new_blk = n_prompt // t
    return pl.pallas_call(
        _sb_sample_kernel,
        out_shape=jax.ShapeDtypeStruct((dec_batch * t, HEADS_WIDTH), BF16),
        grid=(dec_batch, nk),
        in_specs=[pl.BlockSpec((t, HEADS_WIDTH), lambda b, j: (new_blk + b, 6)),
                  pl.BlockSpec((None, None, tk, HEADS_WIDTH), lambda b, j: (layer, b, nk - 1 - j, 0)),
                  pl.BlockSpec((None, None, tk, HEADS_WIDTH), lambda b, j: (layer, b, nk - 1 - j, 0)),
                  pl.BlockSpec((t, HEADS_WIDTH), lambda b, j: (new_blk + b, 7)),
                  pl.BlockSpec((t, HEADS_WIDTH), lambda b, j: (new_blk + b, 8))],
        out_specs=pl.BlockSpec((t, HEADS_WIDTH), lambda b, j: (b, 0)),
        scratch_shapes=[pltpu.VMEM((N_HEADS * t, HEADS_WIDTH), BF16),
                        pltpu.VMEM((SB_SUB, SB_SUB), BF16),
                        pltpu.VMEM((N_HEADS * t, 1), F32),
                        pltpu.VMEM((N_HEADS * t, HEADS_WIDTH), F32)],
        compiler_params=_params("parallel", "arbitrary"),
        name="sb_sample",
    )(qkv, cache_k, cache_v, qkv, qkv)


def _merge_kernel(x_ref, oa_ref, ob_ref, oc_ref, od_ref, wg_ref, bg_ref, wb_ref, wo_ref,
                  g_ref, b_ref, wr_ref, y_ref, logit_ref, *, alpha):
    x = x_ref[...]
    xb = x.astype(BF16)
    merged = None
    for n, o_ref in enumerate((oa_ref, ob_ref, oc_ref, od_ref)):
        gate = jax.nn.sigmoid(_dot(xb, wg_ref[n]) + bg_ref[n])
        term = gate * _dot(o_ref[...], wb_ref[n])
        merged = term if merged is None else merged + term
    h = alpha * x + _dot(merged.astype(BF16), wo_ref[...])
    y = _layer_norm(h, g_ref[...], b_ref[...])
    y_ref[...] = y
    logit_ref[...] = _dot(y.astype(BF16), wr_ref[...])


def _merge(x, branches, wg, bg, wb, wo, g, b, wr, alpha):
    n = x.shape[0]
    tm = TOKEN_TILE
    row = lambda width: pl.BlockSpec((tm, width), lambda i: (i, 0))
    return pl.pallas_call(
        functools.partial(_merge_kernel, alpha=alpha),
        out_shape=[jax.ShapeDtypeStruct((n, D_MODEL), F32), jax.ShapeDtypeStruct((n, LANES), F32)],
        grid=(n // tm,),
        in_specs=[row(D_MODEL)] + [row(HEADS_WIDTH)] * 4
                 + [_const_spec(a.shape) for a in (wg, bg, wb, wo, g, b, wr)],
        out_specs=[row(D_MODEL), row(LANES)],
        compiler_params=_params("parallel"),
        name="merge",
    )(x, *branches, wg, bg, wb, wo, g, b, wr)


def _route_kernel(logit_ref, bias_ref, comb_ref):
    scores = jax.nn.sigmoid(logit_ref[...])
    sel = scores + bias_ref[...]
    rows = [sel[e:e + 1, :] for e in range(N_EXPERTS)]
    best = None
    best_group = None
    for g in range(N_GROUPS):
        m = rows[g * EXPERTS_PER_GROUP:(g + 1) * EXPERTS_PER_GROUP]
        top2 = None
        for i in range(EXPERTS_PER_GROUP):
            for j in range(i + 1, EXPERTS_PER_GROUP):
                pair = m[i] + m[j]
                top2 = pair if top2 is None else jnp.maximum(top2, pair)
        if best is None:
            best, best_group = top2, jnp.zeros(top2.shape, jnp.int32)
        else:
            better = top2 > best
            best = jnp.where(better, top2, best)
            best_group = jnp.where(better, g, best_group)
    weights = []
    total = None
    for e in range(N_EXPERTS):
        g, i = divmod(e, EXPERTS_PER_GROUP)
        rank = jnp.zeros(best.shape, jnp.int32)
        for j in range(EXPERTS_PER_GROUP):
            if j == i:
                continue
            other = rows[g * EXPERTS_PER_GROUP + j]
            ahead = (other >= rows[e]) if j < i else (other > rows[e])
            rank = rank + ahead.astype(jnp.int32)
        chosen = (best_group == g) & (rank < 2)
        w = jnp.where(chosen, scores[e:e + 1, :], 0.0)
        weights.append(w)
        total = w if total is None else total + w
    comb_ref[...] = jnp.concatenate(weights, axis=0) / total


def _route(logits_t, bias_col):
    return pl.pallas_call(
        _route_kernel,
        out_shape=jax.ShapeDtypeStruct(logits_t.shape, F32),
        compiler_params=pltpu.CompilerParams(vmem_limit_bytes=VMEM_LIMIT),
        name="route",
    )(logits_t, bias_col)


def _moe_kernel(x_ref, comb_ref, wg_ref, wu_ref, wd_ref, g_ref, b_ref, y_ref, xb_ref, acc_ref, *, alpha):
    e = pl.program_id(1)

    @pl.when(e == 0)
    def _():
        xb_ref[...] = x_ref[...].astype(BF16)
        acc_ref[...] = jnp.zeros(acc_ref.shape, F32)

    xb = xb_ref[...]
    hidden = jax.nn.silu(_dot(xb, wg_ref[...])) * _dot(xb, wu_ref[...])
    out = _dot(hidden.astype(BF16), wd_ref[...])
    comb = comb_ref[...]
    lane = lax.broadcasted_iota(jnp.int32, comb.shape, 1)
    weight = jnp.sum(jnp.where(lane == e, comb, 0.0), axis=1, keepdims=True)
    acc_ref[...] += weight * out

    @pl.when(e == pl.num_programs(1) - 1)
    def _():
        y_ref[...] = _layer_norm(alpha * x_ref[...] + acc_ref[...], g_ref[...], b_ref[...])


def _moe(x, comb, wg, wu, wd, g, b, alpha):
    n = x.shape[0]
    tm = MOE_TILE
    return pl.pallas_call(
        functools.partial(_moe_kernel, alpha=alpha),
        out_shape=jax.ShapeDtypeStruct((n, D_MODEL), F32),
        grid=(n // tm, N_EXPERTS),
        in_specs=[pl.BlockSpec((tm, D_MODEL), lambda i, e: (i, 0)),
                  pl.BlockSpec((tm, N_EXPERTS), lambda i, e: (i, 0)),
                  pl.BlockSpec((None, D_MODEL, D_EXPERT), lambda i, e: (e, 0, 0)),
                  pl.BlockSpec((None, D_MODEL, D_EXPERT), lambda i, e: (e, 0, 0)),
                  pl.BlockSpec((None, D_EXPERT, D_MODEL), lambda i, e: (e, 0, 0)),
                  _const_spec(g.shape), _const_spec(b.shape)],
        out_specs=pl.BlockSpec((tm, D_MODEL), lambda i, e: (i, 0)),
        scratch_shapes=[pltpu.VMEM((tm, D_MODEL), BF16), pltpu.VMEM((tm, D_MODEL), F32)],
        compiler_params=_params("parallel", "arbitrary"),
        name="moe",
    )(x, comb, wg, wu, wd, g, b)


IN_SIZES = (HEADS_WIDTH,) * 3 + (N_HEADS, MLA_Q_RANK, MLA_KV_RANK, MLA_ROPE) + (HEADS_WIDTH,) * 6


def _pad_cols(a, width):
    return jnp.pad(a, ((0, 0), (0, width - a.shape[1])))


def _prep_in_proj_weights(w_in, b_forget, q_norm, w_q_up, w_kv_up):
    offs = np.cumsum(IN_SIZES)[:-1].tolist()
    qa, ka, va, fa, cq, ckv, kr, qc, kc, vc, qd, kd, vd = jnp.split(w_in, offs, axis=1)
    w = jnp.concatenate([qa, ka, va, qc, kc, vc, qd, kd, vd, _pad_cols(cq, 256), ckv,
                         _pad_cols(kr, LANES), _pad_cols(fa, LANES)], axis=1).astype(BF16)
    bf = _pad_cols(b_forget[None, :], LANES)
    qn = _pad_cols(q_norm[None, :], 256)
    wq3 = w_q_up.reshape(MLA_Q_RANK, N_HEADS, MLA_NOPE + MLA_ROPE)
    wq = jnp.concatenate([wq3[:, :, :MLA_NOPE].reshape(MLA_Q_RANK, -1),
                          wq3[:, :, MLA_NOPE:].reshape(MLA_Q_RANK, -1)], axis=1)
    wq = jnp.pad(wq, ((0, 256 - MLA_Q_RANK), (0, 0))).astype(BF16)
    wkv = w_kv_up.reshape(MLA_KV_RANK, N_HEADS, MLA_NOPE + MLA_V)
    wcat = jnp.zeros((HEADS_WIDTH + N_HEADS * MLA_ROPE, N_HEADS * Q_CAT), F32)
    wv_pad = jnp.zeros((N_HEADS, Q_CAT, HEADS_WIDTH), F32)
    eye = jnp.eye(MLA_ROPE, dtype=F32)
    for h in range(N_HEADS):
        wcat = wcat.at[h * MLA_NOPE:(h + 1) * MLA_NOPE, h * Q_CAT:h * Q_CAT + MLA_KV_RANK].set(
            wkv[:, h, :MLA_NOPE].T)
        wcat = wcat.at[HEADS_WIDTH + h * MLA_ROPE:HEADS_WIDTH + (h + 1) * MLA_ROPE,
                       h * Q_CAT + MLA_KV_RANK:h * Q_CAT + MLA_KV_RANK + MLA_ROPE].set(eye)
        wv_pad = wv_pad.at[h, :MLA_KV_RANK, h * MLA_V:(h + 1) * MLA_V].set(wkv[:, h, MLA_NOPE:])
    return w, bf, qn, wq, wcat.astype(BF16), wv_pad.astype(BF16)


def _rope_tables(pos):
    half = MLA_ROPE // 2
    inv = ROPE_THETA ** (-jnp.arange(half, dtype=F32) / half)
    ang = pos.astype(F32)[:, None] * inv[None, :]
    cos = jnp.cos(ang)
    sin = jnp.sin(ang)
    reps = LANES // MLA_ROPE
    return (jnp.tile(jnp.concatenate([cos, cos], axis=1), (1, reps)),
            jnp.tile(jnp.concatenate([-sin, sin], axis=1), (1, reps)))


def _band_bias(rel_bias, q_pos, k_pos):
    rel = np.clip(q_pos[:, None] - k_pos[None, :], -REL_CLIP, REL_CLIP) + REL_CLIP
    return rel_bias[:, rel]


def kernel(x_prompt, x_sample, cache_fox_k, cache_fox_v, cache_fox_logf, cache_mla_latent, cache_mla_krope, cache_band_k, cache_band_v, cache_sb_k, cache_sb_v, w_in, b_forget, mla_q_norm, w_q_up, mla_kv_norm, w_kv_up, rel_bias, w_branch, w_gate, b_gate, w_out, ln1_g, ln1_b, w_router, router_bias, w_e_gate, w_e_up, w_e_down, ln2_g, ln2_b):
    batch, seq, _ = x_prompt.shape
    dec_batch, t_new, _ = x_sample.shape
    depth = w_in.shape[0]
    past = cache_fox_k.shape[2]
    n_band = cache_band_k.shape[2]
    n_p = batch * seq
    n_s = dec_batch * t_new
    alpha = (2 * depth) ** 0.25
    band_tile = BAND_CHUNKS * CHUNK
    assert seq % ATT_TILE == 0 and seq % band_tile == 0 and n_p % TOKEN_TILE == 0
    assert (n_p + n_s) % MOE_TILE == 0 and n_s % TOKEN_TILE == 0 and n_p % t_new == 0
    assert past % LANES == 0 and t_new <= LANES and t_new % 16 == 0 and n_band % SB_SUB == 0
    assert past % min(SAMPLE_KEY_TILE, past) == 0 and min(SAMPLE_KEY_TILE, past) % SB_SUB == 0

    x = jnp.concatenate([x_prompt.reshape(n_p, D_MODEL), x_sample.reshape(n_s, D_MODEL)], axis=0)
    pos = jnp.concatenate([jnp.tile(jnp.arange(seq, dtype=jnp.int32), batch),
                           jnp.tile(past + jnp.arange(t_new, dtype=jnp.int32), dec_batch)])
    cos, sin = _rope_tables(pos)

    flat = lambda c: c.reshape(*c.shape[:3], -1)
    c_fox_k, c_fox_v = flat(cache_fox_k), flat(cache_fox_v)
    c_band_k, c_band_v = flat(cache_band_k), flat(cache_band_v)
    c_sb_k, c_sb_v = flat(cache_sb_k), flat(cache_sb_v)

    tile_q = np.arange(band_tile)
    tile_k = np.arange(-band_tile, band_tile)
    samp_q = past + np.arange(t_new)
    wr = _pad_cols(w_router, LANES).astype(BF16)
    rb = router_bias.astype(F32)[:, None]

    prompt_states, sample_states = [], []
    for l in range(depth):
        w, bf, qn, wq, wcat, wv_pad = _prep_in_proj_weights(w_in[l], b_forget[l], mla_q_norm[l],
                                                             w_q_up[l], w_kv_up[l])
        (qkv, ka, va, kc, vc, kd, vd, logf, lat, kr, kcat, qcat) = _in_proj(
            x, w, bf, qn, wq, mla_kv_norm[l][None, :], wcat, cos, sin)
        logf = logf[:, :N_HEADS]
        kr = kr[:, :MLA_ROPE]

        logf_p = logf[:n_p].reshape(batch, seq, N_HEADS)
        logf_s = logf[n_p:].reshape(dec_batch, t_new, N_HEADS)
        c_p = _cumsum_rows(jnp.swapaxes(logf_p, 1, 2).reshape(batch * N_HEADS, seq))
        c_p = c_p.reshape(batch, N_HEADS, seq)
        all_s = jnp.concatenate([cache_fox_logf[l].astype(F32), logf_s], axis=1)
        total = past + t_new
        padded = -(-total // LANES) * LANES
        all_s = jnp.pad(jnp.swapaxes(all_s, 1, 2), ((0, 0), (0, 0), (0, padded - total)))
        c_s = _cumsum_rows(all_s.reshape(dec_batch * N_HEADS, padded)).reshape(dec_batch, N_HEADS, padded)
        cq_s = c_s[:, :, past:past + t_new].reshape(dec_batch, N_HEADS * t_new, 1)

        bias_p = _band_bias(rel_bias[l], tile_q, tile_k)
        bias_s = _band_bias(rel_bias[l], samp_q, np.arange(past - n_band, past + t_new))
        bias_s = bias_s.reshape(N_HEADS * t_new, n_band + t_new)

        o_a = jnp.concatenate([
            _fox_prompt(qkv, jnp.swapaxes(c_p, 1, 2), c_p, batch, seq),
            _fox_sample(qkv, c_fox_k, c_fox_v, cq_s, c_s, l, n_p, dec_batch, t_new)], axis=0)
        o_b = jnp.concatenate([
            _mla_prompt(qcat, kcat, wv_pad, batch, seq),
            _mla_sample(qcat, kcat, cache_mla_latent, cache_mla_krope, wv_pad, l, n_p, dec_batch, t_new)],
            axis=0)
        o_c = jnp.concatenate([
            _band_prompt(qkv, bias_p, batch, seq),
            _band_sample(qkv, c_band_k, c_band_v, bias_s[:, :n_band], bias_s[:, n_band:], l, past, n_p,
                         dec_batch, t_new)], axis=0)
        o_d = jnp.concatenate([
            _sb_prompt(qkv, batch, seq),
            _sb_sample(qkv, c_sb_k, c_sb_v, l, n_p, dec_batch, t_new)], axis=0)

        x1, logits = _merge(x, (o_a, o_b, o_c, o_d), w_gate[l].astype(BF16), b_gate[l][:, None, :],
                            w_branch[l].astype(BF16), w_out[l].astype(BF16),
                            ln1_g[l][None, :], ln1_b[l][None, :], wr, alpha)
        comb = _route(logits[:, :N_EXPERTS].T, rb).T
        x = _moe(x1, comb, w_e_gate[l].astype(BF16), w_e_up[l].astype(BF16), w_e_down[l].astype(BF16),
                 ln2_g[l][None, :], ln2_b[l][None, :], alpha)

        heads = lambda a, b_, t_: a.reshape(b_, t_, N_HEADS, HEAD_DIM)
        p = lambda a: a[:n_p]
        s = lambda a: a[n_p:]
        n_keep = min(BAND_CHUNKS * CHUNK, seq)
        prompt_states.append((
            heads(p(ka), batch, seq), heads(p(va), batch, seq), logf_p,
            p(lat).reshape(batch, seq, -1), p(kr).reshape(batch, seq, -1),
            heads(p(kc), batch, seq)[:, seq - n_keep:], heads(p(vc), batch, seq)[:, seq - n_keep:],
            heads(p(kd), batch, seq), heads(p(vd), batch, seq)))
        sample_states.append((
            heads(s(ka), dec_batch, t_new), heads(s(va), dec_batch, t_new), logf_s,
            s(lat).reshape(dec_batch, t_new, -1), s(kr).reshape(dec_batch, t_new, -1),
            heads(s(kc), dec_batch, t_new), heads(s(vc), dec_batch, t_new),
            heads(s(kd), dec_batch, t_new), heads(s(vd), dec_batch, t_new)))

    stack = lambda states: [jnp.stack(z, axis=0) for z in zip(*states)]
    xp = x[:n_p].reshape(batch, seq, D_MODEL)
    xs = x[n_p:].reshape(dec_batch, t_new, D_MODEL)
    return (xp, xs, *stack(prompt_states), *stack(sample_states))
```

```python
import functools

import numpy as np
import jax
import jax.numpy as jnp
from jax import lax
from jax.experimental import pallas as pl
from jax.experimental.pallas import tpu as pltpu

F32 = jnp.float32
BF16 = jnp.bfloat16

D_MODEL = 1024
N_HEADS = 4
HEAD_DIM = 64
HEADS_WIDTH = N_HEADS * HEAD_DIM
CHUNK = 64
BAND_CHUNKS = 8
REL_CLIP = 128
MLA_NOPE = 64
MLA_ROPE = 32
MLA_V = 64
MLA_Q_RANK = 192
MLA_KV_RANK = 128
ROPE_THETA = 10000.0
N_BRANCH = 4
N_EXPERTS = 16
N_GROUPS = 4
EXPERTS_PER_GROUP = N_EXPERTS // N_GROUPS
D_EXPERT = 512
NEG_INF = -1e30
LANES = 128
VMEM_LIMIT = 56 * 1024 * 1024

QKV_WIDTH = 9 * HEADS_WIDTH
STATE_COLS = (1, 2, 4, 5, 7, 8)
OFF_CQ = QKV_WIDTH
OFF_CKV = OFF_CQ + 256
OFF_KR = OFF_CKV + MLA_KV_RANK
OFF_FA = OFF_KR + LANES
PROJ_WIDTH = OFF_FA + LANES
Q_CAT = 256

TOKEN_TILE = 512
MOE_TILE = 1024
ATT_TILE = 512
SB_SUB = 256
SAMPLE_KEY_TILE = 1024
SB_DEAD = -105.0


def _dot(a, b):
    return jnp.dot(a, b, preferred_element_type=F32)


def _dot_nt(a, b):
    return lax.dot_general(a, b, (((1,), (1,)), ((), ())), preferred_element_type=F32)


def _const_spec(shape):
    nd = len(shape)
    return pl.BlockSpec(shape, lambda *_: (0,) * nd, pipeline_mode=pl.Buffered(1))


def _params(*sem):
    return pltpu.CompilerParams(dimension_semantics=sem, vmem_limit_bytes=VMEM_LIMIT)


def _log_sigmoid(x):
    return jnp.minimum(x, 0.0) - jnp.log1p(jnp.exp(-jnp.abs(x)))


def _softplus(x):
    return jnp.maximum(x, 0.0) + jnp.log1p(jnp.exp(-jnp.abs(x)))


def _layer_norm(h, g, b):
    mu = jnp.mean(h, axis=-1, keepdims=True)
    d = h - mu
    var = jnp.mean(d * d, axis=-1, keepdims=True)
    return d * lax.rsqrt(var + 1e-5) * g + b


def _head_of_lane(shape):
    return lax.broadcasted_iota(jnp.int32, shape, len(shape) - 1) // HEAD_DIM


def _in_proj_kernel(x_ref, w_ref, bf_ref, qn_ref, wq_ref, kvn_ref, wcat_ref, cos_ref, sin_ref,
                    qkv_ref, ka_ref, va_ref, kc_ref, vc_ref, kd_ref, vd_ref,
                    logf_ref, lat_ref, kr_ref, kcat_ref, qcat_ref, *, key_major):
    xb = x_ref[...].astype(BF16)
    proj = _dot(xb, w_ref[...])
    qkv_ref[...] = proj[:, :QKV_WIDTH].astype(BF16)
    for ref, col in zip((ka_ref, va_ref, kc_ref, vc_ref, kd_ref, vd_ref), STATE_COLS):
        state = proj[:, col * HEADS_WIDTH:(col + 1) * HEADS_WIDTH]
        ref[...] = state.T if key_major else state

    cos = cos_ref[...]
    sin = sin_ref[...]
    lane = lax.broadcasted_iota(jnp.int32, cos.shape, 1)
    first_half = (lane % MLA_ROPE) < (MLA_ROPE // 2)

    def rope(v):
        partner = jnp.where(first_half, pltpu.roll(v, LANES - MLA_ROPE // 2, 1),
                            pltpu.roll(v, MLA_ROPE // 2, 1))
        return v * cos + partner * sin

    cq = proj[:, OFF_CQ:OFF_CQ + 256]
    ms = jnp.sum(cq * cq, axis=-1, keepdims=True) * (1.0 / MLA_Q_RANK)
    cqn = cq * lax.rsqrt(ms + 1e-6) * qn_ref[...]
    qb = _dot(cqn.astype(BF16), wq_ref[...])
    q_rope = rope(qb[:, HEADS_WIDTH:])
    q_in = jnp.concatenate([qb[:, :HEADS_WIDTH], q_rope], axis=1) * ((MLA_NOPE + MLA_ROPE) ** -0.5)
    qcat_ref[...] = _dot(q_in.astype(BF16), wcat_ref[...]).astype(BF16)

    ckv = proj[:, OFF_CKV:OFF_CKV + MLA_KV_RANK]
    lat = ckv * lax.rsqrt(jnp.mean(ckv * ckv, axis=-1, keepdims=True) + 1e-6) * kvn_ref[...]
    lat_ref[...] = lat
    kr = rope(proj[:, OFF_KR:OFF_KR + LANES])
    kcat_ref[...] = jnp.concatenate([lat, kr], axis=1).astype(BF16)
    logf = _log_sigmoid(proj[:, OFF_FA:OFF_FA + LANES] + bf_ref[...])
    if key_major:
        kr_ref[...] = kr.T[:MLA_ROPE, :]
        logf_ref[...] = logf.T[:N_HEADS, :]
    else:
        kr_ref[...] = kr
        logf_ref[...] = logf


def _in_proj(x, w, bf, qn, wq, kvn, wcat, cos, sin, *, batch=None, seq=None):
    n = x.shape[0]
    tm = TOKEN_TILE
    key_major = batch is not None
    row = lambda width: pl.BlockSpec((tm, width), lambda i: (i, 0))
    if key_major:
        per = seq // tm
        tr = lambda height: pl.BlockSpec((None, height, tm), lambda i: (i // per, 0, i % per))
        tshape = lambda height: jax.ShapeDtypeStruct((batch, height, seq), F32)
        state_shapes = [tshape(HEADS_WIDTH)] * 6 + [tshape(N_HEADS), jax.ShapeDtypeStruct((n, LANES), F32),
                                                    tshape(MLA_ROPE)]
        state_specs = [tr(HEADS_WIDTH)] * 6 + [tr(N_HEADS), row(LANES), tr(MLA_ROPE)]
    else:
        state_shapes = ([jax.ShapeDtypeStruct((n, HEADS_WIDTH), F32)] * 6
                        + [jax.ShapeDtypeStruct((n, LANES), F32)] * 3)
        state_specs = [row(HEADS_WIDTH)] * 6 + [row(LANES)] * 3
    out_shape = ([jax.ShapeDtypeStruct((n, QKV_WIDTH), BF16)] + state_shapes
                 + [jax.ShapeDtypeStruct((n, 2 * LANES), BF16),
                    jax.ShapeDtypeStruct((n, N_HEADS * Q_CAT), BF16)])
    out_specs = [row(QKV_WIDTH)] + state_specs + [row(2 * LANES), row(N_HEADS * Q_CAT)]
    return pl.pallas_call(
        functools.partial(_in_proj_kernel, key_major=key_major),
        out_shape=out_shape,
        grid=(n // tm,),
        in_specs=[row(D_MODEL), _const_spec(w.shape), _const_spec(bf.shape), _const_spec(qn.shape),
                  _const_spec(wq.shape), _const_spec(kvn.shape), _const_spec(wcat.shape),
                  row(LANES), row(LANES)],
        out_specs=out_specs,
        compiler_params=_params("parallel"),
        name="in_proj_prompt" if key_major else "in_proj_sample",
    )(x, w, bf, qn, wq, kvn, wcat, cos, sin)


def _cumsum_kernel(x_ref, o_ref):
    r, t = x_ref.shape
    row = lax.broadcasted_iota(jnp.int32, (LANES, LANES), 0)
    col = lax.broadcasted_iota(jnp.int32, (LANES, LANES), 1)
    tri = jnp.where(row <= col, 1.0, 0.0).astype(BF16)
    carry = jnp.zeros((r, 1), F32)
    for c in range(t // LANES):
        xc = x_ref[:, c * LANES:(c + 1) * LANES]
        hi = xc.astype(BF16)
        r1 = xc - hi.astype(F32)
        mid = r1.astype(BF16)
        lo = (r1 - mid.astype(F32)).astype(BF16)
        y = _dot(hi, tri) + _dot(mid, tri) + _dot(lo, tri) + carry
        o_ref[:, c * LANES:(c + 1) * LANES] = y
        carry = y[:, LANES - 1:LANES]


def _cumsum_rows(x):
    return pl.pallas_call(
        _cumsum_kernel,
        out_shape=jax.ShapeDtypeStruct(x.shape, F32),
        name="cumsum",
    )(x)


def _online_softmax_update(s, pv, m_ref, l_ref, acc_ref, h):
    m_prev = m_ref[h]
    m_new = jnp.maximum(m_prev, jnp.max(s, axis=1, keepdims=True))
    alpha = jnp.exp(m_prev - m_new)
    p = jnp.exp(s - m_new)
    l_ref[h] = alpha * l_ref[h] + jnp.sum(p, axis=1, keepdims=True)
    acc_ref[h] = alpha * acc_ref[h] + pv(p.astype(BF16))
    m_ref[h] = m_new


def _init_softmax_state(m_ref, l_ref, acc_ref):
    m_ref[...] = jnp.full(m_ref.shape, NEG_INF, F32)
    l_ref[...] = jnp.zeros(l_ref.shape, F32)
    acc_ref[...] = jnp.zeros(acc_ref.shape, F32)


def _store_head_masked_q(q_ref, qs_ref, rows):
    q = q_ref[...] * (HEAD_DIM ** -0.5)
    head = _head_of_lane(q.shape)
    for h in range(N_HEADS):
        qs_ref[h * rows:(h + 1) * rows, :] = jnp.where(head == h, q, jnp.zeros_like(q))


def _merge_heads(per_head):
    head = _head_of_lane(per_head[0].shape)
    out = jnp.zeros_like(per_head[0])
    for h in range(N_HEADS):
        out = jnp.where(head == h, per_head[h], out)
    return out


def _fox_prompt_kernel(q_ref, k_ref, v_ref, cq_ref, ck_ref, o_ref, qs_ref, m_ref, l_ref, acc_ref):
    tq = q_ref.shape[0]
    qi = pl.program_id(1)
    ki = pl.program_id(2)

    @pl.when(ki == 0)
    def _():
        _init_softmax_state(m_ref, l_ref, acc_ref)
        _store_head_masked_q(q_ref, qs_ref, tq)

    def step(diagonal):
        k = k_ref[...]
        v = v_ref[...]
        cq = cq_ref[...]
        ck = ck_ref[...]
        for h in range(N_HEADS):
            s = _dot_nt(qs_ref[h * tq:(h + 1) * tq, :], k) + cq[:, h:h + 1] - ck[h:h + 1, :]
            if diagonal:
                row = lax.broadcasted_iota(jnp.int32, s.shape, 0)
                col = lax.broadcasted_iota(jnp.int32, s.shape, 1)
                s = jnp.where(col <= row, s, NEG_INF)
            _online_softmax_update(s, lambda p: _dot(p, v), m_ref, l_ref, acc_ref, h)

    pl.when(ki < qi)(lambda: step(False))
    pl.when(ki == qi)(lambda: step(True))

    @pl.when(ki == pl.num_programs(2) - 1)
    def _():
        o_ref[...] = _merge_heads([acc_ref[h] / l_ref[h] for h in range(N_HEADS)]).astype(BF16)


def _fox_prompt(qkv, c_tok, c_head, batch, seq):
    t = ATT_TILE
    nt = seq // t
    return pl.pallas_call(
        _fox_prompt_kernel,
        out_shape=jax.ShapeDtypeStruct((batch * seq, HEADS_WIDTH), BF16),
        grid=(batch, nt, nt),
        in_specs=[pl.BlockSpec((t, HEADS_WIDTH), lambda b, i, j: (b * nt + i, 0)),
                  pl.BlockSpec((t, HEADS_WIDTH), lambda b, i, j: (b * nt + jnp.minimum(i, j), 1)),
                  pl.BlockSpec((t, HEADS_WIDTH), lambda b, i, j: (b * nt + jnp.minimum(i, j), 2)),
                  pl.BlockSpec((None, t, N_HEADS), lambda b, i, j: (b, i, 0)),
                  pl.BlockSpec((None, N_HEADS, t), lambda b, i, j: (b, 0, jnp.minimum(i, j)))],
        out_specs=pl.BlockSpec((t, HEADS_WIDTH), lambda b, i, j: (b * nt + i, 0)),
        scratch_shapes=[pltpu.VMEM((N_HEADS * t, HEADS_WIDTH), BF16),
                        pltpu.VMEM((N_HEADS, t, 1), F32), pltpu.VMEM((N_HEADS, t, 1), F32),
                        pltpu.VMEM((N_HEADS, t, HEADS_WIDTH), F32)],
        compiler_params=_params("parallel", "parallel", "arbitrary"),
        name="fox_prompt",
    )(qkv, qkv, qkv, c_tok, c_head)


def _mla_prompt_kernel(q_ref, k_ref, wv_ref, o_ref, m_ref, l_ref, acc_ref):
    qi = pl.program_id(1)
    ki = pl.program_id(2)

    @pl.when(ki == 0)
    def _():
        _init_softmax_state(m_ref, l_ref, acc_ref)

    def step(diagonal):
        k = k_ref[...]
        for h in range(N_HEADS):
            s = _dot_nt(q_ref[:, h * Q_CAT:(h + 1) * Q_CAT], k)
            if diagonal:
                row = lax.broadcasted_iota(jnp.int32, s.shape, 0)
                col = lax.broadcasted_iota(jnp.int32, s.shape, 1)
                s = jnp.where(col // CHUNK <= row // CHUNK, s, NEG_INF)
            _online_softmax_update(s, lambda p: _dot(p, k), m_ref, l_ref, acc_ref, h)

    pl.when(ki < qi)(lambda: step(False))
    pl.when(ki == qi)(lambda: step(True))

    @pl.when(ki == pl.num_programs(2) - 1)
    def _():
        out = None
        for h in range(N_HEADS):
            term = _dot((acc_ref[h] / l_ref[h]).astype(BF16), wv_ref[h])
            out = term if out is None else out + term
        o_ref[...] = out.astype(BF16)


def _mla_prompt(qcat, kcat, wv_pad, batch, seq):
    t = ATT_TILE
    nt = seq // t
    return pl.pallas_call(
        _mla_prompt_kernel,
        out_shape=jax.ShapeDtypeStruct((batch * seq, HEADS_WIDTH), BF16),
        grid=(batch, nt, nt),
        in_specs=[pl.BlockSpec((t, N_HEADS * Q_CAT), lambda b, i, j: (b * nt + i, 0)),
                  pl.BlockSpec((t, Q_CAT), lambda b, i, j: (b * nt + jnp.minimum(i, j), 0)),
                  _const_spec(wv_pad.shape)],
        out_specs=pl.BlockSpec((t, HEADS_WIDTH), lambda b, i, j: (b * nt + i, 0)),
        scratch_shapes=[pltpu.VMEM((N_HEADS, t, 1), F32), pltpu.VMEM((N_HEADS, t, 1), F32),
                        pltpu.VMEM((N_HEADS, t, Q_CAT), F32)],
        compiler_params=_params("parallel", "parallel", "arbitrary"),
        name="mla_prompt",
    )(qcat, kcat, wv_pad)


def _strict_upper_ones(n):
    row = lax.broadcasted_iota(jnp.int32, (n, n), 0)
    col = lax.broadcasted_iota(jnp.int32, (n, n), 1)
    return jnp.where(row > col, 1.0, 0.0).astype(BF16)


def _sb_block(z, u, rest, mask):
    sp = _softplus(z)
    log_1m = -sp
    if mask is not None:
        log_1m = jnp.where(mask, log_1m, 0.0)
    hi = log_1m.astype(BF16)
    lo = (log_1m - hi.astype(F32)).astype(BF16)
    later = _dot(hi, u) + _dot(lo, u) + rest
    a = jnp.exp(z - sp + later)
    if mask is not None:
        a = jnp.where(mask, a, 0.0)
    return a.astype(BF16), rest + jnp.sum(log_1m, axis=1, keepdims=True)


def _sb_prompt_kernel(q_ref, k_ref, v_ref, o_ref, qs_ref, u_ref, rest_ref, acc_ref):
    tq = q_ref.shape[0]
    qi = pl.program_id(1)
    _store_head_masked_q(q_ref, qs_ref, tq)
    u_ref[...] = _strict_upper_ones(SB_SUB)
    rest_ref[...] = jnp.zeros(rest_ref.shape, F32)
    acc_ref[...] = jnp.zeros(acc_ref.shape, F32)

    def visit(start, first_col):
        k = k_ref[pl.ds(start, SB_SUB), :]
        v = v_ref[pl.ds(start, SB_SUB), :]
        u = u_ref[...]
        live = None
        for h in range(N_HEADS):
            z = _dot_nt(qs_ref[h * tq:(h + 1) * tq, :], k)
            mask = None
            if first_col is not None:
                row = lax.broadcasted_iota(jnp.int32, z.shape, 0)
                col = lax.broadcasted_iota(jnp.int32, z.shape, 1) + first_col
                mask = col < row
            a, rest = _sb_block(z, u, rest_ref[h], mask)
            acc_ref[h] += _dot(a, v)
            rest_ref[h] = rest
            top = jnp.max(rest)
            live = top if live is None else jnp.maximum(live, top)
        return live

    live = None
    for sub in reversed(range(tq // SB_SUB)):
        live = visit(pl.multiple_of(qi * tq + sub * SB_SUB, SB_SUB), sub * SB_SUB)

    def more(carry):
        return jnp.logical_and(carry[0] >= 0, carry[1] > SB_DEAD)

    def older(carry):
        return carry[0] - 1, visit(pl.multiple_of(carry[0] * SB_SUB, SB_SUB), None)

    lax.while_loop(more, older, (qi * (tq // SB_SUB) - 1, live))
    o_ref[...] = _merge_heads([acc_ref[h] for h in range(N_HEADS)]).astype(BF16)


def _sb_prompt(qkv, batch, seq):
    t = ATT_TILE
    nt = seq // t
    return pl.pallas_call(
        _sb_prompt_kernel,
        out_shape=jax.ShapeDtypeStruct((batch * seq, HEADS_WIDTH), BF16),
        grid=(batch, nt),
        in_specs=[pl.BlockSpec((t, HEADS_WIDTH), lambda b, i: (b * nt + i, 6)),
                  pl.BlockSpec((seq, HEADS_WIDTH), lambda b, i: (b, 7)),
                  pl.BlockSpec((seq, HEADS_WIDTH), lambda b, i: (b, 8))],
        out_specs=pl.BlockSpec((t, HEADS_WIDTH), lambda b, i: (b * nt + i, 0)),
        scratch_shapes=[pltpu.VMEM((N_HEADS * t, HEADS_WIDTH), BF16),
                        pltpu.VMEM((SB_SUB, SB_SUB), BF16),
                        pltpu.VMEM((N_HEADS, t, 1), F32),
                        pltpu.VMEM((N_HEADS, t, HEADS_WIDTH), F32)],
        compiler_params=_params("parallel", "arbitrary"),
        name="sb_prompt",
    )(qkv, qkv, qkv)


def _band_prompt_kernel(q_ref, kp_ref, kc_ref, vp_ref, vc_ref, bias_ref, o_ref, qs_ref):
    tq = q_ref.shape[0]
    qi = pl.program_id(1)
    _store_head_masked_q(q_ref, qs_ref, tq)
    k = jnp.concatenate([kp_ref[...], kc_ref[...]], axis=0)
    v = jnp.concatenate([vp_ref[...], vc_ref[...]], axis=0)
    row_chunk = lax.broadcasted_iota(jnp.int32, (tq, 2 * tq), 0) // CHUNK
    col = lax.broadcasted_iota(jnp.int32, (tq, 2 * tq), 1)
    key_chunk = col // CHUNK - tq // CHUNK
    valid = (key_chunk <= row_chunk) & (key_chunk >= row_chunk - BAND_CHUNKS)
    valid = valid & ((col >= tq) | (qi > 0))
    outs = []
    for h in range(N_HEADS):
        s = _dot_nt(qs_ref[h * tq:(h + 1) * tq, :], k) + bias_ref[h]
        s = jnp.where(valid, s, NEG_INF)
        p = jnp.exp(s - jnp.max(s, axis=1, keepdims=True))
        denom = jnp.sum(p, axis=1, keepdims=True)
        outs.append(_dot(p.astype(BF16), v) / denom)
    o_ref[...] = _merge_heads(outs).astype(BF16)


def _band_prompt(qkv, bias_tile, batch, seq):
    t = BAND_CHUNKS * CHUNK
    nt = seq // t
    prev = lambda b, i: b * nt + jnp.maximum(i - 1, 0)
    return pl.pallas_call(
        _band_prompt_kernel,
        out_shape=jax.ShapeDtypeStruct((batch * seq, HEADS_WIDTH), BF16),
        grid=(batch, nt),
        in_specs=[pl.BlockSpec((t, HEADS_WIDTH), lambda b, i: (b * nt + i, 3)),
                  pl.BlockSpec((t, HEADS_WIDTH), lambda b, i: (prev(b, i), 4)),
                  pl.BlockSpec((t, HEADS_WIDTH), lambda b, i: (b * nt + i, 4)),
                  pl.BlockSpec((t, HEADS_WIDTH), lambda b, i: (prev(b, i), 5)),
                  pl.BlockSpec((t, HEADS_WIDTH), lambda b, i: (b * nt + i, 5)),
                  _const_spec(bias_tile.shape)],
        out_specs=pl.BlockSpec((t, HEADS_WIDTH), lambda b, i: (b * nt + i, 0)),
        scratch_shapes=[pltpu.VMEM((N_HEADS * t, HEADS_WIDTH), BF16)],
        compiler_params=_params("parallel", "parallel"),
        name="band_prompt",
    )(qkv, qkv, qkv, qkv, qkv, bias_tile)


def _rows_per_head(x, rows):
    return jnp.concatenate([jnp.broadcast_to(x[h:h + 1, :], (rows, x.shape[1])) for h in range(N_HEADS)],
                           axis=0)


def _unstack_heads(acc, rows):
    return _merge_heads([acc[h * rows:(h + 1) * rows, :] for h in range(N_HEADS)])


def _fox_sample_kernel(q_ref, kc_ref, vc_ref, kn_ref, vn_ref, cq_ref, ckc_ref, ckn_ref, o_ref,
                       qs_ref, m_ref, l_ref, acc_ref):
    t = q_ref.shape[0]
    ki = pl.program_id(1)

    @pl.when(ki == 0)
    def _():
        _init_softmax_state(m_ref, l_ref, acc_ref)
        _store_head_masked_q(q_ref, qs_ref, t)

    qs = qs_ref[...]
    cq = cq_ref[...]
    s = _dot(qs, kc_ref[...].astype(BF16)) + cq - _rows_per_head(ckc_ref[...], t)
    vc = vc_ref[...].astype(BF16)
    _online_softmax_update(s, lambda p: _dot_nt(p, vc), m_ref, l_ref, acc_ref, 0)

    @pl.when(ki == pl.num_programs(1) - 1)
    def _():
        s = _dot_nt(qs, kn_ref[...]) + cq - _rows_per_head(ckn_ref[:, :t], t)
        row = lax.broadcasted_iota(jnp.int32, s.shape, 0) % t
        col = lax.broadcasted_iota(jnp.int32, s.shape, 1)
        s = jnp.where(col <= row, s, NEG_INF)
        _online_softmax_update(s, lambda p: _dot(p, vn_ref[...]), m_ref, l_ref, acc_ref, 0)
        o_ref[...] = _unstack_heads(acc_ref[0] / l_ref[0], t).astype(BF16)


def _fox_sample(qkv, cache_kt, cache_vt, cq_rows, c_head, layer, dec_batch, t):
    past = cache_kt.shape[3]
    tk = min(SAMPLE_KEY_TILE, past)
    nk = past // tk
    return pl.pallas_call(
        _fox_sample_kernel,
        out_shape=jax.ShapeDtypeStruct((dec_batch * t, HEADS_WIDTH), BF16),
        grid=(dec_batch, nk),
        in_specs=[pl.BlockSpec((t, HEADS_WIDTH), lambda b, j: (b, 0)),
                  pl.BlockSpec((None, None, HEADS_WIDTH, tk), lambda b, j: (layer, b, 0, j)),
                  pl.BlockSpec((None, None, HEADS_WIDTH, tk), lambda b, j: (layer, b, 0, j)),
                  pl.BlockSpec((t, HEADS_WIDTH), lambda b, j: (b, 1)),
                  pl.BlockSpec((t, HEADS_WIDTH), lambda b, j: (b, 2)),
                  pl.BlockSpec((None, N_HEADS * t, 1), lambda b, j: (b, 0, 0)),
                  pl.BlockSpec((None, N_HEADS, tk), lambda b, j: (b, 0, j)),
                  pl.BlockSpec((None, N_HEADS, LANES), lambda b, j: (b, 0, past // LANES))],
        out_specs=pl.BlockSpec((t, HEADS_WIDTH), lambda b, j: (b, 0)),
        scratch_shapes=[pltpu.VMEM((N_HEADS * t, HEADS_WIDTH), BF16),
                        pltpu.VMEM((1, N_HEADS * t, 1), F32), pltpu.VMEM((1, N_HEADS * t, 1), F32),
                        pltpu.VMEM((1, N_HEADS * t, HEADS_WIDTH), F32)],
        compiler_params=_params("parallel", "arbitrary"),
        name="fox_sample",
    )(qkv, cache_kt, cache_vt, qkv, qkv, cq_rows, c_head, c_head)


def _mla_sample_kernel(q_ref, latc_ref, krc_ref, kn_ref, wv_ref, o_ref,
                       qa_ref, qr_ref, m_ref, l_ref, acc_ref, *, past):
    t = q_ref.shape[0]
    ki = pl.program_id(1)

    @pl.when(ki == 0)
    def _():
        _init_softmax_state(m_ref, l_ref, acc_ref)
        for h in range(N_HEADS):
            qa_ref[h * t:(h + 1) * t, :] = q_ref[:, h * Q_CAT:h * Q_CAT + MLA_KV_RANK]
            qr_ref[h * t:(h + 1) * t, :] = q_ref[:, h * Q_CAT + MLA_KV_RANK:
                                                 h * Q_CAT + MLA_KV_RANK + MLA_ROPE]

    lat = latc_ref[...].astype(BF16)
    s = _dot_nt(qa_ref[...], lat) + _dot(qr_ref[...], krc_ref[...].astype(BF16))
    _online_softmax_update(s, lambda p: _dot(p, lat), m_ref, l_ref, acc_ref, 0)

    @pl.when(ki == pl.num_programs(1) - 1)
    def _():
        lat_n = kn_ref[:, :MLA_KV_RANK]
        s = _dot_nt(qa_ref[...], lat_n) + _dot_nt(qr_ref[...],
                                                  kn_ref[:, MLA_KV_RANK:MLA_KV_RANK + MLA_ROPE])
        q_pos = past + lax.broadcasted_iota(jnp.int32, s.shape, 0) % t
        k_pos = past + lax.broadcasted_iota(jnp.int32, s.shape, 1)
        s = jnp.where(k_pos // CHUNK <= q_pos // CHUNK, s, NEG_INF)
        _online_softmax_update(s, lambda p: _dot(p, lat_n), m_ref, l_ref, acc_ref, 0)
        o = (acc_ref[0] / l_ref[0]).astype(BF16)
        out = None
        for h in range(N_HEADS):
            term = _dot(o[h * t:(h + 1) * t, :], wv_ref[h, :MLA_KV_RANK, :])
            out = term if out is None else out + term
        o_ref[...] = out.astype(BF16)


def _mla_sample(qcat, kcat, cache_lat, cache_krt, wv_pad, layer, dec_batch, t):
    past = cache_lat.shape[2]
    tk = min(SAMPLE_KEY_TILE, past)
    nk = past // tk
    return pl.pallas_call(
        functools.partial(_mla_sample_kernel, past=past),
        out_shape=jax.ShapeDtypeStruct((dec_batch * t, HEADS_WIDTH), BF16),
        grid=(dec_batch, nk),
        in_specs=[pl.BlockSpec((t, N_HEADS * Q_CAT), lambda b, j: (b, 0)),
                  pl.BlockSpec((None, None, tk, MLA_KV_RANK), lambda b, j: (layer, b, j, 0)),
                  pl.BlockSpec((None, None, MLA_ROPE, tk), lambda b, j: (layer, b, 0, j)),
                  pl.BlockSpec((t, Q_CAT), lambda b, j: (b, 0)),
                  _const_spec(wv_pad.shape)],
        out_specs=pl.BlockSpec((t, HEADS_WIDTH), lambda b, j: (b, 0)),
        scratch_shapes=[pltpu.VMEM((N_HEADS * t, MLA_KV_RANK), BF16),
                        pltpu.VMEM((N_HEADS * t, MLA_ROPE), BF16),
                        pltpu.VMEM((1, N_HEADS * t, 1), F32), pltpu.VMEM((1, N_HEADS * t, 1), F32),
                        pltpu.VMEM((1, N_HEADS * t, MLA_KV_RANK), F32)],
        compiler_params=_params("parallel", "arbitrary"),
        name="mla_sample",
    )(qcat, cache_lat, cache_krt, kcat, wv_pad)


def _band_sample_kernel(q_ref, kc_ref, vc_ref, kn_ref, vn_ref, biasc_ref, biasn_ref, o_ref, qs_ref,
                        *, past):
    t = q_ref.shape[0]
    n_band = kc_ref.shape[1]
    _store_head_masked_q(q_ref, qs_ref, t)
    qs = qs_ref[...]

    def masked(s, first_pos):
        q_chunk = (past + lax.broadcasted_iota(jnp.int32, s.shape, 0) % t) // CHUNK
        k_pos = first_pos + lax.broadcasted_iota(jnp.int32, s.shape, 1)
        k_chunk = k_pos // CHUNK
        valid = (k_pos >= 0) & (k_chunk <= q_chunk) & (k_chunk >= q_chunk - BAND_CHUNKS)
        return jnp.where(valid, s, NEG_INF)

    s_c = masked(_dot(qs, kc_ref[...].astype(BF16)) + biasc_ref[...], past - n_band)
    s_n = masked(_dot_nt(qs, kn_ref[...]) + biasn_ref[...], past)
    m = jnp.maximum(jnp.max(s_c, axis=1, keepdims=True), jnp.max(s_n, axis=1, keepdims=True))
    p_c = jnp.exp(s_c - m)
    p_n = jnp.exp(s_n - m)
    denom = jnp.sum(p_c, axis=1, keepdims=True) + jnp.sum(p_n, axis=1, keepdims=True)
    acc = _dot_nt(p_c.astype(BF16), vc_ref[...].astype(BF16)) + _dot(p_n.astype(BF16), vn_ref[...])
    o_ref[...] = _unstack_heads(acc / denom, t).astype(BF16)


def _band_sample(qkv, cache_kt, cache_vt, bias_cache, bias_new, layer, past, dec_batch, t):
    n_band = cache_kt.shape[3]
    return pl.pallas_call(
        functools.partial(_band_sample_kernel, past=past),
        out_shape=jax.ShapeDtypeStruct((dec_batch * t, HEADS_WIDTH), BF16),
        grid=(dec_batch,),
        in_specs=[pl.BlockSpec((t, HEADS_WIDTH), lambda b: (b, 3)),
                  pl.BlockSpec((None, None, HEADS_WIDTH, n_band), lambda b: (layer, b, 0, 0)),
                  pl.BlockSpec((None, None, HEADS_WIDTH, n_band), lambda b: (layer, b, 0, 0)),
                  pl.BlockSpec((t, HEADS_WIDTH), lambda b: (b, 4)),
                  pl.BlockSpec((t, HEADS_WIDTH), lambda b: (b, 5)),
                  _const_spec(bias_cache.shape), _const_spec(bias_new.shape)],
        out_specs=pl.BlockSpec((t, HEADS_WIDTH), lambda b: (b, 0)),
        scratch_shapes=[pltpu.VMEM((N_HEADS * t, HEADS_WIDTH), BF16)],
        compiler_params=_params("parallel"),
        name="band_sample",
    )(qkv, cache_kt, cache_vt, qkv, qkv, bias_cache, bias_new)


def _sb_sample_kernel(q_ref, kc_ref, vc_ref, kn_ref, vn_ref, o_ref, qs_ref, u_ref, rest_ref, acc_ref):
    t = q_ref.shape[0]
    past = kc_ref.shape[1]
    _store_head_masked_q(q_ref, qs_ref, t)
    u = _strict_upper_ones(SB_SUB)
    u_ref[...] = u
    qs = qs_ref[...]

    z = _dot_nt(qs, kn_ref[...])
    row = lax.broadcasted_iota(jnp.int32, z.shape, 0) % t
    col = lax.broadcasted_iota(jnp.int32, z.shape, 1)
    a, rest = _sb_block(z, u[:t, :t], jnp.zeros((z.shape[0], 1), F32), col < row)
    acc_ref[...] = _dot(a, vn_ref[...])
    rest_ref[...] = rest

    def more(carry):
        return jnp.logical_and(carry[0] >= 0, carry[1] > SB_DEAD)

    def older(carry):
        start = pl.multiple_of(carry[0] * SB_SUB, SB_SUB)
        k = kc_ref[:, pl.ds(start, SB_SUB)].astype(BF16)
        v = vc_ref[:, pl.ds(start, SB_SUB)].astype(BF16)
        a, rest = _sb_block(_dot(qs_ref[...], k), u_ref[...], rest_ref[...], None)
        acc_ref[...] += _dot_nt(a, v)
        rest_ref[...] = rest
        return carry[0] - 1, jnp.max(rest)

    lax.while_loop(more, older, (past // SB_SUB - 1, jnp.max(rest)))
    o_ref[...] = _unstack_heads(acc_ref[...], t).astype(BF16)


def _sb_sample(qkv, cache_kt, cache_vt, layer, dec_batch, t):
    past = cache_kt.shape[3]
    return pl.pallas_call(
        _sb_sample_kernel,
        out_shape=jax.ShapeDtypeStruct((dec_batch * t, HEADS_WIDTH), BF16),
        grid=(dec_batch,),
        in_specs=[pl.BlockSpec((t, HEADS_WIDTH), lambda b: (b, 6)),
                  pl.BlockSpec((None, None, HEADS_WIDTH, past), lambda b: (layer, b, 0, 0)),
                  pl.BlockSpec((None, None, HEADS_WIDTH, past), lambda b: (layer, b, 0, 0)),
                  pl.BlockSpec((t, HEADS_WIDTH), lambda b: (b, 7)),
                  pl.BlockSpec((t, HEADS_WIDTH), lambda b: (b, 8))],
        out_specs=pl.BlockSpec((t, HEADS_WIDTH), lambda b: (b, 0)),
        scratch_shapes=[pltpu.VMEM((N_HEADS * t, HEADS_WIDTH), BF16),
                        pltpu.VMEM((SB_SUB, SB_SUB), BF16),
                        pltpu.VMEM((N_HEADS * t, 1), F32),
                        pltpu.VMEM((N_HEADS * t, HEADS_WIDTH), F32)],
        compiler_params=_params("parallel"),
        name="sb_sample",
    )(qkv, cache_kt, cache_vt, qkv, qkv)


def _merge_kernel(x_ref, oa_ref, ob_ref, oc_ref, od_ref, wg_ref, bg_ref, wb_ref, wo_ref,
                  g_ref, b_ref, wr_ref, y_ref, logit_ref, *, alpha):
    x = x_ref[...]
    xb = x.astype(BF16)
    merged = None
    for n, o_ref in enumerate((oa_ref, ob_ref, oc_ref, od_ref)):
        gate = jax.nn.sigmoid(_dot(xb, wg_ref[n]) + bg_ref[n])
        term = gate * _dot(o_ref[...], wb_ref[n])
        merged = term if merged is None else merged + term
    h = alpha * x + _dot(merged.astype(BF16), wo_ref[...])
    y = _layer_norm(h, g_ref[...], b_ref[...])
    y_ref[...] = y
    logit_ref[...] = _dot(y.astype(BF16), wr_ref[...])


def _merge(x, branches, wg, bg, wb, wo, g, b, wr, alpha):
    n = x.shape[0]
    tm = TOKEN_TILE
    row = lambda width: pl.BlockSpec((tm, width), lambda i: (i, 0))
    return pl.pallas_call(
        functools.partial(_merge_kernel, alpha=alpha),
        out_shape=[jax.ShapeDtypeStruct((n, D_MODEL), F32), jax.ShapeDtypeStruct((n, LANES), F32)],
        grid=(n // tm,),
        in_specs=[row(D_MODEL)] + [row(HEADS_WIDTH)] * 4
                 + [_const_spec(a.shape) for a in (wg, bg, wb, wo, g, b, wr)],
        out_specs=[row(D_MODEL), row(LANES)],
        compiler_params=_params("parallel"),
        name="merge",
    )(x, *branches, wg, bg, wb, wo, g, b, wr)


def _route_kernel(logit_ref, bias_ref, comb_ref):
    scores = jax.nn.sigmoid(logit_ref[...])
    sel = scores + bias_ref[...]
    rows = [sel[e:e + 1, :] for e in range(N_EXPERTS)]
    best = None
    best_group = None
    for g in range(N_GROUPS):
        m = rows[g * EXPERTS_PER_GROUP:(g + 1) * EXPERTS_PER_GROUP]
        top2 = None
        for i in range(EXPERTS_PER_GROUP):
            for j in range(i + 1, EXPERTS_PER_GROUP):
                pair = m[i] + m[j]
                top2 = pair if top2 is None else jnp.maximum(top2, pair)
        if best is None:
            best, best_group = top2, jnp.zeros(top2.shape, jnp.int32)
        else:
            better = top2 > best
            best = jnp.where(better, top2, best)
            best_group = jnp.where(better, g, best_group)
    weights = []
    total = None
    for e in range(N_EXPERTS):
        g, i = divmod(e, EXPERTS_PER_GROUP)
        rank = jnp.zeros(best.shape, jnp.int32)
        for j in range(EXPERTS_PER_GROUP):
            if j == i:
                continue
            other = rows[g * EXPERTS_PER_GROUP + j]
            ahead = (other >= rows[e]) if j < i else (other > rows[e])
            rank = rank + ahead.astype(jnp.int32)
        chosen = (best_group == g) & (rank < 2)
        w = jnp.where(chosen, scores[e:e + 1, :], 0.0)
        weights.append(w)
        total = w if total is None else total + w
    comb_ref[...] = jnp.concatenate(weights, axis=0) / total


def _route(logits_t, bias_col):
    return pl.pallas_call(
        _route_kernel,
        out_shape=jax.ShapeDtypeStruct(logits_t.shape, F32),
        compiler_params=pltpu.CompilerParams(vmem_limit_bytes=VMEM_LIMIT),
        name="route",
    )(logits_t, bias_col)


def _moe_kernel(x_ref, comb_ref, wg_ref, wu_ref, wd_ref, g_ref, b_ref, y_ref, xb_ref, acc_ref, *, alpha):
    e = pl.program_id(1)

    @pl.when(e == 0)
    def _():
        xb_ref[...] = x_ref[...].astype(BF16)
        acc_ref[...] = jnp.zeros(acc_ref.shape, F32)

    xb = xb_ref[...]
    hidden = jax.nn.silu(_dot(xb, wg_ref[...])) * _dot(xb, wu_ref[...])
    out = _dot(hidden.astype(BF16), wd_ref[...])
    comb = comb_ref[...]
    lane = lax.broadcasted_iota(jnp.int32, comb.shape, 1)
    weight = jnp.sum(jnp.where(lane == e, comb, 0.0), axis=1, keepdims=True)
    acc_ref[...] += weight * out

    @pl.when(e == pl.num_programs(1) - 1)
    def _():
        y_ref[...] = _layer_norm(alpha * x_ref[...] + acc_ref[...], g_ref[...], b_ref[...])


def _moe(x, comb, wg, wu, wd, g, b, alpha):
    n = x.shape[0]
    tm = MOE_TILE
    return pl.pallas_call(
        functools.partial(_moe_kernel, alpha=alpha),
        out_shape=jax.ShapeDtypeStruct((n, D_MODEL), F32),
        grid=(n // tm, N_EXPERTS),
        in_specs=[pl.BlockSpec((tm, D_MODEL), lambda i, e: (i, 0)),
                  pl.BlockSpec((tm, N_EXPERTS), lambda i, e: (i, 0)),
                  pl.BlockSpec((None, D_MODEL, D_EXPERT), lambda i, e: (e, 0, 0)),
                  pl.BlockSpec((None, D_MODEL, D_EXPERT), lambda i, e: (e, 0, 0)),
                  pl.BlockSpec((None, D_EXPERT, D_MODEL), lambda i, e: (e, 0, 0)),
                  _const_spec(g.shape), _const_spec(b.shape)],
        out_specs=pl.BlockSpec((tm, D_MODEL), lambda i, e: (i, 0)),
        scratch_shapes=[pltpu.VMEM((tm, D_MODEL), BF16), pltpu.VMEM((tm, D_MODEL), F32)],
        compiler_params=_params("parallel", "arbitrary"),
        name="moe",
    )(x, comb, wg, wu, wd, g, b)


IN_SIZES = (HEADS_WIDTH,) * 3 + (N_HEADS, MLA_Q_RANK, MLA_KV_RANK, MLA_ROPE) + (HEADS_WIDTH,) * 6


def _pad_cols(a, width):
    return jnp.pad(a, ((0, 0), (0, width - a.shape[1])))


def _prep_in_proj_weights(w_in, b_forget, q_norm, w_q_up, w_kv_up):
    offs = np.cumsum(IN_SIZES)[:-1].tolist()
    qa, ka, va, fa, cq, ckv, kr, qc, kc, vc, qd, kd, vd = jnp.split(w_in, offs, axis=1)
    w = jnp.concatenate([qa, ka, va, qc, kc, vc, qd, kd, vd, _pad_cols(cq, 256), ckv,
                         _pad_cols(kr, LANES), _pad_cols(fa, LANES)], axis=1).astype(BF16)
    bf = _pad_cols(b_forget[None, :], LANES)
    qn = _pad_cols(q_norm[None, :], 256)
    wq3 = w_q_up.reshape(MLA_Q_RANK, N_HEADS, MLA_NOPE + MLA_ROPE)
    wq = jnp.concatenate([wq3[:, :, :MLA_NOPE].reshape(MLA_Q_RANK, -1),
                          wq3[:, :, MLA_NOPE:].reshape(MLA_Q_RANK, -1)], axis=1)
    wq = jnp.pad(wq, ((0, 256 - MLA_Q_RANK), (0, 0))).astype(BF16)
    wkv = w_kv_up.reshape(MLA_KV_RANK, N_HEADS, MLA_NOPE + MLA_V)
    wcat = jnp.zeros((HEADS_WIDTH + N_HEADS * MLA_ROPE, N_HEADS * Q_CAT), F32)
    wv_pad = jnp.zeros((N_HEADS, Q_CAT, HEADS_WIDTH), F32)
    eye = jnp.eye(MLA_ROPE, dtype=F32)
    for h in range(N_HEADS):
        wcat = wcat.at[h * MLA_NOPE:(h + 1) * MLA_NOPE, h * Q_CAT:h * Q_CAT + MLA_KV_RANK].set(
            wkv[:, h, :MLA_NOPE].T)
        wcat = wcat.at[HEADS_WIDTH + h * MLA_ROPE:HEADS_WIDTH + (h + 1) * MLA_ROPE,
                       h * Q_CAT + MLA_KV_RANK:h * Q_CAT + MLA_KV_RANK + MLA_ROPE].set(eye)
        wv_pad = wv_pad.at[h, :MLA_KV_RANK, h * MLA_V:(h + 1) * MLA_V].set(wkv[:, h, MLA_NOPE:])
    return w, bf, qn, wq, wcat.astype(BF16), wv_pad.astype(BF16)


def _rope_tables(pos):
    half = MLA_ROPE // 2
    inv = ROPE_THETA ** (-jnp.arange(half, dtype=F32) / half)
    ang = pos.astype(F32)[:, None] * inv[None, :]
    cos = jnp.cos(ang)
    sin = jnp.sin(ang)
    reps = LANES // MLA_ROPE
    return (jnp.tile(jnp.concatenate([cos, cos], axis=1), (1, reps)),
            jnp.tile(jnp.concatenate([-sin, sin], axis=1), (1, reps)))


def _band_bias(rel_bias, q0, nq, k0, nk):
    period = nq + nk
    j = np.arange(period)
    d = np.where(j < nk, j, j - period)
    idx = np.clip(q0 - k0 - d, -REL_CLIP, REL_CLIP) + REL_CLIP
    u = rel_bias[:, idx]
    rows = jnp.tile(u, (1, nq))[:, :nq * (period - 1)].reshape(rel_bias.shape[0], nq, period - 1)
    return rows[:, :, :nk]


def _key_major(cache):
    l, b, p, h, d = cache.shape
    return jnp.transpose(cache, (0, 1, 3, 4, 2)).reshape(l, b, h * d, p)


def _from_key_major(state, n_heads):
    b, hd, t = state.shape
    return jnp.transpose(state.reshape(b, n_heads, hd // n_heads, t), (0, 3, 1, 2))


def kernel(x_prompt, x_sample, cache_fox_k, cache_fox_v, cache_fox_logf, cache_mla_latent, cache_mla_krope, cache_band_k, cache_band_v, cache_sb_k, cache_sb_v, w_in, b_forget, mla_q_norm, w_q_up, mla_kv_norm, w_kv_up, rel_bias, w_branch, w_gate, b_gate, w_out, ln1_g, ln1_b, w_router, router_bias, w_e_gate, w_e_up, w_e_down, ln2_g, ln2_b):
    batch, seq, _ = x_prompt.shape
    dec_batch, t_new, _ = x_sample.shape
    depth = w_in.shape[0]
    past = cache_fox_k.shape[2]
    n_band = cache_band_k.shape[2]
    n_p = batch * seq
    n_s = dec_batch * t_new
    alpha = (2 * depth) ** 0.25
    band_tile = BAND_CHUNKS * CHUNK
    assert seq % ATT_TILE == 0 and seq % band_tile == 0 and seq % TOKEN_TILE == 0
    assert n_p % MOE_TILE == 0 and n_s % MOE_TILE == 0 and n_s % TOKEN_TILE == 0
    assert past % LANES == 0 and t_new <= LANES and t_new % 16 == 0 and ATT_TILE % SB_SUB == 0
    assert past % min(SAMPLE_KEY_TILE, past) == 0 and past % SB_SUB == 0

    xp = x_prompt.reshape(n_p, D_MODEL)
    xs = x_sample.reshape(n_s, D_MODEL)
    rope_p = _rope_tables(jnp.tile(jnp.arange(seq, dtype=jnp.int32), batch))
    rope_s = _rope_tables(jnp.tile(past + jnp.arange(t_new, dtype=jnp.int32), dec_batch))

    c_fox_k, c_fox_v = _key_major(cache_fox_k), _key_major(cache_fox_v)
    c_band_k, c_band_v = _key_major(cache_band_k), _key_major(cache_band_v)
    c_sb_k, c_sb_v = _key_major(cache_sb_k), _key_major(cache_sb_v)
    c_logf = jnp.swapaxes(cache_fox_logf, 2, 3).astype(F32)
    c_krope = jnp.swapaxes(cache_mla_krope, 2, 3)

    wr = _pad_cols(w_router, LANES).astype(BF16)
    rb = router_bias.astype(F32)[:, None]
    total = past + t_new
    padded = -(-total // LANES) * LANES

    prompt_states, sample_states = [], []
    for l in range(depth):
        w, bf, qn, wq, wcat, wv_pad = _prep_in_proj_weights(w_in[l], b_forget[l], mla_q_norm[l],
                                                             w_q_up[l], w_kv_up[l])
        proj_w = (w, bf, qn, wq, mla_kv_norm[l][None, :], wcat)
        (qkv_p, ka_p, va_p, kc_p, vc_p, kd_p, vd_p, logf_p, lat_p, kr_p, kcat_p, qcat_p) = _in_proj(
            xp, *proj_w, *rope_p, batch=batch, seq=seq)
        (qkv_s, ka_s, va_s, kc_s, vc_s, kd_s, vd_s, logf_s, lat_s, kr_s, kcat_s, qcat_s) = _in_proj(
            xs, *proj_w, *rope_s)
        logf_s = logf_s[:, :N_HEADS].reshape(dec_batch, t_new, N_HEADS)
        kr_s = kr_s[:, :MLA_ROPE]

        c_p = _cumsum_rows(logf_p.reshape(batch * N_HEADS, seq)).reshape(batch, N_HEADS, seq)
        all_s = jnp.concatenate([c_logf[l], jnp.swapaxes(logf_s, 1, 2)], axis=2)
        all_s = jnp.pad(all_s, ((0, 0), (0, 0), (0, padded - total)))
        c_s = _cumsum_rows(all_s.reshape(dec_batch * N_HEADS, padded)).reshape(dec_batch, N_HEADS, padded)
        cq_s = c_s[:, :, past:past + t_new].reshape(dec_batch, N_HEADS * t_new, 1)

        bias_p = _band_bias(rel_bias[l], 0, band_tile, -band_tile, 2 * band_tile)
        bias_s = _band_bias(rel_bias[l], past, t_new, past - n_band, n_band + t_new)
        bias_s = bias_s.reshape(N_HEADS * t_new, n_band + t_new)

        branches_p = (_fox_prompt(qkv_p, jnp.swapaxes(c_p, 1, 2), c_p, batch, seq),
                      _mla_prompt(qcat_p, kcat_p, wv_pad, batch, seq),
                      _band_prompt(qkv_p, bias_p, batch, seq),
                      _sb_prompt(qkv_p, batch, seq))
        branches_s = (_fox_sample(qkv_s, c_fox_k, c_fox_v, cq_s, c_s, l, dec_batch, t_new),
                      _mla_sample(qcat_s, kcat_s, cache_mla_latent, c_krope, wv_pad, l, dec_batch, t_new),
                      _band_sample(qkv_s, c_band_k, c_band_v, bias_s[:, :n_band], bias_s[:, n_band:], l,
                                   past, dec_batch, t_new),
                      _sb_sample(qkv_s, c_sb_k, c_sb_v, l, dec_batch, t_new))

        tail_w = (w_gate[l].astype(BF16), b_gate[l][:, None, :], w_branch[l].astype(BF16),
                  w_out[l].astype(BF16), ln1_g[l][None, :], ln1_b[l][None, :], wr, alpha)
        experts = (w_e_gate[l].astype(BF16), w_e_up[l].astype(BF16), w_e_down[l].astype(BF16),
                   ln2_g[l][None, :], ln2_b[l][None, :], alpha)
        x1_p, logits_p = _merge(xp, branches_p, *tail_w)
        x1_s, logits_s = _merge(xs, branches_s, *tail_w)
        xp = _moe(x1_p, _route(logits_p[:, :N_EXPERTS].T, rb).T, *experts)
        xs = _moe(x1_s, _route(logits_s[:, :N_EXPERTS].T, rb).T, *experts)

        n_keep = min(band_tile, seq)
        prompt_states.append((
            _from_key_major(ka_p, N_HEADS), _from_key_major(va_p, N_HEADS), jnp.swapaxes(logf_p, 1, 2),
            lat_p.reshape(batch, seq, -1), jnp.swapaxes(kr_p, 1, 2),
            _from_key_major(kc_p[:, :, seq - n_keep:], N_HEADS),
            _from_key_major(vc_p[:, :, seq - n_keep:], N_HEADS),
            _from_key_major(kd_p, N_HEADS), _from_key_major(vd_p, N_HEADS)))
        heads = lambda a: a.reshape(dec_batch, t_new, N_HEADS, HEAD_DIM)
        sample_states.append((
            heads(ka_s), heads(va_s), logf_s, lat_s.reshape(dec_batch, t_new, -1),
            kr_s.reshape(dec_batch, t_new, -1), heads(kc_s), heads(vc_s), heads(kd_s), heads(vd_s)))

    stack = lambda states: [jnp.stack(z, axis=0) for z in zip(*states)]
    return (xp.reshape(batch, seq, D_MODEL), xs.reshape(dec_batch, t_new, D_MODEL),
            *stack(prompt_states), *stack(sample_states))
```

```python
import functools

import numpy as np
import jax
import jax.numpy as jnp
from jax import lax
from jax.experimental import pallas as pl
from jax.experimental.pallas import tpu as pltpu

F32 = jnp.float32
BF16 = jnp.bfloat16

D_MODEL = 1024
N_HEADS = 4
HEAD_DIM = 64
HEADS_WIDTH = N_HEADS * HEAD_DIM
CHUNK = 64
BAND_CHUNKS = 8
REL_CLIP = 128
MLA_NOPE = 64
MLA_ROPE = 32
MLA_V = 64
MLA_Q_RANK = 192
MLA_KV_RANK = 128
ROPE_THETA = 10000.0
N_BRANCH = 4
N_EXPERTS = 16
N_GROUPS = 4
EXPERTS_PER_GROUP = N_EXPERTS // N_GROUPS
D_EXPERT = 512
GROUP_PAIRS = tuple((i, j) for i in range(EXPERTS_PER_GROUP) for j in range(i + 1, EXPERTS_PER_GROUP))
N_BUCKETS = N_GROUPS * len(GROUP_PAIRS)
BUCKET_ROWS = 32
NEG_INF = -1e30
LANES = 128
VMEM_LIMIT = 56 * 1024 * 1024

QKV_WIDTH = 9 * HEADS_WIDTH
STATE_COLS = (1, 2, 4, 5, 7, 8)
OFF_CQ = QKV_WIDTH
OFF_CKV = OFF_CQ + 256
OFF_KR = OFF_CKV + MLA_KV_RANK
OFF_FA = OFF_KR + LANES
PROJ_WIDTH = OFF_FA + LANES
Q_CAT = 256

TOKEN_TILE = 512
MOE_TILE = 256
MOE_TILE_SAMPLE = 128
GATHER_ROWS = 2048
ATT_TILE = 512
SB_SUB = 256
SAMPLE_KEY_TILE = 4096
SB_DEAD = -105.0


def _dot(a, b):
    return jnp.dot(a, b, preferred_element_type=F32)


def _dot_nt(a, b):
    return lax.dot_general(a, b, (((1,), (1,)), ((), ())), preferred_element_type=F32)


def _const_spec(shape):
    nd = len(shape)
    return pl.BlockSpec(shape, lambda *_: (0,) * nd, pipeline_mode=pl.Buffered(1))


def _params(*sem):
    return pltpu.CompilerParams(dimension_semantics=sem, vmem_limit_bytes=VMEM_LIMIT)


def _log_sigmoid(x):
    return jnp.minimum(x, 0.0) - jnp.log1p(jnp.exp(-jnp.abs(x)))


def _softplus(x):
    return jnp.maximum(x, 0.0) + jnp.log1p(jnp.exp(-jnp.abs(x)))


def _layer_norm(h, g, b):
    mu = jnp.mean(h, axis=-1, keepdims=True)
    d = h - mu
    var = jnp.mean(d * d, axis=-1, keepdims=True)
    return d * lax.rsqrt(var + 1e-5) * g + b


def _head_of_lane(shape):
    return lax.broadcasted_iota(jnp.int32, shape, len(shape) - 1) // HEAD_DIM


def _in_proj_kernel(x_ref, w_ref, bf_ref, qn_ref, wq_ref, kvn_ref, wcat_ref, cos_ref, sin_ref,
                    qkv_ref, ka_ref, va_ref, kc_ref, vc_ref, kd_ref, vd_ref,
                    logf_ref, lat_ref, kr_ref, kcat_ref, qcat_ref, *key_major_refs, key_major):
    xb = x_ref[...].astype(BF16)
    proj = _dot(xb, w_ref[...])
    qkv_ref[...] = proj[:, :QKV_WIDTH].astype(BF16)
    for ref, col in zip((ka_ref, va_ref, kc_ref, vc_ref, kd_ref, vd_ref), STATE_COLS):
        state = proj[:, col * HEADS_WIDTH:(col + 1) * HEADS_WIDTH]
        if key_major:
            state = state.T
            if ref is va_ref:
                key_major_refs[0][...] = state.astype(BF16)
        ref[...] = state

    cos = cos_ref[...]
    sin = sin_ref[...]
    lane = lax.broadcasted_iota(jnp.int32, cos.shape, 1)
    first_half = (lane % MLA_ROPE) < (MLA_ROPE // 2)

    def rope(v):
        partner = jnp.where(first_half, pltpu.roll(v, LANES - MLA_ROPE // 2, 1),
                            pltpu.roll(v, MLA_ROPE // 2, 1))
        return v * cos + partner * sin

    cq = proj[:, OFF_CQ:OFF_CQ + 256]
    ms = jnp.sum(cq * cq, axis=-1, keepdims=True) * (1.0 / MLA_Q_RANK)
    cqn = cq * lax.rsqrt(ms + 1e-6) * qn_ref[...]
    qb = _dot(cqn.astype(BF16), wq_ref[...])
    q_rope = rope(qb[:, HEADS_WIDTH:])
    q_in = jnp.concatenate([qb[:, :HEADS_WIDTH], q_rope], axis=1) * ((MLA_NOPE + MLA_ROPE) ** -0.5)
    qcat_ref[...] = _dot(q_in.astype(BF16), wcat_ref[...]).astype(BF16)

    ckv = proj[:, OFF_CKV:OFF_CKV + MLA_KV_RANK]
    lat = ckv * lax.rsqrt(jnp.mean(ckv * ckv, axis=-1, keepdims=True) + 1e-6) * kvn_ref[...]
    lat_ref[...] = lat
    kr = rope(proj[:, OFF_KR:OFF_KR + LANES])
    kcat_ref[...] = jnp.concatenate([lat, kr], axis=1).astype(BF16)
    logf = _log_sigmoid(proj[:, OFF_FA:OFF_FA + LANES] + bf_ref[...])
    if key_major:
        kr_ref[...] = kr.T[:MLA_ROPE, :]
        logf_ref[...] = logf.T[:N_HEADS, :]
        key_major_refs[1][...] = lat.T.astype(BF16)
    else:
        kr_ref[...] = kr
        logf_ref[...] = logf


def _in_proj(x, w, bf, qn, wq, kvn, wcat, cos, sin, *, batch=None, seq=None):
    n = x.shape[0]
    tm = TOKEN_TILE
    key_major = batch is not None
    row = lambda width: pl.BlockSpec((tm, width), lambda i: (i, 0))
    if key_major:
        per = seq // tm
        tr = lambda height: pl.BlockSpec((None, height, tm), lambda i: (i // per, 0, i % per))
        tshape = lambda height: jax.ShapeDtypeStruct((batch, height, seq), F32)
        state_shapes = [tshape(HEADS_WIDTH)] * 6 + [tshape(N_HEADS), jax.ShapeDtypeStruct((n, LANES), F32),
                                                    tshape(MLA_ROPE)]
        state_specs = [tr(HEADS_WIDTH)] * 6 + [tr(N_HEADS), row(LANES), tr(MLA_ROPE)]
    else:
        state_shapes = ([jax.ShapeDtypeStruct((n, HEADS_WIDTH), F32)] * 6
                        + [jax.ShapeDtypeStruct((n, LANES), F32)] * 3)
        state_specs = [row(HEADS_WIDTH)] * 6 + [row(LANES)] * 3
    out_shape = ([jax.ShapeDtypeStruct((n, QKV_WIDTH), BF16)] + state_shapes
                 + [jax.ShapeDtypeStruct((n, 2 * LANES), BF16),
                    jax.ShapeDtypeStruct((n, N_HEADS * Q_CAT), BF16)])
    out_specs = [row(QKV_WIDTH)] + state_specs + [row(2 * LANES), row(N_HEADS * Q_CAT)]
    if key_major:
        out_shape += [jax.ShapeDtypeStruct((batch, HEADS_WIDTH, seq), BF16),
                      jax.ShapeDtypeStruct((batch, MLA_KV_RANK, seq), BF16)]
        out_specs += [tr(HEADS_WIDTH), tr(MLA_KV_RANK)]
    return pl.pallas_call(
        functools.partial(_in_proj_kernel, key_major=key_major),
        out_shape=out_shape,
        grid=(n // tm,),
        in_specs=[row(D_MODEL), _const_spec(w.shape), _const_spec(bf.shape), _const_spec(qn.shape),
                  _const_spec(wq.shape), _const_spec(kvn.shape), _const_spec(wcat.shape),
                  row(LANES), row(LANES)],
        out_specs=out_specs,
        compiler_params=_params("parallel"),
        name="in_proj_prompt" if key_major else "in_proj_sample",
    )(x, w, bf, qn, wq, kvn, wcat, cos, sin)


def _cumsum_kernel(x_ref, o_ref):
    r, t = x_ref.shape
    row = lax.broadcasted_iota(jnp.int32, (LANES, LANES), 0)
    col = lax.broadcasted_iota(jnp.int32, (LANES, LANES), 1)
    tri = jnp.where(row <= col, 1.0, 0.0).astype(BF16)
    carry = jnp.zeros((r, 1), F32)
    for c in range(t // LANES):
        xc = x_ref[:, c * LANES:(c + 1) * LANES]
        hi = xc.astype(BF16)
        r1 = xc - hi.astype(F32)
        mid = r1.astype(BF16)
        lo = (r1 - mid.astype(F32)).astype(BF16)
        y = _dot(hi, tri) + _dot(mid, tri) + _dot(lo, tri) + carry
        o_ref[:, c * LANES:(c + 1) * LANES] = y
        carry = y[:, LANES - 1:LANES]


def _cumsum_rows(x):
    return pl.pallas_call(
        _cumsum_kernel,
        out_shape=jax.ShapeDtypeStruct(x.shape, F32),
        name="cumsum",
    )(x)


def _online_softmax_update(s, pv, m_ref, l_ref, acc_ref, h):
    m_prev = m_ref[h]
    m_new = jnp.maximum(m_prev, jnp.max(s, axis=1, keepdims=True))
    alpha = jnp.exp(m_prev - m_new)
    p = jnp.exp(s - m_new)
    l_ref[h] = alpha * l_ref[h] + jnp.sum(p, axis=1, keepdims=True)
    acc_ref[h] = alpha * acc_ref[h] + pv(p.astype(BF16))
    m_ref[h] = m_new


def _init_softmax_state(m_ref, l_ref, acc_ref):
    m_ref[...] = jnp.full(m_ref.shape, NEG_INF, F32)
    l_ref[...] = jnp.zeros(l_ref.shape, F32)
    acc_ref[...] = jnp.zeros(acc_ref.shape, F32)


def _store_head_masked_q(q_ref, qs_ref, rows):
    q = q_ref[...] * (HEAD_DIM ** -0.5)
    head = _head_of_lane(q.shape)
    for h in range(N_HEADS):
        qs_ref[h * rows:(h + 1) * rows, :] = jnp.where(head == h, q, jnp.zeros_like(q))


def _merge_heads(per_head):
    head = _head_of_lane(per_head[0].shape)
    out = jnp.zeros_like(per_head[0])
    for h in range(N_HEADS):
        out = jnp.where(head == h, per_head[h], out)
    return out


def _online_softmax_update_t(st, pv, m_ref, l_ref, acc_ref, h):
    m_prev = m_ref[h]
    m_new = jnp.maximum(m_prev, jnp.max(st, axis=0, keepdims=True))
    alpha = jnp.exp(m_prev - m_new)
    p = jnp.exp(st - m_new)
    l_ref[h] = alpha * l_ref[h] + jnp.sum(p, axis=0, keepdims=True)
    acc_ref[h] = alpha * acc_ref[h] + pv(p.astype(BF16))
    m_ref[h] = m_new


def _fox_prompt_kernel(q_ref, k_ref, vt_ref, cq_ref, ck_ref, o_ref, qt_ref, m_ref, l_ref, acc_ref):
    qi = pl.program_id(1)
    ki = pl.program_id(2)

    @pl.when(ki == 0)
    def _():
        _init_softmax_state(m_ref, l_ref, acc_ref)
        qt = q_ref[...].astype(F32).T * (HEAD_DIM ** -0.5)
        head = lax.broadcasted_iota(jnp.int32, qt.shape, 0) // HEAD_DIM
        for h in range(N_HEADS):
            qt_ref[h] = jnp.where(head == h, qt, 0.0).astype(BF16)

    def step(diagonal):
        k = k_ref[...]
        vt = vt_ref[...]
        cq = cq_ref[...]
        ck = ck_ref[...]
        for h in range(N_HEADS):
            st = _dot(k, qt_ref[h]) + cq[h:h + 1, :] - ck[:, h:h + 1]
            if diagonal:
                key = lax.broadcasted_iota(jnp.int32, st.shape, 0)
                qry = lax.broadcasted_iota(jnp.int32, st.shape, 1)
                st = jnp.where(key <= qry, st, NEG_INF)
            vt_h = vt[h * HEAD_DIM:(h + 1) * HEAD_DIM, :]
            _online_softmax_update_t(st, lambda p: _dot(vt_h, p), m_ref, l_ref, acc_ref, h)

    pl.when(ki < qi)(lambda: step(False))
    pl.when(ki == qi)(lambda: step(True))

    @pl.when(ki == pl.num_programs(2) - 1)
    def _():
        ot = jnp.concatenate([acc_ref[h] / l_ref[h] for h in range(N_HEADS)], axis=0)
        o_ref[...] = ot.T.astype(BF16)


def _fox_prompt(qkv, vt, c_tok, c_head, batch, seq):
    t = ATT_TILE
    nt = seq // t
    return pl.pallas_call(
        _fox_prompt_kernel,
        out_shape=jax.ShapeDtypeStruct((batch * seq, HEADS_WIDTH), BF16),
        grid=(batch, nt, nt),
        in_specs=[pl.BlockSpec((t, HEADS_WIDTH), lambda b, i, j: (b * nt + i, 0)),
                  pl.BlockSpec((t, HEADS_WIDTH), lambda b, i, j: (b * nt + jnp.minimum(i, j), 1)),
                  pl.BlockSpec((None, HEADS_WIDTH, t), lambda b, i, j: (b, 0, jnp.minimum(i, j))),
                  pl.BlockSpec((None, N_HEADS, t), lambda b, i, j: (b, 0, i)),
                  pl.BlockSpec((None, t, N_HEADS), lambda b, i, j: (b, jnp.minimum(i, j), 0))],
        out_specs=pl.BlockSpec((t, HEADS_WIDTH), lambda b, i, j: (b * nt + i, 0)),
        scratch_shapes=[pltpu.VMEM((N_HEADS, HEADS_WIDTH, t), BF16),
                        pltpu.VMEM((N_HEADS, 1, t), F32), pltpu.VMEM((N_HEADS, 1, t), F32),
                        pltpu.VMEM((N_HEADS, HEAD_DIM, t), F32)],
        compiler_params=_params("parallel", "parallel", "arbitrary"),
        name="fox_prompt",
    )(qkv, qkv, vt, c_head, c_tok)


def _mla_prompt_kernel(q_ref, k_ref, latt_ref, wv_ref, o_ref, qt_ref, m_ref, l_ref, acc_ref):
    qi = pl.program_id(1)
    ki = pl.program_id(2)

    @pl.when(ki == 0)
    def _():
        _init_softmax_state(m_ref, l_ref, acc_ref)
        for h in range(N_HEADS):
            qt_ref[h] = q_ref[:, h * Q_CAT:(h + 1) * Q_CAT].astype(F32).T.astype(BF16)

    def step(diagonal):
        k = k_ref[...]
        latt = latt_ref[...]
        for h in range(N_HEADS):
            st = _dot(k, qt_ref[h])
            if diagonal:
                key = lax.broadcasted_iota(jnp.int32, st.shape, 0)
                qry = lax.broadcasted_iota(jnp.int32, st.shape, 1)
                st = jnp.where(key // CHUNK <= qry // CHUNK, st, NEG_INF)
            _online_softmax_update_t(st, lambda p: _dot(latt, p), m_ref, l_ref, acc_ref, h)

    pl.when(ki < qi)(lambda: step(False))
    pl.when(ki == qi)(lambda: step(True))

    @pl.when(ki == pl.num_programs(2) - 1)
    def _():
        out = None
        for h in range(N_HEADS):
            o_h = (acc_ref[h] / l_ref[h]).T.astype(BF16)
            term = _dot(o_h, wv_ref[h, :MLA_KV_RANK, :])
            out = term if out is None else out + term
        o_ref[...] = out.astype(BF16)


def _mla_prompt(qcat, kcat, latt, wv_pad, batch, seq):
    t = ATT_TILE
    nt = seq // t
    return pl.pallas_call(
        _mla_prompt_kernel,
        out_shape=jax.ShapeDtypeStruct((batch * seq, HEADS_WIDTH), BF16),
        grid=(batch, nt, nt),
        in_specs=[pl.BlockSpec((t, N_HEADS * Q_CAT), lambda b, i, j: (b * nt + i, 0)),
                  pl.BlockSpec((t, Q_CAT), lambda b, i, j: (b * nt + jnp.minimum(i, j), 0)),
                  pl.BlockSpec((None, MLA_KV_RANK, t), lambda b, i, j: (b, 0, jnp.minimum(i, j))),
                  _const_spec(wv_pad.shape)],
        out_specs=pl.BlockSpec((t, HEADS_WIDTH), lambda b, i, j: (b * nt + i, 0)),
        scratch_shapes=[pltpu.VMEM((N_HEADS, Q_CAT, t), BF16),
                        pltpu.VMEM((N_HEADS, 1, t), F32), pltpu.VMEM((N_HEADS, 1, t), F32),
                        pltpu.VMEM((N_HEADS, MLA_KV_RANK, t), F32)],
        compiler_params=_params("parallel", "parallel", "arbitrary"),
        name="mla_prompt",
    )(qcat, kcat, latt, wv_pad)


def _strict_upper_ones(n):
    row = lax.broadcasted_iota(jnp.int32, (n, n), 0)
    col = lax.broadcasted_iota(jnp.int32, (n, n), 1)
    return jnp.where(row > col, 1.0, 0.0).astype(BF16)


def _sb_block(z, u, rest, mask):
    sp = _softplus(z)
    log_1m = -sp
    if mask is not None:
        log_1m = jnp.where(mask, log_1m, 0.0)
    hi = log_1m.astype(BF16)
    lo = (log_1m - hi.astype(F32)).astype(BF16)
    later = _dot(hi, u) + _dot(lo, u) + rest
    a = jnp.exp(z - sp + later)
    if mask is not None:
        a = jnp.where(mask, a, 0.0)
    return a.astype(BF16), rest + jnp.sum(log_1m, axis=1, keepdims=True)


def _sb_prompt_kernel(q_ref, k_ref, v_ref, o_ref, qs_ref, u_ref, rest_ref, acc_ref):
    tq = q_ref.shape[0]
    qi = pl.program_id(1)
    _store_head_masked_q(q_ref, qs_ref, tq)
    u_ref[...] = _strict_upper_ones(SB_SUB)
    rest_ref[...] = jnp.zeros(rest_ref.shape, F32)
    acc_ref[...] = jnp.zeros(acc_ref.shape, F32)

    def visit(start, first_col):
        k = k_ref[pl.ds(start, SB_SUB), :]
        v = v_ref[pl.ds(start, SB_SUB), :]
        u = u_ref[...]
        live = None
        for h in range(N_HEADS):
            z = _dot_nt(qs_ref[h * tq:(h + 1) * tq, :], k)
            mask = None
            if first_col is not None:
                row = lax.broadcasted_iota(jnp.int32, z.shape, 0)
                col = lax.broadcasted_iota(jnp.int32, z.shape, 1) + first_col
                mask = col < row
            a, rest = _sb_block(z, u, rest_ref[h], mask)
            acc_ref[h] += _dot(a, v)
            rest_ref[h] = rest
            top = jnp.max(rest)
            live = top if live is None else jnp.maximum(live, top)
        return live

    live = None
    for sub in reversed(range(tq // SB_SUB)):
        live = visit(pl.multiple_of(qi * tq + sub * SB_SUB, SB_SUB), sub * SB_SUB)

    def more(carry):
        return jnp.logical_and(carry[0] >= 0, carry[1] > SB_DEAD)

    def older(carry):
        return carry[0] - 1, visit(pl.multiple_of(carry[0] * SB_SUB, SB_SUB), None)

    lax.while_loop(more, older, (qi * (tq // SB_SUB) - 1, live))
    o_ref[...] = _merge_heads([acc_ref[h] for h in range(N_HEADS)]).astype(BF16)


def _sb_prompt(qkv, batch, seq):
    t = ATT_TILE
    nt = seq // t
    return pl.pallas_call(
        _sb_prompt_kernel,
        out_shape=jax.ShapeDtypeStruct((batch * seq, HEADS_WIDTH), BF16),
        grid=(batch, nt),
        in_specs=[pl.BlockSpec((t, HEADS_WIDTH), lambda b, i: (b * nt + i, 6)),
                  pl.BlockSpec((seq, HEADS_WIDTH), lambda b, i: (b, 7)),
                  pl.BlockSpec((seq, HEADS_WIDTH), lambda b, i: (b, 8))],
        out_specs=pl.BlockSpec((t, HEADS_WIDTH), lambda b, i: (b * nt + i, 0)),
        scratch_shapes=[pltpu.VMEM((N_HEADS * t, HEADS_WIDTH), BF16),
                        pltpu.VMEM((SB_SUB, SB_SUB), BF16),
                        pltpu.VMEM((N_HEADS, t, 1), F32),
                        pltpu.VMEM((N_HEADS, t, HEADS_WIDTH), F32)],
        compiler_params=_params("parallel", "arbitrary"),
        name="sb_prompt",
    )(qkv, qkv, qkv)


def _band_prompt_kernel(q_ref, kp_ref, kc_ref, vp_ref, vc_ref, bias_ref, o_ref, qs_ref):
    tq = q_ref.shape[0]
    qi = pl.program_id(1)
    _store_head_masked_q(q_ref, qs_ref, tq)
    k = jnp.concatenate([kp_ref[...], kc_ref[...]], axis=0)
    v = jnp.concatenate([vp_ref[...], vc_ref[...]], axis=0)
    row_chunk = lax.broadcasted_iota(jnp.int32, (tq, 2 * tq), 0) // CHUNK
    col = lax.broadcasted_iota(jnp.int32, (tq, 2 * tq), 1)
    key_chunk = col // CHUNK - tq // CHUNK
    valid = (key_chunk <= row_chunk) & (key_chunk >= row_chunk - BAND_CHUNKS)
    valid = valid & ((col >= tq) | (qi > 0))
    outs = []
    for h in range(N_HEADS):
        s = _dot_nt(qs_ref[h * tq:(h + 1) * tq, :], k) + bias_ref[h]
        s = jnp.where(valid, s, NEG_INF)
        p = jnp.exp(s - jnp.max(s, axis=1, keepdims=True))
        denom = jnp.sum(p, axis=1, keepdims=True)
        outs.append(_dot(p.astype(BF16), v) / denom)
    o_ref[...] = _merge_heads(outs).astype(BF16)


def _band_prompt(qkv, bias_tile, batch, seq):
    t = BAND_CHUNKS * CHUNK
    nt = seq // t
    prev = lambda b, i: b * nt + jnp.maximum(i - 1, 0)
    return pl.pallas_call(
        _band_prompt_kernel,
        out_shape=jax.ShapeDtypeStruct((batch * seq, HEADS_WIDTH), BF16),
        grid=(batch, nt),
        in_specs=[pl.BlockSpec((t, HEADS_WIDTH), lambda b, i: (b * nt + i, 3)),
                  pl.BlockSpec((t, HEADS_WIDTH), lambda b, i: (prev(b, i), 4)),
                  pl.BlockSpec((t, HEADS_WIDTH), lambda b, i: (b * nt + i, 4)),
                  pl.BlockSpec((t, HEADS_WIDTH), lambda b, i: (prev(b, i), 5)),
                  pl.BlockSpec((t, HEADS_WIDTH), lambda b, i: (b * nt + i, 5)),
                  _const_spec(bias_tile.shape)],
        out_specs=pl.BlockSpec((t, HEADS_WIDTH), lambda b, i: (b * nt + i, 0)),
        scratch_shapes=[pltpu.VMEM((N_HEADS * t, HEADS_WIDTH), BF16)],
        compiler_params=_params("parallel", "parallel"),
        name="band_prompt",
    )(qkv, qkv, qkv, qkv, qkv, bias_tile)


def _rows_per_head(x, rows):
    return jnp.concatenate([jnp.broadcast_to(x[h:h + 1, :], (rows, x.shape[1])) for h in range(N_HEADS)],
                           axis=0)


def _unstack_heads(acc, rows):
    return _merge_heads([acc[h * rows:(h + 1) * rows, :] for h in range(N_HEADS)])


def _fox_sample_kernel(q_ref, kc_ref, vc_ref, kn_ref, vn_ref, cq_ref, ckc_ref, ckn_ref, o_ref,
                       qs_ref, m_ref, l_ref, acc_ref):
    t = q_ref.shape[0]
    ki = pl.program_id(1)

    @pl.when(ki == 0)
    def _():
        _init_softmax_state(m_ref, l_ref, acc_ref)
        _store_head_masked_q(q_ref, qs_ref, t)

    qs = qs_ref[...]
    cq = cq_ref[...]
    s = _dot(qs, kc_ref[...].astype(BF16)) + cq - _rows_per_head(ckc_ref[...], t)
    vc = vc_ref[...].astype(BF16)
    _online_softmax_update(s, lambda p: _dot_nt(p, vc), m_ref, l_ref, acc_ref, 0)

    @pl.when(ki == pl.num_programs(1) - 1)
    def _():
        s = _dot_nt(qs, kn_ref[...]) + cq - _rows_per_head(ckn_ref[:, :t], t)
        row = lax.broadcasted_iota(jnp.int32, s.shape, 0) % t
        col = lax.broadcasted_iota(jnp.int32, s.shape, 1)
        s = jnp.where(col <= row, s, NEG_INF)
        _online_softmax_update(s, lambda p: _dot(p, vn_ref[...]), m_ref, l_ref, acc_ref, 0)
        o_ref[...] = _unstack_heads(acc_ref[0] / l_ref[0], t).astype(BF16)


def _fox_sample(qkv, cache_kt, cache_vt, cq_rows, c_head, layer, dec_batch, t):
    past = cache_kt.shape[3]
    tk = min(SAMPLE_KEY_TILE, past)
    nk = past // tk
    return pl.pallas_call(
        _fox_sample_kernel,
        out_shape=jax.ShapeDtypeStruct((dec_batch * t, HEADS_WIDTH), BF16),
        grid=(dec_batch, nk),
        in_specs=[pl.BlockSpec((t, HEADS_WIDTH), lambda b, j: (b, 0)),
                  pl.BlockSpec((None, None, HEADS_WIDTH, tk), lambda b, j: (layer, b, 0, j)),
                  pl.BlockSpec((None, None, HEADS_WIDTH, tk), lambda b, j: (layer, b, 0, j)),
                  pl.BlockSpec((t, HEADS_WIDTH), lambda b, j: (b, 1)),
                  pl.BlockSpec((t, HEADS_WIDTH), lambda b, j: (b, 2)),
                  pl.BlockSpec((None, N_HEADS * t, 1), lambda b, j: (b, 0, 0)),
                  pl.BlockSpec((None, N_HEADS, tk), lambda b, j: (b, 0, j)),
                  pl.BlockSpec((None, N_HEADS, LANES), lambda b, j: (b, 0, past // LANES))],
        out_specs=pl.BlockSpec((t, HEADS_WIDTH), lambda b, j: (b, 0)),
        scratch_shapes=[pltpu.VMEM((N_HEADS * t, HEADS_WIDTH), BF16),
                        pltpu.VMEM((1, N_HEADS * t, 1), F32), pltpu.VMEM((1, N_HEADS * t, 1), F32),
                        pltpu.VMEM((1, N_HEADS * t, HEADS_WIDTH), F32)],
        compiler_params=_params("parallel", "arbitrary"),
        name="fox_sample",
    )(qkv, cache_kt, cache_vt, qkv, qkv, cq_rows, c_head, c_head)


def _mla_sample_kernel(q_ref, latc_ref, krc_ref, kn_ref, wv_ref, o_ref,
                       qa_ref, qr_ref, m_ref, l_ref, acc_ref, *, past):
    t = q_ref.shape[0]
    ki = pl.program_id(1)

    @pl.when(ki == 0)
    def _():
        _init_softmax_state(m_ref, l_ref, acc_ref)
        for h in range(N_HEADS):
            qa_ref[h * t:(h + 1) * t, :] = q_ref[:, h * Q_CAT:h * Q_CAT + MLA_KV_RANK]
            qr_ref[h * t:(h + 1) * t, :] = q_ref[:, h * Q_CAT + MLA_KV_RANK:
                                                 h * Q_CAT + MLA_KV_RANK + MLA_ROPE]

    lat = latc_ref[...].astype(BF16)
    s = _dot_nt(qa_ref[...], lat) + _dot(qr_ref[...], krc_ref[...].astype(BF16))
    _online_softmax_update(s, lambda p: _dot(p, lat), m_ref, l_ref, acc_ref, 0)

    @pl.when(ki == pl.num_programs(1) - 1)
    def _():
        lat_n = kn_ref[:, :MLA_KV_RANK]
        s = _dot_nt(qa_ref[...], lat_n) + _dot_nt(qr_ref[...],
                                                  kn_ref[:, MLA_KV_RANK:MLA_KV_RANK + MLA_ROPE])
        q_pos = past + lax.broadcasted_iota(jnp.int32, s.shape, 0) % t
        k_pos = past + lax.broadcasted_iota(jnp.int32, s.shape, 1)
        s = jnp.where(k_pos // CHUNK <= q_pos // CHUNK, s, NEG_INF)
        _online_softmax_update(s, lambda p: _dot(p, lat_n), m_ref, l_ref, acc_ref, 0)
        o = (acc_ref[0] / l_ref[0]).astype(BF16)
        out = None
        for h in range(N_HEADS):
            term = _dot(o[h * t:(h + 1) * t, :], wv_ref[h, :MLA_KV_RANK, :])
            out = term if out is None else out + term
        o_ref[...] = out.astype(BF16)


def _mla_sample(qcat, kcat, cache_lat, cache_krt, wv_pad, layer, dec_batch, t):
    past = cache_lat.shape[2]
    tk = min(SAMPLE_KEY_TILE, past)
    nk = past // tk
    return pl.pallas_call(
        functools.partial(_mla_sample_kernel, past=past),
        out_shape=jax.ShapeDtypeStruct((dec_batch * t, HEADS_WIDTH), BF16),
        grid=(dec_batch, nk),
        in_specs=[pl.BlockSpec((t, N_HEADS * Q_CAT), lambda b, j: (b, 0)),
                  pl.BlockSpec((None, None, tk, MLA_KV_RANK), lambda b, j: (layer, b, j, 0)),
                  pl.BlockSpec((None, None, MLA_ROPE, tk), lambda b, j: (layer, b, 0, j)),
                  pl.BlockSpec((t, Q_CAT), lambda b, j: (b, 0)),
                  _const_spec(wv_pad.shape)],
        out_specs=pl.BlockSpec((t, HEADS_WIDTH), lambda b, j: (b, 0)),
        scratch_shapes=[pltpu.VMEM((N_HEADS * t, MLA_KV_RANK), BF16),
                        pltpu.VMEM((N_HEADS * t, MLA_ROPE), BF16),
                        pltpu.VMEM((1, N_HEADS * t, 1), F32), pltpu.VMEM((1, N_HEADS * t, 1), F32),
                        pltpu.VMEM((1, N_HEADS * t, MLA_KV_RANK), F32)],
        compiler_params=_params("parallel", "arbitrary"),
        name="mla_sample",
    )(qcat, cache_lat, cache_krt, kcat, wv_pad)


def _band_sample_kernel(q_ref, kc_ref, vc_ref, kn_ref, vn_ref, biasc_ref, biasn_ref, o_ref, qs_ref,
                        *, past):
    t = q_ref.shape[0]
    n_band = kc_ref.shape[1]
    _store_head_masked_q(q_ref, qs_ref, t)
    qs = qs_ref[...]

    def masked(s, first_pos):
        q_chunk = (past + lax.broadcasted_iota(jnp.int32, s.shape, 0) % t) // CHUNK
        k_pos = first_pos + lax.broadcasted_iota(jnp.int32, s.shape, 1)
        k_chunk = k_pos // CHUNK
        valid = (k_pos >= 0) & (k_chunk <= q_chunk) & (k_chunk >= q_chunk - BAND_CHUNKS)
        return jnp.where(valid, s, NEG_INF)

    s_c = masked(_dot(qs, kc_ref[...].astype(BF16)) + biasc_ref[...], past - n_band)
    s_n = masked(_dot_nt(qs, kn_ref[...]) + biasn_ref[...], past)
    m = jnp.maximum(jnp.max(s_c, axis=1, keepdims=True), jnp.max(s_n, axis=1, keepdims=True))
    p_c = jnp.exp(s_c - m)
    p_n = jnp.exp(s_n - m)
    denom = jnp.sum(p_c, axis=1, keepdims=True) + jnp.sum(p_n, axis=1, keepdims=True)
    acc = _dot_nt(p_c.astype(BF16), vc_ref[...].astype(BF16)) + _dot(p_n.astype(BF16), vn_ref[...])
    o_ref[...] = _unstack_heads(acc / denom, t).astype(BF16)


def _band_sample(qkv, cache_kt, cache_vt, bias_cache, bias_new, layer, past, dec_batch, t):
    n_band = cache_kt.shape[3]
    return pl.pallas_call(
        functools.partial(_band_sample_kernel, past=past),
        out_shape=jax.ShapeDtypeStruct((dec_batch * t, HEADS_WIDTH), BF16),
        grid=(dec_batch,),
        in_specs=[pl.BlockSpec((t, HEADS_WIDTH), lambda b: (b, 3)),
                  pl.BlockSpec((None, None, HEADS_WIDTH, n_band), lambda b: (layer, b, 0, 0)),
                  pl.BlockSpec((None, None, HEADS_WIDTH, n_band), lambda b: (layer, b, 0, 0)),
                  pl.BlockSpec((t, HEADS_WIDTH), lambda b: (b, 4)),
                  pl.BlockSpec((t, HEADS_WIDTH), lambda b: (b, 5)),
                  _const_spec(bias_cache.shape), _const_spec(bias_new.shape)],
        out_specs=pl.BlockSpec((t, HEADS_WIDTH), lambda b: (b, 0)),
        scratch_shapes=[pltpu.VMEM((N_HEADS * t, HEADS_WIDTH), BF16)],
        compiler_params=_params("parallel"),
        name="band_sample",
    )(qkv, cache_kt, cache_vt, qkv, qkv, bias_cache, bias_new)


def _sb_sample_kernel(q_ref, kc_ref, vc_ref, kn_ref, vn_ref, o_ref, qs_ref, u_ref, rest_ref, acc_ref):
    t = q_ref.shape[0]
    past = kc_ref.shape[1]
    _store_head_masked_q(q_ref, qs_ref, t)
    u = _strict_upper_ones(SB_SUB)
    u_ref[...] = u
    qs = qs_ref[...]

    z = _dot_nt(qs, kn_ref[...])
    row = lax.broadcasted_iota(jnp.int32, z.shape, 0) % t
    col = lax.broadcasted_iota(jnp.int32, z.shape, 1)
    a, rest = _sb_block(z, u[:t, :t], jnp.zeros((z.shape[0], 1), F32), col < row)
    acc_ref[...] = _dot(a, vn_ref[...])
    rest_ref[...] = rest

    def more(carry):
        return jnp.logical_and(carry[0] >= 0, carry[1] > SB_DEAD)

    def older(carry):
        start = pl.multiple_of(carry[0] * SB_SUB, SB_SUB)
        k = kc_ref[:, pl.ds(start, SB_SUB)].astype(BF16)
        v = vc_ref[:, pl.ds(start, SB_SUB)].astype(BF16)
        a, rest = _sb_block(_dot(qs_ref[...], k), u_ref[...], rest_ref[...], None)
        acc_ref[...] += _dot_nt(a, v)
        rest_ref[...] = rest
        return carry[0] - 1, jnp.max(rest)

    lax.while_loop(more, older, (past // SB_SUB - 1, jnp.max(rest)))
    o_ref[...] = _unstack_heads(acc_ref[...], t).astype(BF16)


def _sb_sample(qkv, cache_kt, cache_vt, layer, dec_batch, t):
    past = cache_kt.shape[3]
    return pl.pallas_call(
        _sb_sample_kernel,
        out_shape=jax.ShapeDtypeStruct((dec_batch * t, HEADS_WIDTH), BF16),
        grid=(dec_batch,),
        in_specs=[pl.BlockSpec((t, HEADS_WIDTH), lambda b: (b, 6)),
                  pl.BlockSpec((None, None, HEADS_WIDTH, past), lambda b: (layer, b, 0, 0)),
                  pl.BlockSpec((None, None, HEADS_WIDTH, past), lambda b: (layer, b, 0, 0)),
                  pl.BlockSpec((t, HEADS_WIDTH), lambda b: (b, 7)),
                  pl.BlockSpec((t, HEADS_WIDTH), lambda b: (b, 8))],
        out_specs=pl.BlockSpec((t, HEADS_WIDTH), lambda b: (b, 0)),
        scratch_shapes=[pltpu.VMEM((N_HEADS * t, HEADS_WIDTH), BF16),
                        pltpu.VMEM((SB_SUB, SB_SUB), BF16),
                        pltpu.VMEM((N_HEADS * t, 1), F32),
                        pltpu.VMEM((N_HEADS * t, HEADS_WIDTH), F32)],
        compiler_params=_params("parallel"),
        name="sb_sample",
    )(qkv, cache_kt, cache_vt, qkv, qkv)


def _merge_kernel(x_ref, oa_ref, ob_ref, oc_ref, od_ref, wg_ref, bg_ref, wb_ref, wo_ref,
                  g_ref, b_ref, wr_ref, y_ref, logit_ref, *, alpha):
    x = x_ref[...]
    xb = x.astype(BF16)
    merged = None
    for n, o_ref in enumerate((oa_ref, ob_ref, oc_ref, od_ref)):
        gate = jax.nn.sigmoid(_dot(xb, wg_ref[n]) + bg_ref[n])
        term = gate * _dot(o_ref[...], wb_ref[n])
        merged = term if merged is None else merged + term
    h = alpha * x + _dot(merged.astype(BF16), wo_ref[...])
    y = _layer_norm(h, g_ref[...], b_ref[...])
    y_ref[...] = y
    logit_ref[...] = _dot(y.astype(BF16), wr_ref[...])


def _merge(x, branches, wg, bg, wb, wo, g, b, wr, alpha):
    n = x.shape[0]
    tm = TOKEN_TILE
    row = lambda width: pl.BlockSpec((tm, width), lambda i: (i, 0))
    return pl.pallas_call(
        functools.partial(_merge_kernel, alpha=alpha),
        out_shape=[jax.ShapeDtypeStruct((n, D_MODEL), F32), jax.ShapeDtypeStruct((n, LANES), F32)],
        grid=(n // tm,),
        in_specs=[row(D_MODEL)] + [row(HEADS_WIDTH)] * 4
                 + [_const_spec(a.shape) for a in (wg, bg, wb, wo, g, b, wr)],
        out_specs=[row(D_MODEL), row(LANES)],
        compiler_params=_params("parallel"),
        name="merge",
    )(x, *branches, wg, bg, wb, wo, g, b, wr)


def _route_kernel(logit_ref, bias_ref, onehot_ref, w_ref):
    scores = jax.nn.sigmoid(logit_ref[...])
    sel = scores + bias_ref[...]
    rows = [sel[e:e + 1, :] for e in range(N_EXPERTS)]
    best = None
    best_group = None
    for g in range(N_GROUPS):
        m = rows[g * EXPERTS_PER_GROUP:(g + 1) * EXPERTS_PER_GROUP]
        top2 = None
        for i, j in GROUP_PAIRS:
            pair = m[i] + m[j]
            top2 = pair if top2 is None else jnp.maximum(top2, pair)
        if best is None:
            best, best_group = top2, jnp.zeros(top2.shape, jnp.int32)
        else:
            better = top2 > best
            best = jnp.where(better, top2, best)
            best_group = jnp.where(better, g, best_group)
    chosen = []
    weights = []
    total = None
    for e in range(N_EXPERTS):
        g, i = divmod(e, EXPERTS_PER_GROUP)
        rank = jnp.zeros(best.shape, jnp.int32)
        for j in range(EXPERTS_PER_GROUP):
            if j == i:
                continue
            other = rows[g * EXPERTS_PER_GROUP + j]
            ahead = (other >= rows[e]) if j < i else (other > rows[e])
            rank = rank + ahead.astype(jnp.int32)
        chosen.append((best_group == g) & (rank < 2))
        w = jnp.where(chosen[e], scores[e:e + 1, :], 0.0)
        weights.append(w)
        total = w if total is None else total + w
    onehot = []
    w_first = jnp.zeros(best.shape, F32)
    w_second = jnp.zeros(best.shape, F32)
    for g in range(N_GROUPS):
        for i, j in GROUP_PAIRS:
            first, second = g * EXPERTS_PER_GROUP + i, g * EXPERTS_PER_GROUP + j
            both = chosen[first] & chosen[second]
            onehot.append(jnp.where(both, 1.0, 0.0))
            w_first = jnp.where(both, weights[first] / total, w_first)
            w_second = jnp.where(both, weights[second] / total, w_second)
    pad = jnp.zeros((onehot_ref.shape[0] - N_BUCKETS, best.shape[1]), F32)
    onehot_ref[...] = jnp.concatenate(onehot + [pad], axis=0)
    w_ref[...] = jnp.concatenate([w_first, w_second], axis=0)


def _route(logits_t, bias_col):
    n = logits_t.shape[1]
    return pl.pallas_call(
        _route_kernel,
        out_shape=[jax.ShapeDtypeStruct((BUCKET_ROWS, n), F32), jax.ShapeDtypeStruct((2, n), F32)],
        compiler_params=pltpu.CompilerParams(vmem_limit_bytes=VMEM_LIMIT),
        name="route",
    )(logits_t, bias_col)


def _row_gather_kernel(idx_ref, table_ref, out_ref, sem, *, rows):
    base = pl.program_id(0) * rows

    def row_copy(r):
        return pltpu.make_async_copy(table_ref.at[pl.ds(idx_ref[base + r], 1)],
                                     out_ref.at[pl.ds(base + r, 1)], sem)

    def start(r, carry):
        row_copy(r).start()
        return carry

    def wait(r, carry):
        row_copy(r).wait()
        return carry

    lax.fori_loop(0, rows, start, 0)
    lax.fori_loop(0, rows, wait, 0)


def _row_gather(table, idx):
    n_out = idx.shape[0]
    rows = min(GATHER_ROWS, n_out)
    assert n_out % rows == 0
    return pl.pallas_call(
        functools.partial(_row_gather_kernel, rows=rows),
        out_shape=jax.ShapeDtypeStruct((n_out, table.shape[1]), table.dtype),
        grid_spec=pltpu.PrefetchScalarGridSpec(
            num_scalar_prefetch=1,
            grid=(n_out // rows,),
            in_specs=[pl.BlockSpec(memory_space=pl.ANY)],
            out_specs=pl.BlockSpec(memory_space=pl.ANY),
            scratch_shapes=[pltpu.SemaphoreType.DMA]),
        compiler_params=pltpu.CompilerParams(dimension_semantics=("arbitrary",)),
        name="row_gather",
    )(idx, table)


def _experts_kernel(ea_ref, eb_ref, used_ref, x_ref, w_ref, wga_ref, wua_ref, wda_ref,
                    wgb_ref, wub_ref, wdb_ref, g_ref, b_ref, y_ref, *, alpha):
    del ea_ref, eb_ref
    t = pl.program_id(0)

    @pl.when(t < used_ref[0])
    def _():
        x = x_ref[...]
        xb = x.astype(BF16)

        def expert(wg_ref, wu_ref, wd_ref):
            hidden = jax.nn.silu(_dot(xb, wg_ref[...])) * _dot(xb, wu_ref[...])
            return _dot(hidden.astype(BF16), wd_ref[...])

        w = w_ref[...]
        y = w[:, 0:1] * expert(wga_ref, wua_ref, wda_ref) + w[:, 1:2] * expert(wgb_ref, wub_ref, wdb_ref)
        y_ref[...] = _layer_norm(alpha * x + y, g_ref[...], b_ref[...])

    @pl.when(t >= used_ref[0])
    def _():
        y_ref[...] = jnp.zeros(y_ref.shape, F32)


def _experts(x_sorted, w_sorted, first_expert, second_expert, tiles_used, wg, wu, wd, g, b, alpha, tile):
    n_slots = x_sorted.shape[0]
    row = lambda width: pl.BlockSpec((tile, width), lambda t, ea, eb, nu: (t, 0))
    up = lambda which: pl.BlockSpec((None, D_MODEL, D_EXPERT),
                                    lambda t, ea, eb, nu: ((ea, eb)[which][t], 0, 0))
    down = lambda which: pl.BlockSpec((None, D_EXPERT, D_MODEL),
                                      lambda t, ea, eb, nu: ((ea, eb)[which][t], 0, 0))
    const = lambda a: pl.BlockSpec(a.shape, lambda t, ea, eb, nu: (0,) * a.ndim)
    return pl.pallas_call(
        functools.partial(_experts_kernel, alpha=alpha),
        out_shape=jax.ShapeDtypeStruct((n_slots, D_MODEL), F32),
        grid_spec=pltpu.PrefetchScalarGridSpec(
            num_scalar_prefetch=3,
            grid=(n_slots // tile,),
            in_specs=[row(D_MODEL), row(2), up(0), up(0), down(0), up(1), up(1), down(1), const(g), const(b)],
            out_specs=row(D_MODEL)),
        compiler_params=_params("arbitrary"),
        name="experts",
    )(first_expert, second_expert, tiles_used, x_sorted, w_sorted, wg, wu, wd, wg, wu, wd, g, b)


def _moe(x, logits, rb, wg, wu, wd, g, b, alpha, tile):
    n = x.shape[0]
    onehot, w2 = _route(logits.T, rb)
    prefix = _cumsum_rows(onehot)
    counts = prefix[:, -1].astype(jnp.int32)
    padded = (counts + tile - 1) // tile * tile
    ends = jnp.cumsum(padded)
    starts = ends - padded
    rank = jnp.sum(onehot * prefix, axis=0).astype(jnp.int32) - 1
    pos = jnp.sum(onehot * starts.astype(F32)[:, None], axis=0).astype(jnp.int32) + rank
    n_slots = n + N_BUCKETS * tile
    slot_token = jnp.zeros((n_slots,), jnp.int32).at[pos].set(jnp.arange(n, dtype=jnp.int32),
                                                              unique_indices=True)
    w_sorted = jnp.zeros((n_slots, 2), F32).at[pos].set(w2.T, unique_indices=True)
    tile_start = jnp.arange(n_slots // tile, dtype=jnp.int32) * tile
    bucket = jnp.sum(tile_start[:, None] >= ends[None, :N_BUCKETS], axis=1)
    bucket = jnp.minimum(bucket, N_BUCKETS - 1).astype(jnp.int32)
    group, pair = bucket // len(GROUP_PAIRS), bucket % len(GROUP_PAIRS)
    pair_first = jnp.asarray([p[0] for p in GROUP_PAIRS], jnp.int32)
    pair_second = jnp.asarray([p[1] for p in GROUP_PAIRS], jnp.int32)
    first_expert = group * EXPERTS_PER_GROUP + pair_first[pair]
    second_expert = group * EXPERTS_PER_GROUP + pair_second[pair]
    tiles_used = (ends[N_BUCKETS - 1] // tile).astype(jnp.int32)[None]

    x_sorted = _row_gather(x, slot_token)
    y_sorted = _experts(x_sorted, w_sorted, first_expert, second_expert, tiles_used,
                        wg, wu, wd, g, b, alpha, tile)
    return _row_gather(y_sorted, pos)


IN_SIZES = (HEADS_WIDTH,) * 3 + (N_HEADS, MLA_Q_RANK, MLA_KV_RANK, MLA_ROPE) + (HEADS_WIDTH,) * 6


def _pad_cols(a, width):
    return jnp.pad(a, ((0, 0), (0, width - a.shape[1])))


def _prep_in_proj_weights(w_in, b_forget, q_norm, w_q_up, w_kv_up):
    offs = np.cumsum(IN_SIZES)[:-1].tolist()
    qa, ka, va, fa, cq, ckv, kr, qc, kc, vc, qd, kd, vd = jnp.split(w_in, offs, axis=1)
    w = jnp.concatenate([qa, ka, va, qc, kc, vc, qd, kd, vd, _pad_cols(cq, 256), ckv,
                         _pad_cols(kr, LANES), _pad_cols(fa, LANES)], axis=1).astype(BF16)
    bf = _pad_cols(b_forget[None, :], LANES)
    qn = _pad_cols(q_norm[None, :], 256)
    wq3 = w_q_up.reshape(MLA_Q_RANK, N_HEADS, MLA_NOPE + MLA_ROPE)
    wq = jnp.concatenate([wq3[:, :, :MLA_NOPE].reshape(MLA_Q_RANK, -1),
                          wq3[:, :, MLA_NOPE:].reshape(MLA_Q_RANK, -1)], axis=1)
    wq = jnp.pad(wq, ((0, 256 - MLA_Q_RANK), (0, 0))).astype(BF16)
    wkv = w_kv_up.reshape(MLA_KV_RANK, N_HEADS, MLA_NOPE + MLA_V)
    wcat = jnp.zeros((HEADS_WIDTH + N_HEADS * MLA_ROPE, N_HEADS * Q_CAT), F32)
    wv_pad = jnp.zeros((N_HEADS, Q_CAT, HEADS_WIDTH), F32)
    eye = jnp.eye(MLA_ROPE, dtype=F32)
    for h in range(N_HEADS):
        wcat = wcat.at[h * MLA_NOPE:(h + 1) * MLA_NOPE, h * Q_CAT:h * Q_CAT + MLA_KV_RANK].set(
            wkv[:, h, :MLA_NOPE].T)
        wcat = wcat.at[HEADS_WIDTH + h * MLA_ROPE:HEADS_WIDTH + (h + 1) * MLA_ROPE,
                       h * Q_CAT + MLA_KV_RANK:h * Q_CAT + MLA_KV_RANK + MLA_ROPE].set(eye)
        wv_pad = wv_pad.at[h, :MLA_KV_RANK, h * MLA_V:(h + 1) * MLA_V].set(wkv[:, h, MLA_NOPE:])
    return w, bf, qn, wq, wcat.astype(BF16), wv_pad.astype(BF16)


def _rope_tables(pos):
    half = MLA_ROPE // 2
    inv = ROPE_THETA ** (-jnp.arange(half, dtype=F32) / half)
    ang = pos.astype(F32)[:, None] * inv[None, :]
    cos = jnp.cos(ang)
    sin = jnp.sin(ang)
    reps = LANES // MLA_ROPE
    return (jnp.tile(jnp.concatenate([cos, cos], axis=1), (1, reps)),
            jnp.tile(jnp.concatenate([-sin, sin], axis=1), (1, reps)))


def _band_bias(rel_bias, q0, nq, k0, nk):
    period = nq + nk
    j = np.arange(period)
    d = np.where(j < nk, j, j - period)
    idx = np.clip(q0 - k0 - d, -REL_CLIP, REL_CLIP) + REL_CLIP
    u = rel_bias[:, idx]
    rows = jnp.tile(u, (1, nq))[:, :nq * (period - 1)].reshape(rel_bias.shape[0], nq, period - 1)
    return rows[:, :, :nk]


def _key_major(cache):
    l, b, p, h, d = cache.shape
    return jnp.transpose(cache, (0, 1, 3, 4, 2)).reshape(l, b, h * d, p)


def _from_key_major(state, n_heads):
    b, hd, t = state.shape
    return jnp.transpose(state.reshape(b, n_heads, hd // n_heads, t), (0, 3, 1, 2))


def kernel(x_prompt, x_sample, cache_fox_k, cache_fox_v, cache_fox_logf, cache_mla_latent, cache_mla_krope, cache_band_k, cache_band_v, cache_sb_k, cache_sb_v, w_in, b_forget, mla_q_norm, w_q_up, mla_kv_norm, w_kv_up, rel_bias, w_branch, w_gate, b_gate, w_out, ln1_g, ln1_b, w_router, router_bias, w_e_gate, w_e_up, w_e_down, ln2_g, ln2_b):
    batch, seq, _ = x_prompt.shape
    dec_batch, t_new, _ = x_sample.shape
    depth = w_in.shape[0]
    past = cache_fox_k.shape[2]
    n_band = cache_band_k.shape[2]
    n_p = batch * seq
    n_s = dec_batch * t_new
    alpha = (2 * depth) ** 0.25
    band_tile = BAND_CHUNKS * CHUNK
    assert seq % ATT_TILE == 0 and seq % band_tile == 0 and seq % TOKEN_TILE == 0
    assert n_p % MOE_TILE == 0 and n_s % MOE_TILE_SAMPLE == 0 and n_s % TOKEN_TILE == 0
    assert past % LANES == 0 and t_new <= LANES and t_new % 16 == 0 and ATT_TILE % SB_SUB == 0
    assert past % min(SAMPLE_KEY_TILE, past) == 0 and past % SB_SUB == 0

    xp = x_prompt.reshape(n_p, D_MODEL)
    xs = x_sample.reshape(n_s, D_MODEL)
    rope_p = _rope_tables(jnp.tile(jnp.arange(seq, dtype=jnp.int32), batch))
    rope_s = _rope_tables(jnp.tile(past + jnp.arange(t_new, dtype=jnp.int32), dec_batch))

    c_fox_k, c_fox_v = _key_major(cache_fox_k), _key_major(cache_fox_v)
    c_band_k, c_band_v = _key_major(cache_band_k), _key_major(cache_band_v)
    c_sb_k, c_sb_v = _key_major(cache_sb_k), _key_major(cache_sb_v)
    c_logf = jnp.swapaxes(cache_fox_logf, 2, 3).astype(F32)
    c_krope = jnp.swapaxes(cache_mla_krope, 2, 3)

    wr = _pad_cols(w_router, LANES).astype(BF16)
    rb = router_bias.astype(F32)[:, None]
    total = past + t_new
    padded = -(-total // LANES) * LANES

    prompt_states, sample_states = [], []
    for l in range(depth):
        w, bf, qn, wq, wcat, wv_pad = _prep_in_proj_weights(w_in[l], b_forget[l], mla_q_norm[l],
                                                             w_q_up[l], w_kv_up[l])
        proj_w = (w, bf, qn, wq, mla_kv_norm[l][None, :], wcat)
        (qkv_p, ka_p, va_p, kc_p, vc_p, kd_p, vd_p, logf_p, lat_p, kr_p, kcat_p, qcat_p,
         vat_p, latt_p) = _in_proj(xp, *proj_w, *rope_p, batch=batch, seq=seq)
        (qkv_s, ka_s, va_s, kc_s, vc_s, kd_s, vd_s, logf_s, lat_s, kr_s, kcat_s, qcat_s) = _in_proj(
            xs, *proj_w, *rope_s)
        logf_s = logf_s[:, :N_HEADS].reshape(dec_batch, t_new, N_HEADS)
        kr_s = kr_s[:, :MLA_ROPE]

        c_p = _cumsum_rows(logf_p.reshape(batch * N_HEADS, seq)).reshape(batch, N_HEADS, seq)
        all_s = jnp.concatenate([c_logf[l], jnp.swapaxes(logf_s, 1, 2)], axis=2)
        all_s = jnp.pad(all_s, ((0, 0), (0, 0), (0, padded - total)))
        c_s = _cumsum_rows(all_s.reshape(dec_batch * N_HEADS, padded)).reshape(dec_batch, N_HEADS, padded)
        cq_s = c_s[:, :, past:past + t_new].reshape(dec_batch, N_HEADS * t_new, 1)

        bias_p = _band_bias(rel_bias[l], 0, band_tile, -band_tile, 2 * band_tile)
        bias_s = _band_bias(rel_bias[l], past, t_new, past - n_band, n_band + t_new)
        bias_s = bias_s.reshape(N_HEADS * t_new, n_band + t_new)

        branches_p = (_fox_prompt(qkv_p, vat_p, jnp.swapaxes(c_p, 1, 2), c_p, batch, seq),
                      _mla_prompt(qcat_p, kcat_p, latt_p, wv_pad, batch, seq),
                      _band_prompt(qkv_p, bias_p, batch, seq),
                      _sb_prompt(qkv_p, batch, seq))
        branches_s = (_fox_sample(qkv_s, c_fox_k, c_fox_v, cq_s, c_s, l, dec_batch, t_new),
                      _mla_sample(qcat_s, kcat_s, cache_mla_latent, c_krope, wv_pad, l, dec_batch, t_new),
                      _band_sample(qkv_s, c_band_k, c_band_v, bias_s[:, :n_band], bias_s[:, n_band:], l,
                                   past, dec_batch, t_new),
                      _sb_sample(qkv_s, c_sb_k, c_sb_v, l, dec_batch, t_new))

        tail_w = (w_gate[l].astype(BF16), b_gate[l][:, None, :], w_branch[l].astype(BF16),
                  w_out[l].astype(BF16), ln1_g[l][None, :], ln1_b[l][None, :], wr, alpha)
        experts = (w_e_gate[l].astype(BF16), w_e_up[l].astype(BF16), w_e_down[l].astype(BF16),
                   ln2_g[l][None, :], ln2_b[l][None, :], alpha)
        x1_p, logits_p = _merge(xp, branches_p, *tail_w)
        x1_s, logits_s = _merge(xs, branches_s, *tail_w)
        xp = _moe(x1_p, logits_p[:, :N_EXPERTS], rb, *experts, MOE_TILE)
        xs = _moe(x1_s, logits_s[:, :N_EXPERTS], rb, *experts, MOE_TILE_SAMPLE)

        n_keep = min(band_tile, seq)
        prompt_states.append((
            _from_key_major(ka_p, N_HEADS), _from_key_major(va_p, N_HEADS), jnp.swapaxes(logf_p, 1, 2),
            lat_p.reshape(batch, seq, -1), jnp.swapaxes(kr_p, 1, 2),
            _from_key_major(kc_p[:, :, seq - n_keep:], N_HEADS),
            _from_key_major(vc_p[:, :, seq - n_keep:], N_HEADS),
            _from_key_major(kd_p, N_HEADS), _from_key_major(vd_p, N_HEADS)))
        heads = lambda a: a.reshape(dec_batch, t_new, N_HEADS, HEAD_DIM)
        sample_states.append((
            heads(ka_s), heads(va_s), logf_s, lat_s.reshape(dec_batch, t_new, -1),
            kr_s.reshape(dec_batch, t_new, -1), heads(kc_s), heads(vc_s), heads(kd_s), heads(vd_s)))

    stack = lambda states: [jnp.stack(z, axis=0) for z in zip(*states)]
    return (xp.reshape(batch, seq, D_MODEL), xs.reshape(dec_batch, t_new, D_MODEL),
            *stack(prompt_states), *stack(sample_states))
```

```python
import functools

import numpy as np
import jax
import jax.numpy as jnp
from jax import lax
from jax.experimental import pallas as pl
from jax.experimental.pallas import tpu as pltpu

F32 = jnp.float32
BF16 = jnp.bfloat16

D_MODEL = 1024
N_HEADS = 4
HEAD_DIM = 64
HEADS_WIDTH = N_HEADS * HEAD_DIM
CHUNK = 64
BAND_CHUNKS = 8
REL_CLIP = 128
MLA_NOPE = 64
MLA_ROPE = 32
MLA_V = 64
MLA_Q_RANK = 192
MLA_KV_RANK = 128
ROPE_THETA = 10000.0
N_BRANCH = 4
N_EXPERTS = 16
N_GROUPS = 4
EXPERTS_PER_GROUP = N_EXPERTS // N_GROUPS
D_EXPERT = 512
GROUP_PAIRS = tuple((i, j) for i in range(EXPERTS_PER_GROUP) for j in range(i + 1, EXPERTS_PER_GROUP))
NEG_INF = -1e30
LANES = 128
VMEM_LIMIT = 56 * 1024 * 1024

QKV_WIDTH = 9 * HEADS_WIDTH
STATE_COLS = (1, 2, 4, 5, 7, 8)
VALUE_COLS = (2, 5, 8)
OFF_CQ = QKV_WIDTH
OFF_CKV = OFF_CQ + 256
OFF_KR = OFF_CKV + MLA_KV_RANK
OFF_FA = OFF_KR + LANES
PROJ_WIDTH = OFF_FA + LANES
Q_CAT = 256

TOKEN_TILE = 512
MOE_TILE = 1024
ATT_TILE = 512
SB_SUB = 256
SAMPLE_KEY_TILE = 4096
SB_DEAD = -105.0


def _dot(a, b):
    return jnp.dot(a, b, preferred_element_type=F32)


def _dot_nt(a, b):
    return lax.dot_general(a, b, (((1,), (1,)), ((), ())), preferred_element_type=F32)


def _const_spec(shape):
    nd = len(shape)
    return pl.BlockSpec(shape, lambda *_: (0,) * nd, pipeline_mode=pl.Buffered(1))


def _params(*sem):
    return pltpu.CompilerParams(dimension_semantics=sem, vmem_limit_bytes=VMEM_LIMIT)


def _log_sigmoid(x):
    return jnp.minimum(x, 0.0) - jnp.log1p(jnp.exp(-jnp.abs(x)))


def _softplus(x):
    return jnp.maximum(x, 0.0) + jnp.log1p(jnp.exp(-jnp.abs(x)))


def _layer_norm(h, g, b):
    mu = jnp.mean(h, axis=-1, keepdims=True)
    d = h - mu
    var = jnp.mean(d * d, axis=-1, keepdims=True)
    return d * lax.rsqrt(var + 1e-5) * g + b


def _head_of_lane(shape):
    return lax.broadcasted_iota(jnp.int32, shape, len(shape) - 1) // HEAD_DIM


def _in_proj_kernel(x_ref, w_ref, bf_ref, qn_ref, wq_ref, kvn_ref, wcat_ref, cos_ref, sin_ref,
                    qkv_ref, ka_ref, va_ref, kc_ref, vc_ref, kd_ref, vd_ref,
                    logf_ref, lat_ref, kr_ref, kcat_ref, qcat_ref, *key_major_refs, key_major):
    xb = x_ref[...].astype(BF16)
    proj = _dot(xb, w_ref[...])
    qkv_ref[...] = proj[:, :QKV_WIDTH].astype(BF16)
    for ref, col in zip((ka_ref, va_ref, kc_ref, vc_ref, kd_ref, vd_ref), STATE_COLS):
        state = proj[:, col * HEADS_WIDTH:(col + 1) * HEADS_WIDTH]
        if key_major:
            state = state.T
            if col in VALUE_COLS:
                key_major_refs[VALUE_COLS.index(col)][...] = state.astype(BF16)
        ref[...] = state

    cos = cos_ref[...]
    sin = sin_ref[...]
    lane = lax.broadcasted_iota(jnp.int32, cos.shape, 1)
    first_half = (lane % MLA_ROPE) < (MLA_ROPE // 2)

    def rope(v):
        partner = jnp.where(first_half, pltpu.roll(v, LANES - MLA_ROPE // 2, 1),
                            pltpu.roll(v, MLA_ROPE // 2, 1))
        return v * cos + partner * sin

    cq = proj[:, OFF_CQ:OFF_CQ + 256]
    ms = jnp.sum(cq * cq, axis=-1, keepdims=True) * (1.0 / MLA_Q_RANK)
    cqn = cq * lax.rsqrt(ms + 1e-6) * qn_ref[...]
    qb = _dot(cqn.astype(BF16), wq_ref[...])
    q_rope = rope(qb[:, HEADS_WIDTH:])
    q_in = jnp.concatenate([qb[:, :HEADS_WIDTH], q_rope], axis=1) * ((MLA_NOPE + MLA_ROPE) ** -0.5)
    qcat_ref[...] = _dot(q_in.astype(BF16), wcat_ref[...]).astype(BF16)

    ckv = proj[:, OFF_CKV:OFF_CKV + MLA_KV_RANK]
    lat = ckv * lax.rsqrt(jnp.mean(ckv * ckv, axis=-1, keepdims=True) + 1e-6) * kvn_ref[...]
    lat_ref[...] = lat
    kr = rope(proj[:, OFF_KR:OFF_KR + LANES])
    kcat_ref[...] = jnp.concatenate([lat, kr], axis=1).astype(BF16)
    logf = _log_sigmoid(proj[:, OFF_FA:OFF_FA + LANES] + bf_ref[...])
    if key_major:
        kr_ref[...] = kr.T[:MLA_ROPE, :]
        logf_ref[...] = logf.T[:N_HEADS, :]
        key_major_refs[len(VALUE_COLS)][...] = lat.T.astype(BF16)
    else:
        kr_ref[...] = kr
        logf_ref[...] = logf


def _in_proj(x, w, bf, qn, wq, kvn, wcat, cos, sin, *, batch=None, seq=None):
    n = x.shape[0]
    tm = TOKEN_TILE
    key_major = batch is not None
    row = lambda width: pl.BlockSpec((tm, width), lambda i: (i, 0))
    if key_major:
        per = seq // tm
        tr = lambda height: pl.BlockSpec((None, height, tm), lambda i: (i // per, 0, i % per))
        tshape = lambda height: jax.ShapeDtypeStruct((batch, height, seq), F32)
        state_shapes = [tshape(HEADS_WIDTH)] * 6 + [tshape(N_HEADS), jax.ShapeDtypeStruct((n, LANES), F32),
                                                    tshape(MLA_ROPE)]
        state_specs = [tr(HEADS_WIDTH)] * 6 + [tr(N_HEADS), row(LANES), tr(MLA_ROPE)]
    else:
        state_shapes = ([jax.ShapeDtypeStruct((n, HEADS_WIDTH), F32)] * 6
                        + [jax.ShapeDtypeStruct((n, LANES), F32)] * 3)
        state_specs = [row(HEADS_WIDTH)] * 6 + [row(LANES)] * 3
    out_shape = ([jax.ShapeDtypeStruct((n, QKV_WIDTH), BF16)] + state_shapes
                 + [jax.ShapeDtypeStruct((n, 2 * LANES), BF16),
                    jax.ShapeDtypeStruct((n, N_HEADS * Q_CAT), BF16)])
    out_specs = [row(QKV_WIDTH)] + state_specs + [row(2 * LANES), row(N_HEADS * Q_CAT)]
    if key_major:
        out_shape += ([jax.ShapeDtypeStruct((batch, HEADS_WIDTH, seq), BF16)] * len(VALUE_COLS)
                      + [jax.ShapeDtypeStruct((batch, MLA_KV_RANK, seq), BF16)])
        out_specs += [tr(HEADS_WIDTH)] * len(VALUE_COLS) + [tr(MLA_KV_RANK)]
    return pl.pallas_call(
        functools.partial(_in_proj_kernel, key_major=key_major),
        out_shape=out_shape,
        grid=(n // tm,),
        in_specs=[row(D_MODEL), _const_spec(w.shape), _const_spec(bf.shape), _const_spec(qn.shape),
                  _const_spec(wq.shape), _const_spec(kvn.shape), _const_spec(wcat.shape),
                  row(LANES), row(LANES)],
        out_specs=out_specs,
        compiler_params=_params("parallel"),
        name="in_proj_prompt" if key_major else "in_proj_sample",
    )(x, w, bf, qn, wq, kvn, wcat, cos, sin)


def _cumsum_kernel(x_ref, o_ref):
    r, t = x_ref.shape
    row = lax.broadcasted_iota(jnp.int32, (LANES, LANES), 0)
    col = lax.broadcasted_iota(jnp.int32, (LANES, LANES), 1)
    tri = jnp.where(row <= col, 1.0, 0.0).astype(BF16)
    carry = jnp.zeros((r, 1), F32)
    for c in range(t // LANES):
        xc = x_ref[:, c * LANES:(c + 1) * LANES]
        hi = xc.astype(BF16)
        r1 = xc - hi.astype(F32)
        mid = r1.astype(BF16)
        lo = (r1 - mid.astype(F32)).astype(BF16)
        y = _dot(hi, tri) + _dot(mid, tri) + _dot(lo, tri) + carry
        o_ref[:, c * LANES:(c + 1) * LANES] = y
        carry = y[:, LANES - 1:LANES]


def _cumsum_rows(x):
    return pl.pallas_call(
        _cumsum_kernel,
        out_shape=jax.ShapeDtypeStruct(x.shape, F32),
        name="cumsum",
    )(x)


def _online_softmax_update(s, pv, m_ref, l_ref, acc_ref, h):
    m_prev = m_ref[h]
    m_new = jnp.maximum(m_prev, jnp.max(s, axis=1, keepdims=True))
    alpha = jnp.exp(m_prev - m_new)
    p = jnp.exp(s - m_new)
    l_ref[h] = alpha * l_ref[h] + jnp.sum(p, axis=1, keepdims=True)
    acc_ref[h] = alpha * acc_ref[h] + pv(p.astype(BF16))
    m_ref[h] = m_new


def _init_softmax_state(m_ref, l_ref, acc_ref):
    m_ref[...] = jnp.full(m_ref.shape, NEG_INF, F32)
    l_ref[...] = jnp.zeros(l_ref.shape, F32)
    acc_ref[...] = jnp.zeros(acc_ref.shape, F32)


def _store_head_masked_q(q_ref, qs_ref, rows):
    q = q_ref[...] * (HEAD_DIM ** -0.5)
    head = _head_of_lane(q.shape)
    for h in range(N_HEADS):
        qs_ref[h * rows:(h + 1) * rows, :] = jnp.where(head == h, q, jnp.zeros_like(q))


def _merge_heads(per_head):
    head = _head_of_lane(per_head[0].shape)
    out = jnp.zeros_like(per_head[0])
    for h in range(N_HEADS):
        out = jnp.where(head == h, per_head[h], out)
    return out


def _online_softmax_update_t(st, pv, m_ref, l_ref, acc_ref, h):
    m_prev = m_ref[h]
    m_new = jnp.maximum(m_prev, jnp.max(st, axis=0, keepdims=True))
    alpha = jnp.exp(m_prev - m_new)
    p = jnp.exp(st - m_new)
    l_ref[h] = alpha * l_ref[h] + jnp.sum(p, axis=0, keepdims=True)
    acc_ref[h] = alpha * acc_ref[h] + pv(p.astype(BF16))
    m_ref[h] = m_new


def _fox_prompt_kernel(q_ref, k_ref, vt_ref, cq_ref, ck_ref, o_ref, qt_ref, m_ref, l_ref, acc_ref):
    qi = pl.program_id(1)
    ki = pl.program_id(2)

    @pl.when(ki == 0)
    def _():
        _init_softmax_state(m_ref, l_ref, acc_ref)
        _store_head_masked_qt(q_ref, qt_ref)

    def step(diagonal):
        k = k_ref[...]
        vt = vt_ref[...]
        cq = cq_ref[...]
        ck = ck_ref[...]
        for h in range(N_HEADS):
            st = _dot(k, qt_ref[h]) + cq[h:h + 1, :] - ck[:, h:h + 1]
            if diagonal:
                key = lax.broadcasted_iota(jnp.int32, st.shape, 0)
                qry = lax.broadcasted_iota(jnp.int32, st.shape, 1)
                st = jnp.where(key <= qry, st, NEG_INF)
            vt_h = vt[h * HEAD_DIM:(h + 1) * HEAD_DIM, :]
            _online_softmax_update_t(st, lambda p: _dot(vt_h, p), m_ref, l_ref, acc_ref, h)

    pl.when(ki < qi)(lambda: step(False))
    pl.when(ki == qi)(lambda: step(True))

    @pl.when(ki == pl.num_programs(2) - 1)
    def _():
        ot = jnp.concatenate([acc_ref[h] / l_ref[h] for h in range(N_HEADS)], axis=0)
        o_ref[...] = ot.T.astype(BF16)


def _fox_prompt(qkv, vt, c_tok, c_head, batch, seq):
    t = ATT_TILE
    nt = seq // t
    return pl.pallas_call(
        _fox_prompt_kernel,
        out_shape=jax.ShapeDtypeStruct((batch * seq, HEADS_WIDTH), BF16),
        grid=(batch, nt, nt),
        in_specs=[pl.BlockSpec((t, HEADS_WIDTH), lambda b, i, j: (b * nt + i, 0)),
                  pl.BlockSpec((t, HEADS_WIDTH), lambda b, i, j: (b * nt + jnp.minimum(i, j), 1)),
                  pl.BlockSpec((None, HEADS_WIDTH, t), lambda b, i, j: (b, 0, jnp.minimum(i, j))),
                  pl.BlockSpec((None, N_HEADS, t), lambda b, i, j: (b, 0, i)),
                  pl.BlockSpec((None, t, N_HEADS), lambda b, i, j: (b, jnp.minimum(i, j), 0))],
        out_specs=pl.BlockSpec((t, HEADS_WIDTH), lambda b, i, j: (b * nt + i, 0)),
        scratch_shapes=[pltpu.VMEM((N_HEADS, HEADS_WIDTH, t), BF16),
                        pltpu.VMEM((N_HEADS, 1, t), F32), pltpu.VMEM((N_HEADS, 1, t), F32),
                        pltpu.VMEM((N_HEADS, HEAD_DIM, t), F32)],
        compiler_params=_params("parallel", "parallel", "arbitrary"),
        name="fox_prompt",
    )(qkv, qkv, vt, c_head, c_tok)


def _mla_prompt_kernel(q_ref, k_ref, latt_ref, wv_ref, o_ref, qt_ref, m_ref, l_ref, acc_ref):
    qi = pl.program_id(1)
    ki = pl.program_id(2)

    @pl.when(ki == 0)
    def _():
        _init_softmax_state(m_ref, l_ref, acc_ref)
        for h in range(N_HEADS):
            qt_ref[h] = q_ref[:, h * Q_CAT:(h + 1) * Q_CAT].astype(F32).T.astype(BF16)

    def step(diagonal):
        k = k_ref[...]
        latt = latt_ref[...]
        for h in range(N_HEADS):
            st = _dot(k, qt_ref[h])
            if diagonal:
                key = lax.broadcasted_iota(jnp.int32, st.shape, 0)
                qry = lax.broadcasted_iota(jnp.int32, st.shape, 1)
                st = jnp.where(key // CHUNK <= qry // CHUNK, st, NEG_INF)
            _online_softmax_update_t(st, lambda p: _dot(latt, p), m_ref, l_ref, acc_ref, h)

    pl.when(ki < qi)(lambda: step(False))
    pl.when(ki == qi)(lambda: step(True))

    @pl.when(ki == pl.num_programs(2) - 1)
    def _():
        out = None
        for h in range(N_HEADS):
            o_h = (acc_ref[h] / l_ref[h]).T.astype(BF16)
            term = _dot(o_h, wv_ref[h, :MLA_KV_RANK, :])
            out = term if out is None else out + term
        o_ref[...] = out.astype(BF16)


def _mla_prompt(qcat, kcat, latt, wv_pad, batch, seq):
    t = ATT_TILE
    nt = seq // t
    return pl.pallas_call(
        _mla_prompt_kernel,
        out_shape=jax.ShapeDtypeStruct((batch * seq, HEADS_WIDTH), BF16),
        grid=(batch, nt, nt),
        in_specs=[pl.BlockSpec((t, N_HEADS * Q_CAT), lambda b, i, j: (b * nt + i, 0)),
                  pl.BlockSpec((t, Q_CAT), lambda b, i, j: (b * nt + jnp.minimum(i, j), 0)),
                  pl.BlockSpec((None, MLA_KV_RANK, t), lambda b, i, j: (b, 0, jnp.minimum(i, j))),
                  _const_spec(wv_pad.shape)],
        out_specs=pl.BlockSpec((t, HEADS_WIDTH), lambda b, i, j: (b * nt + i, 0)),
        scratch_shapes=[pltpu.VMEM((N_HEADS, Q_CAT, t), BF16),
                        pltpu.VMEM((N_HEADS, 1, t), F32), pltpu.VMEM((N_HEADS, 1, t), F32),
                        pltpu.VMEM((N_HEADS, MLA_KV_RANK, t), F32)],
        compiler_params=_params("parallel", "parallel", "arbitrary"),
        name="mla_prompt",
    )(qcat, kcat, latt, wv_pad)


def _strict_upper_ones(n):
    row = lax.broadcasted_iota(jnp.int32, (n, n), 0)
    col = lax.broadcasted_iota(jnp.int32, (n, n), 1)
    return jnp.where(row > col, 1.0, 0.0).astype(BF16)


def _sb_block(z, u, rest, mask):
    sp = _softplus(z)
    log_1m = -sp
    if mask is not None:
        log_1m = jnp.where(mask, log_1m, 0.0)
    hi = log_1m.astype(BF16)
    lo = (log_1m - hi.astype(F32)).astype(BF16)
    later = _dot(hi, u) + _dot(lo, u) + rest
    a = jnp.exp(z - sp + later)
    if mask is not None:
        a = jnp.where(mask, a, 0.0)
    return a.astype(BF16), rest + jnp.sum(log_1m, axis=1, keepdims=True)


def _store_head_masked_qt(q_ref, qt_ref):
    qt = q_ref[...].astype(F32).T * (HEAD_DIM ** -0.5)
    head = lax.broadcasted_iota(jnp.int32, qt.shape, 0) // HEAD_DIM
    for h in range(N_HEADS):
        qt_ref[h] = jnp.where(head == h, qt, 0.0).astype(BF16)


def _sb_block_t(zt, u, rest, mask):
    sp = _softplus(zt)
    log_1m = -sp
    if mask is not None:
        log_1m = jnp.where(mask, log_1m, 0.0)
    hi = log_1m.astype(BF16)
    lo = (log_1m - hi.astype(F32)).astype(BF16)
    later = _dot(u, hi) + _dot(u, lo) + rest
    a = jnp.exp(zt - sp + later)
    if mask is not None:
        a = jnp.where(mask, a, 0.0)
    return a.astype(BF16), rest + jnp.sum(log_1m, axis=0, keepdims=True)


def _sb_prompt_kernel(q_ref, k_ref, vt_ref, o_ref, qt_ref, u_ref, rest_ref, acc_ref):
    tq = q_ref.shape[0]
    qi = pl.program_id(1)
    _store_head_masked_qt(q_ref, qt_ref)
    key_s = lax.broadcasted_iota(jnp.int32, (SB_SUB, SB_SUB), 0)
    key_j = lax.broadcasted_iota(jnp.int32, (SB_SUB, SB_SUB), 1)
    u_ref[...] = jnp.where(key_j > key_s, 1.0, 0.0).astype(BF16)
    rest_ref[...] = jnp.zeros(rest_ref.shape, F32)
    acc_ref[...] = jnp.zeros(acc_ref.shape, F32)

    def visit(start, first_key):
        k = k_ref[pl.ds(start, SB_SUB), :]
        vt = vt_ref[:, pl.ds(start, SB_SUB)]
        u = u_ref[...]
        live = None
        for h in range(N_HEADS):
            zt = _dot(k, qt_ref[h])
            mask = None
            if first_key is not None:
                key = lax.broadcasted_iota(jnp.int32, zt.shape, 0) + first_key
                qry = lax.broadcasted_iota(jnp.int32, zt.shape, 1)
                mask = key < qry
            a, rest = _sb_block_t(zt, u, rest_ref[h], mask)
            acc_ref[h] += _dot(vt[h * HEAD_DIM:(h + 1) * HEAD_DIM, :], a)
            rest_ref[h] = rest
            top = jnp.max(rest)
            live = top if live is None else jnp.maximum(live, top)
        return live

    live = None
    for sub in reversed(range(tq // SB_SUB)):
        live = visit(pl.multiple_of(qi * tq + sub * SB_SUB, SB_SUB), sub * SB_SUB)

    def more(carry):
        return jnp.logical_and(carry[0] >= 0, carry[1] > SB_DEAD)

    def older(carry):
        return carry[0] - 1, visit(pl.multiple_of(carry[0] * SB_SUB, SB_SUB), None)

    lax.while_loop(more, older, (qi * (tq // SB_SUB) - 1, live))
    o_ref[...] = jnp.concatenate([acc_ref[h] for h in range(N_HEADS)], axis=0).T.astype(BF16)


def _sb_prompt(qkv, vt, batch, seq):
    t = ATT_TILE
    nt = seq // t
    return pl.pallas_call(
        _sb_prompt_kernel,
        out_shape=jax.ShapeDtypeStruct((batch * seq, HEADS_WIDTH), BF16),
        grid=(batch, nt),
        in_specs=[pl.BlockSpec((t, HEADS_WIDTH), lambda b, i: (b * nt + i, 6)),
                  pl.BlockSpec((seq, HEADS_WIDTH), lambda b, i: (b, 7)),
                  pl.BlockSpec((None, HEADS_WIDTH, seq), lambda b, i: (b, 0, 0))],
        out_specs=pl.BlockSpec((t, HEADS_WIDTH), lambda b, i: (b * nt + i, 0)),
        scratch_shapes=[pltpu.VMEM((N_HEADS, HEADS_WIDTH, t), BF16),
                        pltpu.VMEM((SB_SUB, SB_SUB), BF16),
                        pltpu.VMEM((N_HEADS, 1, t), F32),
                        pltpu.VMEM((N_HEADS, HEAD_DIM, t), F32)],
        compiler_params=_params("parallel", "arbitrary"),
        name="sb_prompt",
    )(qkv, qkv, vt)


def _band_prompt_kernel(q_ref, kp_ref, kc_ref, vtp_ref, vtc_ref, bias_ref, o_ref, qt_ref):
    tq = q_ref.shape[0]
    qi = pl.program_id(1)
    _store_head_masked_qt(q_ref, qt_ref)
    k = jnp.concatenate([kp_ref[...], kc_ref[...]], axis=0)
    vt = jnp.concatenate([vtp_ref[...], vtc_ref[...]], axis=1)
    key = lax.broadcasted_iota(jnp.int32, (2 * tq, tq), 0)
    qry_chunk = lax.broadcasted_iota(jnp.int32, (2 * tq, tq), 1) // CHUNK
    key_chunk = key // CHUNK - tq // CHUNK
    valid = (key_chunk <= qry_chunk) & (key_chunk >= qry_chunk - BAND_CHUNKS)
    valid = valid & ((key >= tq) | (qi > 0))
    outs = []
    for h in range(N_HEADS):
        st = _dot(k, qt_ref[h]) + bias_ref[h]
        st = jnp.where(valid, st, NEG_INF)
        p = jnp.exp(st - jnp.max(st, axis=0, keepdims=True))
        denom = jnp.sum(p, axis=0, keepdims=True)
        outs.append(_dot(vt[h * HEAD_DIM:(h + 1) * HEAD_DIM, :], p.astype(BF16)) / denom)
    o_ref[...] = jnp.concatenate(outs, axis=0).T.astype(BF16)


def _band_prompt(qkv, vt, bias_tile, batch, seq):
    t = BAND_CHUNKS * CHUNK
    nt = seq // t
    prev = lambda i: jnp.maximum(i - 1, 0)
    return pl.pallas_call(
        _band_prompt_kernel,
        out_shape=jax.ShapeDtypeStruct((batch * seq, HEADS_WIDTH), BF16),
        grid=(batch, nt),
        in_specs=[pl.BlockSpec((t, HEADS_WIDTH), lambda b, i: (b * nt + i, 3)),
                  pl.BlockSpec((t, HEADS_WIDTH), lambda b, i: (b * nt + prev(i), 4)),
                  pl.BlockSpec((t, HEADS_WIDTH), lambda b, i: (b * nt + i, 4)),
                  pl.BlockSpec((None, HEADS_WIDTH, t), lambda b, i: (b, 0, prev(i))),
                  pl.BlockSpec((None, HEADS_WIDTH, t), lambda b, i: (b, 0, i)),
                  _const_spec(bias_tile.shape)],
        out_specs=pl.BlockSpec((t, HEADS_WIDTH), lambda b, i: (b * nt + i, 0)),
        scratch_shapes=[pltpu.VMEM((N_HEADS, HEADS_WIDTH, t), BF16)],
        compiler_params=_params("parallel", "parallel"),
        name="band_prompt",
    )(qkv, qkv, qkv, vt, vt, bias_tile)


def _rows_per_head(x, rows):
    return jnp.concatenate([jnp.broadcast_to(x[h:h + 1, :], (rows, x.shape[1])) for h in range(N_HEADS)],
                           axis=0)


def _unstack_heads(acc, rows):
    return _merge_heads([acc[h * rows:(h + 1) * rows, :] for h in range(N_HEADS)])


def _fox_sample_kernel(q_ref, kc_ref, vc_ref, kn_ref, vn_ref, cq_ref, ckc_ref, ckn_ref, o_ref,
                       qs_ref, m_ref, l_ref, acc_ref):
    t = q_ref.shape[0]
    ki = pl.program_id(1)

    @pl.when(ki == 0)
    def _():
        _init_softmax_state(m_ref, l_ref, acc_ref)
        _store_head_masked_q(q_ref, qs_ref, t)

    qs = qs_ref[...]
    cq = cq_ref[...]
    s = _dot(qs, kc_ref[...].astype(BF16)) + cq - _rows_per_head(ckc_ref[...], t)
    vc = vc_ref[...].astype(BF16)
    _online_softmax_update(s, lambda p: _dot_nt(p, vc), m_ref, l_ref, acc_ref, 0)

    @pl.when(ki == pl.num_programs(1) - 1)
    def _():
        s = _dot_nt(qs, kn_ref[...]) + cq - _rows_per_head(ckn_ref[:, :t], t)
        row = lax.broadcasted_iota(jnp.int32, s.shape, 0) % t
        col = lax.broadcasted_iota(jnp.int32, s.shape, 1)
        s = jnp.where(col <= row, s, NEG_INF)
        _online_softmax_update(s, lambda p: _dot(p, vn_ref[...]), m_ref, l_ref, acc_ref, 0)
        o_ref[...] = _unstack_heads(acc_ref[0] / l_ref[0], t).astype(BF16)


def _fox_sample(qkv, cache_kt, cache_vt, cq_rows, c_head, layer, dec_batch, t):
    past = cache_kt.shape[3]
    tk = min(SAMPLE_KEY_TILE, past)
    nk = past // tk
    return pl.pallas_call(
        _fox_sample_kernel,
        out_shape=jax.ShapeDtypeStruct((dec_batch * t, HEADS_WIDTH), BF16),
        grid=(dec_batch, nk),
        in_specs=[pl.BlockSpec((t, HEADS_WIDTH), lambda b, j: (b, 0)),
                  pl.BlockSpec((None, None, HEADS_WIDTH, tk), lambda b, j: (layer, b, 0, j)),
                  pl.BlockSpec((None, None, HEADS_WIDTH, tk), lambda b, j: (layer, b, 0, j)),
                  pl.BlockSpec((t, HEADS_WIDTH), lambda b, j: (b, 1)),
                  pl.BlockSpec((t, HEADS_WIDTH), lambda b, j: (b, 2)),
                  pl.BlockSpec((None, N_HEADS * t, 1), lambda b, j: (b, 0, 0)),
                  pl.BlockSpec((None, N_HEADS, tk), lambda b, j: (b, 0, j)),
                  pl.BlockSpec((None, N_HEADS, LANES), lambda b, j: (b, 0, past // LANES))],
        out_specs=pl.BlockSpec((t, HEADS_WIDTH), lambda b, j: (b, 0)),
        scratch_shapes=[pltpu.VMEM((N_HEADS * t, HEADS_WIDTH), BF16),
                        pltpu.VMEM((1, N_HEADS * t, 1), F32), pltpu.VMEM((1, N_HEADS * t, 1), F32),
                        pltpu.VMEM((1, N_HEADS * t, HEADS_WIDTH), F32)],
        compiler_params=_params("parallel", "arbitrary"),
        name="fox_sample",
    )(qkv, cache_kt, cache_vt, qkv, qkv, cq_rows, c_head, c_head)


def _mla_sample_kernel(q_ref, latc_ref, krc_ref, kn_ref, wv_ref, o_ref,
                       qa_ref, qr_ref, m_ref, l_ref, acc_ref, *, past):
    t = q_ref.shape[0]
    ki = pl.program_id(1)

    @pl.when(ki == 0)
    def _():
        _init_softmax_state(m_ref, l_ref, acc_ref)
        for h in range(N_HEADS):
            qa_ref[h * t:(h + 1) * t, :] = q_ref[:, h * Q_CAT:h * Q_CAT + MLA_KV_RANK]
            qr_ref[h * t:(h + 1) * t, :] = q_ref[:, h * Q_CAT + MLA_KV_RANK:
                                                 h * Q_CAT + MLA_KV_RANK + MLA_ROPE]

    lat = latc_ref[...].astype(BF16)
    s = _dot_nt(qa_ref[...], lat) + _dot(qr_ref[...], krc_ref[...].astype(BF16))
    _online_softmax_update(s, lambda p: _dot(p, lat), m_ref, l_ref, acc_ref, 0)

    @pl.when(ki == pl.num_programs(1) - 1)
    def _():
        lat_n = kn_ref[:, :MLA_KV_RANK]
        s = _dot_nt(qa_ref[...], lat_n) + _dot_nt(qr_ref[...],
                                                  kn_ref[:, MLA_KV_RANK:MLA_KV_RANK + MLA_ROPE])
        q_pos = past + lax.broadcasted_iota(jnp.int32, s.shape, 0) % t
        k_pos = past + lax.broadcasted_iota(jnp.int32, s.shape, 1)
        s = jnp.where(k_pos // CHUNK <= q_pos // CHUNK, s, NEG_INF)
        _online_softmax_update(s, lambda p: _dot(p, lat_n), m_ref, l_ref, acc_ref, 0)
        o = (acc_ref[0] / l_ref[0]).astype(BF16)
        out = None
        for h in range(N_HEADS):
            term = _dot(o[h * t:(h + 1) * t, :], wv_ref[h, :MLA_KV_RANK, :])
            out = term if out is None else out + term
        o_ref[...] = out.astype(BF16)


def _mla_sample(qcat, kcat, cache_lat, cache_krt, wv_pad, layer, dec_batch, t):
    past = cache_lat.shape[2]
    tk = min(SAMPLE_KEY_TILE, past)
    nk = past // tk
    return pl.pallas_call(
        functools.partial(_mla_sample_kernel, past=past),
        out_shape=jax.ShapeDtypeStruct((dec_batch * t, HEADS_WIDTH), BF16),
        grid=(dec_batch, nk),
        in_specs=[pl.BlockSpec((t, N_HEADS * Q_CAT), lambda b, j: (b, 0)),
                  pl.BlockSpec((None, None, tk, MLA_KV_RANK), lambda b, j: (layer, b, j, 0)),
                  pl.BlockSpec((None, None, MLA_ROPE, tk), lambda b, j: (layer, b, 0, j)),
                  pl.BlockSpec((t, Q_CAT), lambda b, j: (b, 0)),
                  _const_spec(wv_pad.shape)],
        out_specs=pl.BlockSpec((t, HEADS_WIDTH), lambda b, j: (b, 0)),
        scratch_shapes=[pltpu.VMEM((N_HEADS * t, MLA_KV_RANK), BF16),
                        pltpu.VMEM((N_HEADS * t, MLA_ROPE), BF16),
                        pltpu.VMEM((1, N_HEADS * t, 1), F32), pltpu.VMEM((1, N_HEADS * t, 1), F32),
                        pltpu.VMEM((1, N_HEADS * t, MLA_KV_RANK), F32)],
        compiler_params=_params("parallel", "arbitrary"),
        name="mla_sample",
    )(qcat, cache_lat, cache_krt, kcat, wv_pad)


def _band_sample_kernel(q_ref, kc_ref, vc_ref, kn_ref, vn_ref, biasc_ref, biasn_ref, o_ref, qs_ref,
                        *, past):
    t = q_ref.shape[0]
    n_band = kc_ref.shape[1]
    _store_head_masked_q(q_ref, qs_ref, t)
    qs = qs_ref[...]

    def masked(s, first_pos):
        q_chunk = (past + lax.broadcasted_iota(jnp.int32, s.shape, 0) % t) // CHUNK
        k_pos = first_pos + lax.broadcasted_iota(jnp.int32, s.shape, 1)
        k_chunk = k_pos // CHUNK
        valid = (k_pos >= 0) & (k_chunk <= q_chunk) & (k_chunk >= q_chunk - BAND_CHUNKS)
        return jnp.where(valid, s, NEG_INF)

    s_c = masked(_dot(qs, kc_ref[...].astype(BF16)) + biasc_ref[...], past - n_band)
    s_n = masked(_dot_nt(qs, kn_ref[...]) + biasn_ref[...], past)
    m = jnp.maximum(jnp.max(s_c, axis=1, keepdims=True), jnp.max(s_n, axis=1, keepdims=True))
    p_c = jnp.exp(s_c - m)
    p_n = jnp.exp(s_n - m)
    denom = jnp.sum(p_c, axis=1, keepdims=True) + jnp.sum(p_n, axis=1, keepdims=True)
    acc = _dot_nt(p_c.astype(BF16), vc_ref[...].astype(BF16)) + _dot(p_n.astype(BF16), vn_ref[...])
    o_ref[...] = _unstack_heads(acc / denom, t).astype(BF16)


def _band_sample(qkv, cache_kt, cache_vt, bias_cache, bias_new, layer, past, dec_batch, t):
    n_band = cache_kt.shape[3]
    return pl.pallas_call(
        functools.partial(_band_sample_kernel, past=past),
        out_shape=jax.ShapeDtypeStruct((dec_batch * t, HEADS_WIDTH), BF16),
        grid=(dec_batch,),
        in_specs=[pl.BlockSpec((t, HEADS_WIDTH), lambda b: (b, 3)),
                  pl.BlockSpec((None, None, HEADS_WIDTH, n_band), lambda b: (layer, b, 0, 0)),
                  pl.BlockSpec((None, None, HEADS_WIDTH, n_band), lambda b: (layer, b, 0, 0)),
                  pl.BlockSpec((t, HEADS_WIDTH), lambda b: (b, 4)),
                  pl.BlockSpec((t, HEADS_WIDTH), lambda b: (b, 5)),
                  _const_spec(bias_cache.shape), _const_spec(bias_new.shape)],
        out_specs=pl.BlockSpec((t, HEADS_WIDTH), lambda b: (b, 0)),
        scratch_shapes=[pltpu.VMEM((N_HEADS * t, HEADS_WIDTH), BF16)],
        compiler_params=_params("parallel"),
        name="band_sample",
    )(qkv, cache_kt, cache_vt, qkv, qkv, bias_cache, bias_new)


def _sb_sample_kernel(q_ref, kc_ref, vc_ref, kn_ref, vn_ref, o_ref, qs_ref, u_ref, rest_ref, acc_ref):
    t = q_ref.shape[0]
    past = kc_ref.shape[1]
    _store_head_masked_q(q_ref, qs_ref, t)
    u = _strict_upper_ones(SB_SUB)
    u_ref[...] = u
    qs = qs_ref[...]

    z = _dot_nt(qs, kn_ref[...])
    row = lax.broadcasted_iota(jnp.int32, z.shape, 0) % t
    col = lax.broadcasted_iota(jnp.int32, z.shape, 1)
    a, rest = _sb_block(z, u[:t, :t], jnp.zeros((z.shape[0], 1), F32), col < row)
    acc_ref[...] = _dot(a, vn_ref[...])
    rest_ref[...] = rest

    def more(carry):
        return jnp.logical_and(carry[0] >= 0, carry[1] > SB_DEAD)

    def older(carry):
        start = pl.multiple_of(carry[0] * SB_SUB, SB_SUB)
        k = kc_ref[:, pl.ds(start, SB_SUB)].astype(BF16)
        v = vc_ref[:, pl.ds(start, SB_SUB)].astype(BF16)
        a, rest = _sb_block(_dot(qs_ref[...], k), u_ref[...], rest_ref[...], None)
        acc_ref[...] += _dot_nt(a, v)
        rest_ref[...] = rest
        return carry[0] - 1, jnp.max(rest)

    lax.while_loop(more, older, (past // SB_SUB - 1, jnp.max(rest)))
    o_ref[...] = _unstack_heads(acc_ref[...], t).astype(BF16)


def _sb_sample(qkv, cache_kt, cache_vt, layer, dec_batch, t):
    past = cache_kt.shape[3]
    return pl.pallas_call(
        _sb_sample_kernel,
        out_shape=jax.ShapeDtypeStruct((dec_batch * t, HEADS_WIDTH), BF16),
        grid=(dec_batch,),
        in_specs=[pl.BlockSpec((t, HEADS_WIDTH), lambda b: (b, 6)),
                  pl.BlockSpec((None, None, HEADS_WIDTH, past), lambda b: (layer, b, 0, 0)),
                  pl.BlockSpec((None, None, HEADS_WIDTH, past), lambda b: (layer, b, 0, 0)),
                  pl.BlockSpec((t, HEADS_WIDTH), lambda b: (b, 7)),
                  pl.BlockSpec((t, HEADS_WIDTH), lambda b: (b, 8))],
        out_specs=pl.BlockSpec((t, HEADS_WIDTH), lambda b: (b, 0)),
        scratch_shapes=[pltpu.VMEM((N_HEADS * t, HEADS_WIDTH), BF16),
                        pltpu.VMEM((SB_SUB, SB_SUB), BF16),
                        pltpu.VMEM((N_HEADS * t, 1), F32),
                        pltpu.VMEM((N_HEADS * t, HEADS_WIDTH), F32)],
        compiler_params=_params("parallel"),
        name="sb_sample",
    )(qkv, cache_kt, cache_vt, qkv, qkv)


def _merge_kernel(x_ref, oa_ref, ob_ref, oc_ref, od_ref, wg_ref, bg_ref, wb_ref, wo_ref,
                  g_ref, b_ref, wr_ref, y_ref, logit_ref, *, alpha):
    x = x_ref[...]
    xb = x.astype(BF16)
    merged = None
    for n, o_ref in enumerate((oa_ref, ob_ref, oc_ref, od_ref)):
        gate = jax.nn.sigmoid(_dot(xb, wg_ref[n]) + bg_ref[n])
        term = gate * _dot(o_ref[...], wb_ref[n])
        merged = term if merged is None else merged + term
    h = alpha * x + _dot(merged.astype(BF16), wo_ref[...])
    y = _layer_norm(h, g_ref[...], b_ref[...])
    y_ref[...] = y
    logit_ref[...] = _dot(y.astype(BF16), wr_ref[...])


def _merge(x, branches, wg, bg, wb, wo, g, b, wr, alpha):
    n = x.shape[0]
    tm = TOKEN_TILE
    row = lambda width: pl.BlockSpec((tm, width), lambda i: (i, 0))
    return pl.pallas_call(
        functools.partial(_merge_kernel, alpha=alpha),
        out_shape=[jax.ShapeDtypeStruct((n, D_MODEL), F32), jax.ShapeDtypeStruct((n, LANES), F32)],
        grid=(n // tm,),
        in_specs=[row(D_MODEL)] + [row(HEADS_WIDTH)] * 4
                 + [_const_spec(a.shape) for a in (wg, bg, wb, wo, g, b, wr)],
        out_specs=[row(D_MODEL), row(LANES)],
        compiler_params=_params("parallel"),
        name="merge",
    )(x, *branches, wg, bg, wb, wo, g, b, wr)


def _route_kernel(logit_ref, bias_ref, comb_ref):
    scores = jax.nn.sigmoid(logit_ref[...])
    sel = scores + bias_ref[...]
    rows = [sel[e:e + 1, :] for e in range(N_EXPERTS)]
    best = None
    best_group = None
    for g in range(N_GROUPS):
        m = rows[g * EXPERTS_PER_GROUP:(g + 1) * EXPERTS_PER_GROUP]
        top2 = None
        for i, j in GROUP_PAIRS:
            pair = m[i] + m[j]
            top2 = pair if top2 is None else jnp.maximum(top2, pair)
        if best is None:
            best, best_group = top2, jnp.zeros(top2.shape, jnp.int32)
        else:
            better = top2 > best
            best = jnp.where(better, top2, best)
            best_group = jnp.where(better, g, best_group)
    weights = []
    total = None
    for e in range(N_EXPERTS):
        g, i = divmod(e, EXPERTS_PER_GROUP)
        rank = jnp.zeros(best.shape, jnp.int32)
        for j in range(EXPERTS_PER_GROUP):
            if j == i:
                continue
            other = rows[g * EXPERTS_PER_GROUP + j]
            ahead = (other >= rows[e]) if j < i else (other > rows[e])
            rank = rank + ahead.astype(jnp.int32)
        chosen = (best_group == g) & (rank < 2)
        w = jnp.where(chosen, scores[e:e + 1, :], 0.0)
        weights.append(w)
        total = w if total is None else total + w
    comb_ref[...] = jnp.concatenate(weights, axis=0) / total


def _route(logits_t, bias_col):
    return pl.pallas_call(
        _route_kernel,
        out_shape=jax.ShapeDtypeStruct(logits_t.shape, F32),
        compiler_params=pltpu.CompilerParams(vmem_limit_bytes=VMEM_LIMIT),
        name="route",
    )(logits_t, bias_col)


def _moe_kernel(x_ref, comb_ref, wg_ref, wu_ref, wd_ref, g_ref, b_ref, y_ref, xb_ref, acc_ref, *, alpha):
    e = pl.program_id(1)

    @pl.when(e == 0)
    def _():
        xb_ref[...] = x_ref[...].astype(BF16)
        acc_ref[...] = jnp.zeros(acc_ref.shape, F32)

    xb = xb_ref[...]
    hidden = jax.nn.silu(_dot(xb, wg_ref[...])) * _dot(xb, wu_ref[...])
    out = _dot(hidden.astype(BF16), wd_ref[...])
    comb = comb_ref[...]
    lane = lax.broadcasted_iota(jnp.int32, comb.shape, 1)
    weight = jnp.sum(jnp.where(lane == e, comb, 0.0), axis=1, keepdims=True)
    acc_ref[...] += weight * out

    @pl.when(e == pl.num_programs(1) - 1)
    def _():
        y_ref[...] = _layer_norm(alpha * x_ref[...] + acc_ref[...], g_ref[...], b_ref[...])


def _moe(x, logits, rb, wg, wu, wd, g, b, alpha):
    n = x.shape[0]
    tm = MOE_TILE
    comb = _route(logits.T, rb).T
    return pl.pallas_call(
        functools.partial(_moe_kernel, alpha=alpha),
        out_shape=jax.ShapeDtypeStruct((n, D_MODEL), F32),
        grid=(n // tm, N_EXPERTS),
        in_specs=[pl.BlockSpec((tm, D_MODEL), lambda i, e: (i, 0)),
                  pl.BlockSpec((tm, N_EXPERTS), lambda i, e: (i, 0)),
                  pl.BlockSpec((None, D_MODEL, D_EXPERT), lambda i, e: (e, 0, 0)),
                  pl.BlockSpec((None, D_MODEL, D_EXPERT), lambda i, e: (e, 0, 0)),
                  pl.BlockSpec((None, D_EXPERT, D_MODEL), lambda i, e: (e, 0, 0)),
                  _const_spec(g.shape), _const_spec(b.shape)],
        out_specs=pl.BlockSpec((tm, D_MODEL), lambda i, e: (i, 0)),
        scratch_shapes=[pltpu.VMEM((tm, D_MODEL), BF16), pltpu.VMEM((tm, D_MODEL), F32)],
        compiler_params=_params("parallel", "arbitrary"),
        name="moe",
    )(x, comb, wg, wu, wd, g, b)


IN_SIZES = (HEADS_WIDTH,) * 3 + (N_HEADS, MLA_Q_RANK, MLA_KV_RANK, MLA_ROPE) + (HEADS_WIDTH,) * 6


def _pad_cols(a, width):
    return jnp.pad(a, ((0, 0), (0, width - a.shape[1])))


def _prep_in_proj_weights(w_in, b_forget, q_norm, w_q_up, w_kv_up):
    offs = np.cumsum(IN_SIZES)[:-1].tolist()
    qa, ka, va, fa, cq, ckv, kr, qc, kc, vc, qd, kd, vd = jnp.split(w_in, offs, axis=1)
    w = jnp.concatenate([qa, ka, va, qc, kc, vc, qd, kd, vd, _pad_cols(cq, 256), ckv,
                         _pad_cols(kr, LANES), _pad_cols(fa, LANES)], axis=1).astype(BF16)
    bf = _pad_cols(b_forget[None, :], LANES)
    qn = _pad_cols(q_norm[None, :], 256)
    wq3 = w_q_up.reshape(MLA_Q_RANK, N_HEADS, MLA_NOPE + MLA_ROPE)
    wq = jnp.concatenate([wq3[:, :, :MLA_NOPE].reshape(MLA_Q_RANK, -1),
                          wq3[:, :, MLA_NOPE:].reshape(MLA_Q_RANK, -1)], axis=1)
    wq = jnp.pad(wq, ((0, 256 - MLA_Q_RANK), (0, 0))).astype(BF16)
    wkv = w_kv_up.reshape(MLA_KV_RANK, N_HEADS, MLA_NOPE + MLA_V)
    wcat = jnp.zeros((HEADS_WIDTH + N_HEADS * MLA_ROPE, N_HEADS * Q_CAT), F32)
    wv_pad = jnp.zeros((N_HEADS, Q_CAT, HEADS_WIDTH), F32)
    eye = jnp.eye(MLA_ROPE, dtype=F32)
    for h in range(N_HEADS):
        wcat = wcat.at[h * MLA_NOPE:(h + 1) * MLA_NOPE, h * Q_CAT:h * Q_CAT + MLA_KV_RANK].set(
            wkv[:, h, :MLA_NOPE].T)
        wcat = wcat.at[HEADS_WIDTH + h * MLA_ROPE:HEADS_WIDTH + (h + 1) * MLA_ROPE,
                       h * Q_CAT + MLA_KV_RANK:h * Q_CAT + MLA_KV_RANK + MLA_ROPE].set(eye)
        wv_pad = wv_pad.at[h, :MLA_KV_RANK, h * MLA_V:(h + 1) * MLA_V].set(wkv[:, h, MLA_NOPE:])
    return w, bf, qn, wq, wcat.astype(BF16), wv_pad.astype(BF16)


def _rope_tables(pos):
    half = MLA_ROPE // 2
    inv = ROPE_THETA ** (-jnp.arange(half, dtype=F32) / half)
    ang = pos.astype(F32)[:, None] * inv[None, :]
    cos = jnp.cos(ang)
    sin = jnp.sin(ang)
    reps = LANES // MLA_ROPE
    return (jnp.tile(jnp.concatenate([cos, cos], axis=1), (1, reps)),
            jnp.tile(jnp.concatenate([-sin, sin], axis=1), (1, reps)))


def _band_bias(rel_bias, q0, nq, k0, nk):
    period = nq + nk
    j = np.arange(period)
    d = np.where(j < nk, j, j - period)
    idx = np.clip(q0 - k0 - d, -REL_CLIP, REL_CLIP) + REL_CLIP
    u = rel_bias[:, idx]
    rows = jnp.tile(u, (1, nq))[:, :nq * (period - 1)].reshape(rel_bias.shape[0], nq, period - 1)
    return rows[:, :, :nk]


def _key_major(cache):
    l, b, p, h, d = cache.shape
    return jnp.transpose(cache, (0, 1, 3, 4, 2)).reshape(l, b, h * d, p)


def _from_key_major(state, n_heads):
    b, hd, t = state.shape
    return jnp.transpose(state.reshape(b, n_heads, hd // n_heads, t), (0, 3, 1, 2))


def kernel(x_prompt, x_sample, cache_fox_k, cache_fox_v, cache_fox_logf, cache_mla_latent, cache_mla_krope, cache_band_k, cache_band_v, cache_sb_k, cache_sb_v, w_in, b_forget, mla_q_norm, w_q_up, mla_kv_norm, w_kv_up, rel_bias, w_branch, w_gate, b_gate, w_out, ln1_g, ln1_b, w_router, router_bias, w_e_gate, w_e_up, w_e_down, ln2_g, ln2_b):
    batch, seq, _ = x_prompt.shape
    dec_batch, t_new, _ = x_sample.shape
    depth = w_in.shape[0]
    past = cache_fox_k.shape[2]
    n_band = cache_band_k.shape[2]
    n_p = batch * seq
    n_s = dec_batch * t_new
    alpha = (2 * depth) ** 0.25
    band_tile = BAND_CHUNKS * CHUNK
    assert seq % ATT_TILE == 0 and seq % band_tile == 0 and seq % TOKEN_TILE == 0
    assert n_p % MOE_TILE == 0 and n_s % MOE_TILE == 0 and n_s % TOKEN_TILE == 0
    assert past % LANES == 0 and t_new <= LANES and t_new % 16 == 0 and ATT_TILE % SB_SUB == 0
    assert past % min(SAMPLE_KEY_TILE, past) == 0 and past % SB_SUB == 0

    xp = x_prompt.reshape(n_p, D_MODEL)
    xs = x_sample.reshape(n_s, D_MODEL)
    rope_p = _rope_tables(jnp.tile(jnp.arange(seq, dtype=jnp.int32), batch))
    rope_s = _rope_tables(jnp.tile(past + jnp.arange(t_new, dtype=jnp.int32), dec_batch))

    c_fox_k, c_fox_v = _key_major(cache_fox_k), _key_major(cache_fox_v)
    c_band_k, c_band_v = _key_major(cache_band_k), _key_major(cache_band_v)
    c_sb_k, c_sb_v = _key_major(cache_sb_k), _key_major(cache_sb_v)
    c_logf = jnp.swapaxes(cache_fox_logf, 2, 3).astype(F32)
    c_krope = jnp.swapaxes(cache_mla_krope, 2, 3)

    wr = _pad_cols(w_router, LANES).astype(BF16)
    rb = router_bias.astype(F32)[:, None]
    total = past + t_new
    padded = -(-total // LANES) * LANES

    prompt_states, sample_states = [], []
    for l in range(depth):
        w, bf, qn, wq, wcat, wv_pad = _prep_in_proj_weights(w_in[l], b_forget[l], mla_q_norm[l],
                                                             w_q_up[l], w_kv_up[l])
        proj_w = (w, bf, qn, wq, mla_kv_norm[l][None, :], wcat)
        (qkv_p, ka_p, va_p, kc_p, vc_p, kd_p, vd_p, logf_p, lat_p, kr_p, kcat_p, qcat_p,
         vat_p, vct_p, vdt_p, latt_p) = _in_proj(xp, *proj_w, *rope_p, batch=batch, seq=seq)
        (qkv_s, ka_s, va_s, kc_s, vc_s, kd_s, vd_s, logf_s, lat_s, kr_s, kcat_s, qcat_s) = _in_proj(
            xs, *proj_w, *rope_s)
        logf_s = logf_s[:, :N_HEADS].reshape(dec_batch, t_new, N_HEADS)
        kr_s = kr_s[:, :MLA_ROPE]

        c_p = _cumsum_rows(logf_p.reshape(batch * N_HEADS, seq)).reshape(batch, N_HEADS, seq)
        all_s = jnp.concatenate([c_logf[l], jnp.swapaxes(logf_s, 1, 2)], axis=2)
        all_s = jnp.pad(all_s, ((0, 0), (0, 0), (0, padded - total)))
        c_s = _cumsum_rows(all_s.reshape(dec_batch * N_HEADS, padded)).reshape(dec_batch, N_HEADS, padded)
        cq_s = c_s[:, :, past:past + t_new].reshape(dec_batch, N_HEADS * t_new, 1)

        bias_p = _band_bias(rel_bias[l], 0, band_tile, -band_tile, 2 * band_tile)
        bias_s = _band_bias(rel_bias[l], past, t_new, past - n_band, n_band + t_new)
        bias_s = bias_s.reshape(N_HEADS * t_new, n_band + t_new)

        branches_p = (_fox_prompt(qkv_p, vat_p, jnp.swapaxes(c_p, 1, 2), c_p, batch, seq),
                      _mla_prompt(qcat_p, kcat_p, latt_p, wv_pad, batch, seq),
                      _band_prompt(qkv_p, vct_p, jnp.swapaxes(bias_p, 1, 2), batch, seq),
                      _sb_prompt(qkv_p, vdt_p, batch, seq))
        branches_s = (_fox_sample(qkv_s, c_fox_k, c_fox_v, cq_s, c_s, l, dec_batch, t_new),
                      _mla_sample(qcat_s, kcat_s, cache_mla_latent, c_krope, wv_pad, l, dec_batch, t_new),
                      _band_sample(qkv_s, c_band_k, c_band_v, bias_s[:, :n_band], bias_s[:, n_band:], l,
                                   past, dec_batch, t_new),
                      _sb_sample(qkv_s, c_sb_k, c_sb_v, l, dec_batch, t_new))

        tail_w = (w_gate[l].astype(BF16), b_gate[l][:, None, :], w_branch[l].astype(BF16),
                  w_out[l].astype(BF16), ln1_g[l][None, :], ln1_b[l][None, :], wr, alpha)
        experts = (w_e_gate[l].astype(BF16), w_e_up[l].astype(BF16), w_e_down[l].astype(BF16),
                   ln2_g[l][None, :], ln2_b[l][None, :], alpha)
        x1_p, logits_p = _merge(xp, branches_p, *tail_w)
        x1_s, logits_s = _merge(xs, branches_s, *tail_w)
        xp = _moe(x1_p, logits_p[:, :N_EXPERTS], rb, *experts)
        xs = _moe(x1_s, logits_s[:, :N_EXPERTS], rb, *experts)

        n_keep = min(band_tile, seq)
        prompt_states.append((
            _from_key_major(ka_p, N_HEADS), _from_key_major(va_p, N_HEADS), jnp.swapaxes(logf_p, 1, 2),
            lat_p.reshape(batch, seq, -1), jnp.swapaxes(kr_p, 1, 2),
            _from_key_major(kc_p[:, :, seq - n_keep:], N_HEADS),
            _from_key_major(vc_p[:, :, seq - n_keep:], N_HEADS),
            _from_key_major(kd_p, N_HEADS), _from_key_major(vd_p, N_HEADS)))
        heads = lambda a: a.reshape(dec_batch, t_new, N_HEADS, HEAD_DIM)
        sample_states.append((
            heads(ka_s), heads(va_s), logf_s, lat_s.reshape(dec_batch, t_new, -1),
            kr_s.reshape(dec_batch, t_new, -1), heads(kc_s), heads(vc_s), heads(kd_s), heads(vd_s)))

    stack = lambda states: [jnp.stack(z, axis=0) for z in zip(*states)]
    return (xp.reshape(batch, seq, D_MODEL), xs.reshape(dec_batch, t_new, D_MODEL),
            *stack(prompt_states), *stack(sample_states))
```

```python
import functools

import numpy as np
import jax
import jax.numpy as jnp
from jax import lax
from jax.experimental import pallas as pl
from jax.experimental.pallas import tpu as pltpu

F32 = jnp.float32
BF16 = jnp.bfloat16

D_MODEL = 1024
N_HEADS = 4
HEAD_DIM = 64
HEADS_WIDTH = N_HEADS * HEAD_DIM
CHUNK = 64
BAND_CHUNKS = 8
REL_CLIP = 128
MLA_NOPE = 64
MLA_ROPE = 32
MLA_V = 64
MLA_Q_RANK = 192
MLA_KV_RANK = 128
ROPE_THETA = 10000.0
N_BRANCH = 4
N_EXPERTS = 16
N_GROUPS = 4
EXPERTS_PER_GROUP = N_EXPERTS // N_GROUPS
D_EXPERT = 512
GROUP_PAIRS = tuple((i, j) for i in range(EXPERTS_PER_GROUP) for j in range(i + 1, EXPERTS_PER_GROUP))
N_BUCKETS = N_GROUPS * len(GROUP_PAIRS)
BUCKET_ROWS = 32
NEG_INF = -1e30
LANES = 128
TILE_ROWS = D_MODEL // LANES
VMEM_LIMIT = 56 * 1024 * 1024

QKV_WIDTH = 9 * HEADS_WIDTH
STATE_COLS = (1, 2, 4, 5, 7, 8)
VALUE_COLS = (2, 5, 8)
OFF_CQ = QKV_WIDTH
OFF_CKV = OFF_CQ + 256
OFF_KR = OFF_CKV + MLA_KV_RANK
OFF_FA = OFF_KR + LANES
PROJ_WIDTH = OFF_FA + LANES
Q_CAT = 256

TOKEN_TILE = 512
MOE_TILE = 256
MOE_TILE_SAMPLE = 128
GATHER_ROWS = 2048
ATT_TILE = 512
SB_SUB = 256
SAMPLE_KEY_TILE = 4096
SB_DEAD = -105.0


def _dot(a, b):
    return jnp.dot(a, b, preferred_element_type=F32)


def _dot_nt(a, b):
    return lax.dot_general(a, b, (((1,), (1,)), ((), ())), preferred_element_type=F32)


def _const_spec(shape):
    nd = len(shape)
    return pl.BlockSpec(shape, lambda *_: (0,) * nd, pipeline_mode=pl.Buffered(1))


def _params(*sem):
    return pltpu.CompilerParams(dimension_semantics=sem, vmem_limit_bytes=VMEM_LIMIT)


def _log_sigmoid(x):
    return jnp.minimum(x, 0.0) - jnp.log1p(jnp.exp(-jnp.abs(x)))


def _softplus(x):
    return jnp.maximum(x, 0.0) + jnp.log1p(jnp.exp(-jnp.abs(x)))


def _layer_norm(h, g, b):
    mu = jnp.mean(h, axis=-1, keepdims=True)
    d = h - mu
    var = jnp.mean(d * d, axis=-1, keepdims=True)
    return d * lax.rsqrt(var + 1e-5) * g + b


def _head_of_lane(shape):
    return lax.broadcasted_iota(jnp.int32, shape, len(shape) - 1) // HEAD_DIM


def _in_proj_kernel(x_ref, w_ref, bf_ref, qn_ref, wq_ref, kvn_ref, wcat_ref, cos_ref, sin_ref,
                    qkv_ref, ka_ref, va_ref, kc_ref, vc_ref, kd_ref, vd_ref,
                    logf_ref, lat_ref, kr_ref, kcat_ref, qcat_ref, *key_major_refs, key_major):
    xb = x_ref[...].astype(BF16)
    proj = _dot(xb, w_ref[...])
    qkv_ref[...] = proj[:, :QKV_WIDTH].astype(BF16)
    for ref, col in zip((ka_ref, va_ref, kc_ref, vc_ref, kd_ref, vd_ref), STATE_COLS):
        state = proj[:, col * HEADS_WIDTH:(col + 1) * HEADS_WIDTH]
        if key_major:
            state = state.T
            if col in VALUE_COLS:
                key_major_refs[VALUE_COLS.index(col)][...] = state.astype(BF16)
        ref[...] = state

    cos = cos_ref[...]
    sin = sin_ref[...]
    lane = lax.broadcasted_iota(jnp.int32, cos.shape, 1)
    first_half = (lane % MLA_ROPE) < (MLA_ROPE // 2)

    def rope(v):
        partner = jnp.where(first_half, pltpu.roll(v, LANES - MLA_ROPE // 2, 1),
                            pltpu.roll(v, MLA_ROPE // 2, 1))
        return v * cos + partner * sin

    cq = proj[:, OFF_CQ:OFF_CQ + 256]
    ms = jnp.sum(cq * cq, axis=-1, keepdims=True) * (1.0 / MLA_Q_RANK)
    cqn = cq * lax.rsqrt(ms + 1e-6) * qn_ref[...]
    qb = _dot(cqn.astype(BF16), wq_ref[...])
    q_rope = rope(qb[:, HEADS_WIDTH:])
    q_in = jnp.concatenate([qb[:, :HEADS_WIDTH], q_rope], axis=1) * ((MLA_NOPE + MLA_ROPE) ** -0.5)
    qcat_ref[...] = _dot(q_in.astype(BF16), wcat_ref[...]).astype(BF16)

    ckv = proj[:, OFF_CKV:OFF_CKV + MLA_KV_RANK]
    lat = ckv * lax.rsqrt(jnp.mean(ckv * ckv, axis=-1, keepdims=True) + 1e-6) * kvn_ref[...]
    lat_ref[...] = lat
    kr = rope(proj[:, OFF_KR:OFF_KR + LANES])
    kcat_ref[...] = jnp.concatenate([lat, kr], axis=1).astype(BF16)
    logf = _log_sigmoid(proj[:, OFF_FA:OFF_FA + LANES] + bf_ref[...])
    if key_major:
        kr_ref[...] = kr.T[:MLA_ROPE, :]
        logf_ref[...] = logf.T[:N_HEADS, :]
        key_major_refs[len(VALUE_COLS)][...] = lat.T.astype(BF16)
    else:
        kr_ref[...] = kr
        logf_ref[...] = logf


def _in_proj(x, w, bf, qn, wq, kvn, wcat, cos, sin, *, batch=None, seq=None):
    n = x.shape[0]
    tm = TOKEN_TILE
    key_major = batch is not None
    row = lambda width: pl.BlockSpec((tm, width), lambda i: (i, 0))
    if key_major:
        per = seq // tm
        tr = lambda height: pl.BlockSpec((None, height, tm), lambda i: (i // per, 0, i % per))
        tshape = lambda height: jax.ShapeDtypeStruct((batch, height, seq), F32)
        state_shapes = [tshape(HEADS_WIDTH)] * 6 + [tshape(N_HEADS), jax.ShapeDtypeStruct((n, LANES), F32),
                                                    tshape(MLA_ROPE)]
        state_specs = [tr(HEADS_WIDTH)] * 6 + [tr(N_HEADS), row(LANES), tr(MLA_ROPE)]
    else:
        state_shapes = ([jax.ShapeDtypeStruct((n, HEADS_WIDTH), F32)] * 6
                        + [jax.ShapeDtypeStruct((n, LANES), F32)] * 3)
        state_specs = [row(HEADS_WIDTH)] * 6 + [row(LANES)] * 3
    out_shape = ([jax.ShapeDtypeStruct((n, QKV_WIDTH), BF16)] + state_shapes
                 + [jax.ShapeDtypeStruct((n, 2 * LANES), BF16),
                    jax.ShapeDtypeStruct((n, N_HEADS * Q_CAT), BF16)])
    out_specs = [row(QKV_WIDTH)] + state_specs + [row(2 * LANES), row(N_HEADS * Q_CAT)]
    if key_major:
        out_shape += ([jax.ShapeDtypeStruct((batch, HEADS_WIDTH, seq), BF16)] * len(VALUE_COLS)
                      + [jax.ShapeDtypeStruct((batch, MLA_KV_RANK, seq), BF16)])
        out_specs += [tr(HEADS_WIDTH)] * len(VALUE_COLS) + [tr(MLA_KV_RANK)]
    return pl.pallas_call(
        functools.partial(_in_proj_kernel, key_major=key_major),
        out_shape=out_shape,
        grid=(n // tm,),
        in_specs=[row(D_MODEL), _const_spec(w.shape), _const_spec(bf.shape), _const_spec(qn.shape),
                  _const_spec(wq.shape), _const_spec(kvn.shape), _const_spec(wcat.shape),
                  row(LANES), row(LANES)],
        out_specs=out_specs,
        compiler_params=_params("parallel"),
        name="in_proj_prompt" if key_major else "in_proj_sample",
    )(x, w, bf, qn, wq, kvn, wcat, cos, sin)


def _cumsum_kernel(x_ref, o_ref):
    r, t = x_ref.shape
    row = lax.broadcasted_iota(jnp.int32, (LANES, LANES), 0)
    col = lax.broadcasted_iota(jnp.int32, (LANES, LANES), 1)
    tri = jnp.where(row <= col, 1.0, 0.0).astype(BF16)
    carry = jnp.zeros((r, 1), F32)
    for c in range(t // LANES):
        xc = x_ref[:, c * LANES:(c + 1) * LANES]
        hi = xc.astype(BF16)
        r1 = xc - hi.astype(F32)
        mid = r1.astype(BF16)
        lo = (r1 - mid.astype(F32)).astype(BF16)
        y = _dot(hi, tri) + _dot(mid, tri) + _dot(lo, tri) + carry
        o_ref[:, c * LANES:(c + 1) * LANES] = y
        carry = y[:, LANES - 1:LANES]


def _cumsum_rows(x):
    return pl.pallas_call(
        _cumsum_kernel,
        out_shape=jax.ShapeDtypeStruct(x.shape, F32),
        name="cumsum",
    )(x)


def _online_softmax_update(s, pv, m_ref, l_ref, acc_ref, h):
    m_prev = m_ref[h]
    m_new = jnp.maximum(m_prev, jnp.max(s, axis=1, keepdims=True))
    alpha = jnp.exp(m_prev - m_new)
    p = jnp.exp(s - m_new)
    l_ref[h] = alpha * l_ref[h] + jnp.sum(p, axis=1, keepdims=True)
    acc_ref[h] = alpha * acc_ref[h] + pv(p.astype(BF16))
    m_ref[h] = m_new


def _init_softmax_state(m_ref, l_ref, acc_ref):
    m_ref[...] = jnp.full(m_ref.shape, NEG_INF, F32)
    l_ref[...] = jnp.zeros(l_ref.shape, F32)
    acc_ref[...] = jnp.zeros(acc_ref.shape, F32)


def _store_head_masked_q(q_ref, qs_ref, rows):
    q = q_ref[...] * (HEAD_DIM ** -0.5)
    head = _head_of_lane(q.shape)
    for h in range(N_HEADS):
        qs_ref[h * rows:(h + 1) * rows, :] = jnp.where(head == h, q, jnp.zeros_like(q))


def _merge_heads(per_head):
    head = _head_of_lane(per_head[0].shape)
    out = jnp.zeros_like(per_head[0])
    for h in range(N_HEADS):
        out = jnp.where(head == h, per_head[h], out)
    return out


def _online_softmax_update_t(st, pv, m_ref, l_ref, acc_ref, h):
    m_prev = m_ref[h]
    m_new = jnp.maximum(m_prev, jnp.max(st, axis=0, keepdims=True))
    alpha = jnp.exp(m_prev - m_new)
    p = jnp.exp(st - m_new)
    l_ref[h] = alpha * l_ref[h] + jnp.sum(p, axis=0, keepdims=True)
    acc_ref[h] = alpha * acc_ref[h] + pv(p.astype(BF16))
    m_ref[h] = m_new


def _fox_prompt_kernel(q_ref, k_ref, vt_ref, cq_ref, ck_ref, o_ref, qt_ref, m_ref, l_ref, acc_ref):
    qi = pl.program_id(1)
    ki = pl.program_id(2)

    @pl.when(ki == 0)
    def _():
        _init_softmax_state(m_ref, l_ref, acc_ref)
        _store_head_masked_qt(q_ref, qt_ref)

    def step(diagonal):
        k = k_ref[...]
        vt = vt_ref[...]
        cq = cq_ref[...]
        ck = ck_ref[...]
        for h in range(N_HEADS):
            st = _dot(k, qt_ref[h]) + cq[h:h + 1, :] - ck[:, h:h + 1]
            if diagonal:
                key = lax.broadcasted_iota(jnp.int32, st.shape, 0)
                qry = lax.broadcasted_iota(jnp.int32, st.shape, 1)
                st = jnp.where(key <= qry, st, NEG_INF)
            vt_h = vt[h * HEAD_DIM:(h + 1) * HEAD_DIM, :]
            _online_softmax_update_t(st, lambda p: _dot(vt_h, p), m_ref, l_ref, acc_ref, h)

    pl.when(ki < qi)(lambda: step(False))
    pl.when(ki == qi)(lambda: step(True))

    @pl.when(ki == pl.num_programs(2) - 1)
    def _():
        ot = jnp.concatenate([acc_ref[h] / l_ref[h] for h in range(N_HEADS)], axis=0)
        o_ref[...] = ot.T.astype(BF16)


def _fox_prompt(qkv, vt, c_tok, c_head, batch, seq):
    t = ATT_TILE
    nt = seq // t
    return pl.pallas_call(
        _fox_prompt_kernel,
        out_shape=jax.ShapeDtypeStruct((batch * seq, HEADS_WIDTH), BF16),
        grid=(batch, nt, nt),
        in_specs=[pl.BlockSpec((t, HEADS_WIDTH), lambda b, i, j: (b * nt + i, 0)),
                  pl.BlockSpec((t, HEADS_WIDTH), lambda b, i, j: (b * nt + jnp.minimum(i, j), 1)),
                  pl.BlockSpec((None, HEADS_WIDTH, t), lambda b, i, j: (b, 0, jnp.minimum(i, j))),
                  pl.BlockSpec((None, N_HEADS, t), lambda b, i, j: (b, 0, i)),
                  pl.BlockSpec((None, t, N_HEADS), lambda b, i, j: (b, jnp.minimum(i, j), 0))],
        out_specs=pl.BlockSpec((t, HEADS_WIDTH), lambda b, i, j: (b * nt + i, 0)),
        scratch_shapes=[pltpu.VMEM((N_HEADS, HEADS_WIDTH, t), BF16),
                        pltpu.VMEM((N_HEADS, 1, t), F32), pltpu.VMEM((N_HEADS, 1, t), F32),
                        pltpu.VMEM((N_HEADS, HEAD_DIM, t), F32)],
        compiler_params=_params("parallel", "parallel", "arbitrary"),
        name="fox_prompt",
    )(qkv, qkv, vt, c_head, c_tok)


def _mla_prompt_kernel(q_ref, k_ref, latt_ref, wv_ref, o_ref, qt_ref, m_ref, l_ref, acc_ref):
    qi = pl.program_id(1)
    ki = pl.program_id(2)

    @pl.when(ki == 0)
    def _():
        _init_softmax_state(m_ref, l_ref, acc_ref)
        for h in range(N_HEADS):
            qt_ref[h] = q_ref[:, h * Q_CAT:(h + 1) * Q_CAT].astype(F32).T.astype(BF16)

    def step(diagonal):
        k = k_ref[...]
        latt = latt_ref[...]
        for h in range(N_HEADS):
            st = _dot(k, qt_ref[h])
            if diagonal:
                key = lax.broadcasted_iota(jnp.int32, st.shape, 0)
                qry = lax.broadcasted_iota(jnp.int32, st.shape, 1)
                st = jnp.where(key // CHUNK <= qry // CHUNK, st, NEG_INF)
            _online_softmax_update_t(st, lambda p: _dot(latt, p), m_ref, l_ref, acc_ref, h)

    pl.when(ki < qi)(lambda: step(False))
    pl.when(ki == qi)(lambda: step(True))

    @pl.when(ki == pl.num_programs(2) - 1)
    def _():
        out = None
        for h in range(N_HEADS):
            o_h = (acc_ref[h] / l_ref[h]).T.astype(BF16)
            term = _dot(o_h, wv_ref[h, :MLA_KV_RANK, :])
            out = term if out is None else out + term
        o_ref[...] = out.astype(BF16)


def _mla_prompt(qcat, kcat, latt, wv_pad, batch, seq):
    t = ATT_TILE
    nt = seq // t
    return pl.pallas_call(
        _mla_prompt_kernel,
        out_shape=jax.ShapeDtypeStruct((batch * seq, HEADS_WIDTH), BF16),
        grid=(batch, nt, nt),
        in_specs=[pl.BlockSpec((t, N_HEADS * Q_CAT), lambda b, i, j: (b * nt + i, 0)),
                  pl.BlockSpec((t, Q_CAT), lambda b, i, j: (b * nt + jnp.minimum(i, j), 0)),
                  pl.BlockSpec((None, MLA_KV_RANK, t), lambda b, i, j: (b, 0, jnp.minimum(i, j))),
                  _const_spec(wv_pad.shape)],
        out_specs=pl.BlockSpec((t, HEADS_WIDTH), lambda b, i, j: (b * nt + i, 0)),
        scratch_shapes=[pltpu.VMEM((N_HEADS, Q_CAT, t), BF16),
                        pltpu.VMEM((N_HEADS, 1, t), F32), pltpu.VMEM((N_HEADS, 1, t), F32),
                        pltpu.VMEM((N_HEADS, MLA_KV_RANK, t), F32)],
        compiler_params=_params("parallel", "parallel", "arbitrary"),
        name="mla_prompt",
    )(qcat, kcat, latt, wv_pad)


def _strict_upper_ones(n):
    row = lax.broadcasted_iota(jnp.int32, (n, n), 0)
    col = lax.broadcasted_iota(jnp.int32, (n, n), 1)
    return jnp.where(row > col, 1.0, 0.0).astype(BF16)


def _sb_block(z, u, rest, mask):
    sp = _softplus(z)
    log_1m = -sp
    if mask is not None:
        log_1m = jnp.where(mask, log_1m, 0.0)
    hi = log_1m.astype(BF16)
    lo = (log_1m - hi.astype(F32)).astype(BF16)
    later = _dot(hi, u) + _dot(lo, u) + rest
    a = jnp.exp(z - sp + later)
    if mask is not None:
        a = jnp.where(mask, a, 0.0)
    return a.astype(BF16), rest + jnp.sum(log_1m, axis=1, keepdims=True)


def _store_head_masked_qt(q_ref, qt_ref):
    qt = q_ref[...].astype(F32).T * (HEAD_DIM ** -0.5)
    head = lax.broadcasted_iota(jnp.int32, qt.shape, 0) // HEAD_DIM
    for h in range(N_HEADS):
        qt_ref[h] = jnp.where(head == h, qt, 0.0).astype(BF16)


def _sb_block_t(zt, u, rest, mask):
    sp = _softplus(zt)
    log_1m = -sp
    if mask is not None:
        log_1m = jnp.where(mask, log_1m, 0.0)
    hi = log_1m.astype(BF16)
    lo = (log_1m - hi.astype(F32)).astype(BF16)
    later = _dot(u, hi) + _dot(u, lo) + rest
    a = jnp.exp(zt - sp + later)
    if mask is not None:
        a = jnp.where(mask, a, 0.0)
    return a.astype(BF16), rest + jnp.sum(log_1m, axis=0, keepdims=True)


def _sb_prompt_kernel(q_ref, k_ref, vt_ref, o_ref, qt_ref, u_ref, rest_ref, acc_ref):
    tq = q_ref.shape[0]
    qi = pl.program_id(1)
    _store_head_masked_qt(q_ref, qt_ref)
    key_s = lax.broadcasted_iota(jnp.int32, (SB_SUB, SB_SUB), 0)
    key_j = lax.broadcasted_iota(jnp.int32, (SB_SUB, SB_SUB), 1)
    u_ref[...] = jnp.where(key_j > key_s, 1.0, 0.0).astype(BF16)
    rest_ref[...] = jnp.zeros(rest_ref.shape, F32)
    acc_ref[...] = jnp.zeros(acc_ref.shape, F32)

    def visit(start, first_key):
        k = k_ref[pl.ds(start, SB_SUB), :]
        vt = vt_ref[:, pl.ds(start, SB_SUB)]
        u = u_ref[...]
        live = None
        for h in range(N_HEADS):
            zt = _dot(k, qt_ref[h])
            mask = None
            if first_key is not None:
                key = lax.broadcasted_iota(jnp.int32, zt.shape, 0) + first_key
                qry = lax.broadcasted_iota(jnp.int32, zt.shape, 1)
                mask = key < qry
            a, rest = _sb_block_t(zt, u, rest_ref[h], mask)
            acc_ref[h] += _dot(vt[h * HEAD_DIM:(h + 1) * HEAD_DIM, :], a)
            rest_ref[h] = rest
            top = jnp.max(rest)
            live = top if live is None else jnp.maximum(live, top)
        return live

    live = None
    for sub in reversed(range(tq // SB_SUB)):
        live = visit(pl.multiple_of(qi * tq + sub * SB_SUB, SB_SUB), sub * SB_SUB)

    def more(carry):
        return jnp.logical_and(carry[0] >= 0, carry[1] > SB_DEAD)

    def older(carry):
        return carry[0] - 1, visit(pl.multiple_of(carry[0] * SB_SUB, SB_SUB), None)

    lax.while_loop(more, older, (qi * (tq // SB_SUB) - 1, live))
    o_ref[...] = jnp.concatenate([acc_ref[h] for h in range(N_HEADS)], axis=0).T.astype(BF16)


def _sb_prompt(qkv, vt, batch, seq):
    t = ATT_TILE
    nt = seq // t
    return pl.pallas_call(
        _sb_prompt_kernel,
        out_shape=jax.ShapeDtypeStruct((batch * seq, HEADS_WIDTH), BF16),
        grid=(batch, nt),
        in_specs=[pl.BlockSpec((t, HEADS_WIDTH), lambda b, i: (b * nt + i, 6)),
                  pl.BlockSpec((seq, HEADS_WIDTH), lambda b, i: (b, 7)),
                  pl.BlockSpec((None, HEADS_WIDTH, seq), lambda b, i: (b, 0, 0))],
        out_specs=pl.BlockSpec((t, HEADS_WIDTH), lambda b, i: (b * nt + i, 0)),
        scratch_shapes=[pltpu.VMEM((N_HEADS, HEADS_WIDTH, t), BF16),
                        pltpu.VMEM((SB_SUB, SB_SUB), BF16),
                        pltpu.VMEM((N_HEADS, 1, t), F32),
                        pltpu.VMEM((N_HEADS, HEAD_DIM, t), F32)],
        compiler_params=_params("parallel", "arbitrary"),
        name="sb_prompt",
    )(qkv, qkv, vt)


def _band_prompt_kernel(q_ref, kp_ref, kc_ref, vtp_ref, vtc_ref, bias_ref, o_ref, qt_ref):
    tq = q_ref.shape[0]
    qi = pl.program_id(1)
    _store_head_masked_qt(q_ref, qt_ref)
    k = jnp.concatenate([kp_ref[...], kc_ref[...]], axis=0)
    vt = jnp.concatenate([vtp_ref[...], vtc_ref[...]], axis=1)
    key = lax.broadcasted_iota(jnp.int32, (2 * tq, tq), 0)
    qry_chunk = lax.broadcasted_iota(jnp.int32, (2 * tq, tq), 1) // CHUNK
    key_chunk = key // CHUNK - tq // CHUNK
    valid = (key_chunk <= qry_chunk) & (key_chunk >= qry_chunk - BAND_CHUNKS)
    valid = valid & ((key >= tq) | (qi > 0))
    outs = []
    for h in range(N_HEADS):
        st = _dot(k, qt_ref[h]) + bias_ref[h]
        st = jnp.where(valid, st, NEG_INF)
        p = jnp.exp(st - jnp.max(st, axis=0, keepdims=True))
        denom = jnp.sum(p, axis=0, keepdims=True)
        outs.append(_dot(vt[h * HEAD_DIM:(h + 1) * HEAD_DIM, :], p.astype(BF16)) / denom)
    o_ref[...] = jnp.concatenate(outs, axis=0).T.astype(BF16)


def _band_prompt(qkv, vt, bias_tile, batch, seq):
    t = BAND_CHUNKS * CHUNK
    nt = seq // t
    prev = lambda i: jnp.maximum(i - 1, 0)
    return pl.pallas_call(
        _band_prompt_kernel,
        out_shape=jax.ShapeDtypeStruct((batch * seq, HEADS_WIDTH), BF16),
        grid=(batch, nt),
        in_specs=[pl.BlockSpec((t, HEADS_WIDTH), lambda b, i: (b * nt + i, 3)),
                  pl.BlockSpec((t, HEADS_WIDTH), lambda b, i: (b * nt + prev(i), 4)),
                  pl.BlockSpec((t, HEADS_WIDTH), lambda b, i: (b * nt + i, 4)),
                  pl.BlockSpec((None, HEADS_WIDTH, t), lambda b, i: (b, 0, prev(i))),
                  pl.BlockSpec((None, HEADS_WIDTH, t), lambda b, i: (b, 0, i)),
                  _const_spec(bias_tile.shape)],
        out_specs=pl.BlockSpec((t, HEADS_WIDTH), lambda b, i: (b * nt + i, 0)),
        scratch_shapes=[pltpu.VMEM((N_HEADS, HEADS_WIDTH, t), BF16)],
        compiler_params=_params("parallel", "parallel"),
        name="band_prompt",
    )(qkv, qkv, qkv, vt, vt, bias_tile)


def _rows_per_head(x, rows):
    return jnp.concatenate([jnp.broadcast_to(x[h:h + 1, :], (rows, x.shape[1])) for h in range(N_HEADS)],
                           axis=0)


def _unstack_heads(acc, rows):
    return _merge_heads([acc[h * rows:(h + 1) * rows, :] for h in range(N_HEADS)])


def _fox_sample_kernel(q_ref, kc_ref, vc_ref, kn_ref, vn_ref, cq_ref, ckc_ref, ckn_ref, o_ref,
                       qs_ref, m_ref, l_ref, acc_ref):
    t = q_ref.shape[0]
    ki = pl.program_id(1)

    @pl.when(ki == 0)
    def _():
        _init_softmax_state(m_ref, l_ref, acc_ref)
        _store_head_masked_q(q_ref, qs_ref, t)

    qs = qs_ref[...]
    cq = cq_ref[...]
    s = _dot(qs, kc_ref[...].astype(BF16)) + cq - _rows_per_head(ckc_ref[...], t)
    vc = vc_ref[...].astype(BF16)
    _online_softmax_update(s, lambda p: _dot_nt(p, vc), m_ref, l_ref, acc_ref, 0)

    @pl.when(ki == pl.num_programs(1) - 1)
    def _():
        s = _dot_nt(qs, kn_ref[...]) + cq - _rows_per_head(ckn_ref[:, :t], t)
        row = lax.broadcasted_iota(jnp.int32, s.shape, 0) % t
        col = lax.broadcasted_iota(jnp.int32, s.shape, 1)
        s = jnp.where(col <= row, s, NEG_INF)
        _online_softmax_update(s, lambda p: _dot(p, vn_ref[...]), m_ref, l_ref, acc_ref, 0)
        o_ref[...] = _unstack_heads(acc_ref[0] / l_ref[0], t).astype(BF16)


def _fox_sample(qkv, cache_kt, cache_vt, cq_rows, c_head, layer, dec_batch, t):
    past = cache_kt.shape[3]
    tk = min(SAMPLE_KEY_TILE, past)
    nk = past // tk
    return pl.pallas_call(
        _fox_sample_kernel,
        out_shape=jax.ShapeDtypeStruct((dec_batch * t, HEADS_WIDTH), BF16),
        grid=(dec_batch, nk),
        in_specs=[pl.BlockSpec((t, HEADS_WIDTH), lambda b, j: (b, 0)),
                  pl.BlockSpec((None, None, HEADS_WIDTH, tk), lambda b, j: (layer, b, 0, j)),
                  pl.BlockSpec((None, None, HEADS_WIDTH, tk), lambda b, j: (layer, b, 0, j)),
                  pl.BlockSpec((t, HEADS_WIDTH), lambda b, j: (b, 1)),
                  pl.BlockSpec((t, HEADS_WIDTH), lambda b, j: (b, 2)),
                  pl.BlockSpec((None, N_HEADS * t, 1), lambda b, j: (b, 0, 0)),
                  pl.BlockSpec((None, N_HEADS, tk), lambda b, j: (b, 0, j)),
                  pl.BlockSpec((None, N_HEADS, LANES), lambda b, j: (b, 0, past // LANES))],
        out_specs=pl.BlockSpec((t, HEADS_WIDTH), lambda b, j: (b, 0)),
        scratch_shapes=[pltpu.VMEM((N_HEADS * t, HEADS_WIDTH), BF16),
                        pltpu.VMEM((1, N_HEADS * t, 1), F32), pltpu.VMEM((1, N_HEADS * t, 1), F32),
                        pltpu.VMEM((1, N_HEADS * t, HEADS_WIDTH), F32)],
        compiler_params=_params("parallel", "arbitrary"),
        name="fox_sample",
    )(qkv, cache_kt, cache_vt, qkv, qkv, cq_rows, c_head, c_head)


def _mla_sample_kernel(q_ref, latc_ref, krc_ref, kn_ref, wv_ref, o_ref,
                       qa_ref, qr_ref, m_ref, l_ref, acc_ref, *, past):
    t = q_ref.shape[0]
    ki = pl.program_id(1)

    @pl.when(ki == 0)
    def _():
        _init_softmax_state(m_ref, l_ref, acc_ref)
        for h in range(N_HEADS):
            qa_ref[h * t:(h + 1) * t, :] = q_ref[:, h * Q_CAT:h * Q_CAT + MLA_KV_RANK]
            qr_ref[h * t:(h + 1) * t, :] = q_ref[:, h * Q_CAT + MLA_KV_RANK:
                                                 h * Q_CAT + MLA_KV_RANK + MLA_ROPE]

    lat = latc_ref[...].astype(BF16)
    s = _dot_nt(qa_ref[...], lat) + _dot(qr_ref[...], krc_ref[...].astype(BF16))
    _online_softmax_update(s, lambda p: _dot(p, lat), m_ref, l_ref, acc_ref, 0)

    @pl.when(ki == pl.num_programs(1) - 1)
    def _():
        lat_n = kn_ref[:, :MLA_KV_RANK]
        s = _dot_nt(qa_ref[...], lat_n) + _dot_nt(qr_ref[...],
                                                  kn_ref[:, MLA_KV_RANK:MLA_KV_RANK + MLA_ROPE])
        q_pos = past + lax.broadcasted_iota(jnp.int32, s.shape, 0) % t
        k_pos = past + lax.broadcasted_iota(jnp.int32, s.shape, 1)
        s = jnp.where(k_pos // CHUNK <= q_pos // CHUNK, s, NEG_INF)
        _online_softmax_update(s, lambda p: _dot(p, lat_n), m_ref, l_ref, acc_ref, 0)
        o = (acc_ref[0] / l_ref[0]).astype(BF16)
        out = None
        for h in range(N_HEADS):
            term = _dot(o[h * t:(h + 1) * t, :], wv_ref[h, :MLA_KV_RANK, :])
            out = term if out is None else out + term
        o_ref[...] = out.astype(BF16)


def _mla_sample(qcat, kcat, cache_lat, cache_krt, wv_pad, layer, dec_batch, t):
    past = cache_lat.shape[2]
    tk = min(SAMPLE_KEY_TILE, past)
    nk = past // tk
    return pl.pallas_call(
        functools.partial(_mla_sample_kernel, past=past),
        out_shape=jax.ShapeDtypeStruct((dec_batch * t, HEADS_WIDTH), BF16),
        grid=(dec_batch, nk),
        in_specs=[pl.BlockSpec((t, N_HEADS * Q_CAT), lambda b, j: (b, 0)),
                  pl.BlockSpec((None, None, tk, MLA_KV_RANK), lambda b, j: (layer, b, j, 0)),
                  pl.BlockSpec((None, None, MLA_ROPE, tk), lambda b, j: (layer, b, 0, j)),
                  pl.BlockSpec((t, Q_CAT), lambda b, j: (b, 0)),
                  _const_spec(wv_pad.shape)],
        out_specs=pl.BlockSpec((t, HEADS_WIDTH), lambda b, j: (b, 0)),
        scratch_shapes=[pltpu.VMEM((N_HEADS * t, MLA_KV_RANK), BF16),
                        pltpu.VMEM((N_HEADS * t, MLA_ROPE), BF16),
                        pltpu.VMEM((1, N_HEADS * t, 1), F32), pltpu.VMEM((1, N_HEADS * t, 1), F32),
                        pltpu.VMEM((1, N_HEADS * t, MLA_KV_RANK), F32)],
        compiler_params=_params("parallel", "arbitrary"),
        name="mla_sample",
    )(qcat, cache_lat, cache_krt, kcat, wv_pad)


def _band_sample_kernel(q_ref, kc_ref, vc_ref, kn_ref, vn_ref, biasc_ref, biasn_ref, o_ref, qs_ref,
                        *, past):
    t = q_ref.shape[0]
    n_band = kc_ref.shape[1]
    _store_head_masked_q(q_ref, qs_ref, t)
    qs = qs_ref[...]

    def masked(s, first_pos):
        q_chunk = (past + lax.broadcasted_iota(jnp.int32, s.shape, 0) % t) // CHUNK
        k_pos = first_pos + lax.broadcasted_iota(jnp.int32, s.shape, 1)
        k_chunk = k_pos // CHUNK
        valid = (k_pos >= 0) & (k_chunk <= q_chunk) & (k_chunk >= q_chunk - BAND_CHUNKS)
        return jnp.where(valid, s, NEG_INF)

    s_c = masked(_dot(qs, kc_ref[...].astype(BF16)) + biasc_ref[...], past - n_band)
    s_n = masked(_dot_nt(qs, kn_ref[...]) + biasn_ref[...], past)
    m = jnp.maximum(jnp.max(s_c, axis=1, keepdims=True), jnp.max(s_n, axis=1, keepdims=True))
    p_c = jnp.exp(s_c - m)
    p_n = jnp.exp(s_n - m)
    denom = jnp.sum(p_c, axis=1, keepdims=True) + jnp.sum(p_n, axis=1, keepdims=True)
    acc = _dot_nt(p_c.astype(BF16), vc_ref[...].astype(BF16)) + _dot(p_n.astype(BF16), vn_ref[...])
    o_ref[...] = _unstack_heads(acc / denom, t).astype(BF16)


def _band_sample(qkv, cache_kt, cache_vt, bias_cache, bias_new, layer, past, dec_batch, t):
    n_band = cache_kt.shape[3]
    return pl.pallas_call(
        functools.partial(_band_sample_kernel, past=past),
        out_shape=jax.ShapeDtypeStruct((dec_batch * t, HEADS_WIDTH), BF16),
        grid=(dec_batch,),
        in_specs=[pl.BlockSpec((t, HEADS_WIDTH), lambda b: (b, 3)),
                  pl.BlockSpec((None, None, HEADS_WIDTH, n_band), lambda b: (layer, b, 0, 0)),
                  pl.BlockSpec((None, None, HEADS_WIDTH, n_band), lambda b: (layer, b, 0, 0)),
                  pl.BlockSpec((t, HEADS_WIDTH), lambda b: (b, 4)),
                  pl.BlockSpec((t, HEADS_WIDTH), lambda b: (b, 5)),
                  _const_spec(bias_cache.shape), _const_spec(bias_new.shape)],
        out_specs=pl.BlockSpec((t, HEADS_WIDTH), lambda b: (b, 0)),
        scratch_shapes=[pltpu.VMEM((N_HEADS * t, HEADS_WIDTH), BF16)],
        compiler_params=_params("parallel"),
        name="band_sample",
    )(qkv, cache_kt, cache_vt, qkv, qkv, bias_cache, bias_new)


def _sb_sample_kernel(q_ref, kc_ref, vc_ref, kn_ref, vn_ref, o_ref, qs_ref, u_ref, rest_ref, acc_ref):
    t = q_ref.shape[0]
    past = kc_ref.shape[1]
    _store_head_masked_q(q_ref, qs_ref, t)
    u = _strict_upper_ones(SB_SUB)
    u_ref[...] = u
    qs = qs_ref[...]

    z = _dot_nt(qs, kn_ref[...])
    row = lax.broadcasted_iota(jnp.int32, z.shape, 0) % t
    col = lax.broadcasted_iota(jnp.int32, z.shape, 1)
    a, rest = _sb_block(z, u[:t, :t], jnp.zeros((z.shape[0], 1), F32), col < row)
    acc_ref[...] = _dot(a, vn_ref[...])
    rest_ref[...] = rest

    def more(carry):
        return jnp.logical_and(carry[0] >= 0, carry[1] > SB_DEAD)

    def older(carry):
        start = pl.multiple_of(carry[0] * SB_SUB, SB_SUB)
        k = kc_ref[:, pl.ds(start, SB_SUB)].astype(BF16)
        v = vc_ref[:, pl.ds(start, SB_SUB)].astype(BF16)
        a, rest = _sb_block(_dot(qs_ref[...], k), u_ref[...], rest_ref[...], None)
        acc_ref[...] += _dot_nt(a, v)
        rest_ref[...] = rest
        return carry[0] - 1, jnp.max(rest)

    lax.while_loop(more, older, (past // SB_SUB - 1, jnp.max(rest)))
    o_ref[...] = _unstack_heads(acc_ref[...], t).astype(BF16)


def _sb_sample(qkv, cache_kt, cache_vt, layer, dec_batch, t):
    past = cache_kt.shape[3]
    return pl.pallas_call(
        _sb_sample_kernel,
        out_shape=jax.ShapeDtypeStruct((dec_batch * t, HEADS_WIDTH), BF16),
        grid=(dec_batch,),
        in_specs=[pl.BlockSpec((t, HEADS_WIDTH), lambda b: (b, 6)),
                  pl.BlockSpec((None, None, HEADS_WIDTH, past), lambda b: (layer, b, 0, 0)),
                  pl.BlockSpec((None, None, HEADS_WIDTH, past), lambda b: (layer, b, 0, 0)),
                  pl.BlockSpec((t, HEADS_WIDTH), lambda b: (b, 7)),
                  pl.BlockSpec((t, HEADS_WIDTH), lambda b: (b, 8))],
        out_specs=pl.BlockSpec((t, HEADS_WIDTH), lambda b: (b, 0)),
        scratch_shapes=[pltpu.VMEM((N_HEADS * t, HEADS_WIDTH), BF16),
                        pltpu.VMEM((SB_SUB, SB_SUB), BF16),
                        pltpu.VMEM((N_HEADS * t, 1), F32),
                        pltpu.VMEM((N_HEADS * t, HEADS_WIDTH), F32)],
        compiler_params=_params("parallel"),
        name="sb_sample",
    )(qkv, cache_kt, cache_vt, qkv, qkv)


def _store_token_tiles(ref, value):
    for s in range(TILE_ROWS):
        ref[:, s, :] = value[:, s * LANES:(s + 1) * LANES]


def _load_token_tiles(ref):
    return jnp.concatenate([ref[:, s, :] for s in range(TILE_ROWS)], axis=1)


def _merge_kernel(x_ref, oa_ref, ob_ref, oc_ref, od_ref, wg_ref, bg_ref, wb_ref, wo_ref,
                  g_ref, b_ref, wr_ref, y_ref, logit_ref, *, alpha):
    x = x_ref[...]
    xb = x.astype(BF16)
    merged = None
    for n, o_ref in enumerate((oa_ref, ob_ref, oc_ref, od_ref)):
        gate = jax.nn.sigmoid(_dot(xb, wg_ref[n]) + bg_ref[n])
        term = gate * _dot(o_ref[...], wb_ref[n])
        merged = term if merged is None else merged + term
    h = alpha * x + _dot(merged.astype(BF16), wo_ref[...])
    y = _layer_norm(h, g_ref[...], b_ref[...])
    _store_token_tiles(y_ref, y)
    logit_ref[...] = _dot(y.astype(BF16), wr_ref[...])


def _merge(x, branches, wg, bg, wb, wo, g, b, wr, alpha):
    n = x.shape[0]
    tm = TOKEN_TILE
    row = lambda width: pl.BlockSpec((tm, width), lambda i: (i, 0))
    return pl.pallas_call(
        functools.partial(_merge_kernel, alpha=alpha),
        out_shape=[jax.ShapeDtypeStruct((n, TILE_ROWS, LANES), F32), jax.ShapeDtypeStruct((n, LANES), F32)],
        grid=(n // tm,),
        in_specs=[row(D_MODEL)] + [row(HEADS_WIDTH)] * 4
                 + [_const_spec(a.shape) for a in (wg, bg, wb, wo, g, b, wr)],
        out_specs=[pl.BlockSpec((tm, TILE_ROWS, LANES), lambda i: (i, 0, 0)), row(LANES)],
        compiler_params=_params("parallel"),
        name="merge",
    )(x, *branches, wg, bg, wb, wo, g, b, wr)


def _route_kernel(logit_ref, bias_ref, onehot_ref, w_ref):
    scores = jax.nn.sigmoid(logit_ref[...])
    sel = scores + bias_ref[...]
    rows = [sel[e:e + 1, :] for e in range(N_EXPERTS)]
    best = None
    best_group = None
    for g in range(N_GROUPS):
        m = rows[g * EXPERTS_PER_GROUP:(g + 1) * EXPERTS_PER_GROUP]
        top2 = None
        for i, j in GROUP_PAIRS:
            pair = m[i] + m[j]
            top2 = pair if top2 is None else jnp.maximum(top2, pair)
        if best is None:
            best, best_group = top2, jnp.zeros(top2.shape, jnp.int32)
        else:
            better = top2 > best
            best = jnp.where(better, top2, best)
            best_group = jnp.where(better, g, best_group)
    chosen = []
    weights = []
    total = None
    for e in range(N_EXPERTS):
        g, i = divmod(e, EXPERTS_PER_GROUP)
        rank = jnp.zeros(best.shape, jnp.int32)
        for j in range(EXPERTS_PER_GROUP):
            if j == i:
                continue
            other = rows[g * EXPERTS_PER_GROUP + j]
            ahead = (other >= rows[e]) if j < i else (other > rows[e])
            rank = rank + ahead.astype(jnp.int32)
        chosen.append((best_group == g) & (rank < 2))
        w = jnp.where(chosen[e], scores[e:e + 1, :], 0.0)
        weights.append(w)
        total = w if total is None else total + w
    onehot = []
    w_first = jnp.zeros(best.shape, F32)
    w_second = jnp.zeros(best.shape, F32)
    for g in range(N_GROUPS):
        for i, j in GROUP_PAIRS:
            first, second = g * EXPERTS_PER_GROUP + i, g * EXPERTS_PER_GROUP + j
            both = chosen[first] & chosen[second]
            onehot.append(jnp.where(both, 1.0, 0.0))
            w_first = jnp.where(both, weights[first] / total, w_first)
            w_second = jnp.where(both, weights[second] / total, w_second)
    pad = jnp.zeros((onehot_ref.shape[0] - N_BUCKETS, best.shape[1]), F32)
    onehot_ref[...] = jnp.concatenate(onehot + [pad], axis=0)
    w_ref[...] = jnp.concatenate([w_first, w_second], axis=0)


def _route(logits_t, bias_col):
    n = logits_t.shape[1]
    return pl.pallas_call(
        _route_kernel,
        out_shape=[jax.ShapeDtypeStruct((BUCKET_ROWS, n), F32), jax.ShapeDtypeStruct((2, n), F32)],
        compiler_params=pltpu.CompilerParams(vmem_limit_bytes=VMEM_LIMIT),
        name="route",
    )(logits_t, bias_col)


def _row_gather_kernel(idx_ref, table_ref, out_ref, sem, *, rows):
    base = pl.program_id(0) * rows

    def row_copy(r):
        return pltpu.make_async_copy(table_ref.at[idx_ref[base + r]], out_ref.at[base + r], sem)

    def start(r, carry):
        row_copy(r).start()
        return carry

    def wait(r, carry):
        row_copy(r).wait()
        return carry

    lax.fori_loop(0, rows, start, 0, unroll=8)
    lax.fori_loop(0, rows, wait, 0, unroll=8)


def _row_gather(table, idx):
    n_out = idx.shape[0]
    rows = min(GATHER_ROWS, n_out)
    assert n_out % rows == 0
    return pl.pallas_call(
        functools.partial(_row_gather_kernel, rows=rows),
        out_shape=jax.ShapeDtypeStruct((n_out,) + table.shape[1:], table.dtype),
        grid_spec=pltpu.PrefetchScalarGridSpec(
            num_scalar_prefetch=1,
            grid=(n_out // rows,),
            in_specs=[pl.BlockSpec(memory_space=pl.ANY)],
            out_specs=pl.BlockSpec(memory_space=pl.ANY),
            scratch_shapes=[pltpu.SemaphoreType.DMA]),
        compiler_params=pltpu.CompilerParams(dimension_semantics=("arbitrary",)),
        name="row_gather",
    )(idx, table)


def _experts_kernel(ea_ref, eb_ref, used_ref, x_ref, w_ref, wga_ref, wua_ref, wda_ref,
                    wgb_ref, wub_ref, wdb_ref, g_ref, b_ref, y_ref, *, alpha):
    del ea_ref, eb_ref
    t = pl.program_id(0)

    @pl.when(t < used_ref[0])
    def _():
        x = _load_token_tiles(x_ref)
        xb = x.astype(BF16)

        def expert(wg_ref, wu_ref, wd_ref):
            hidden = jax.nn.silu(_dot(xb, wg_ref[...])) * _dot(xb, wu_ref[...])
            return _dot(hidden.astype(BF16), wd_ref[...])

        w = w_ref[...]
        y = w[:, 0:1] * expert(wga_ref, wua_ref, wda_ref) + w[:, 1:2] * expert(wgb_ref, wub_ref, wdb_ref)
        _store_token_tiles(y_ref, _layer_norm(alpha * x + y, g_ref[...], b_ref[...]))

    @pl.when(t >= used_ref[0])
    def _():
        y_ref[...] = jnp.zeros(y_ref.shape, F32)


def _experts(x_sorted, w_sorted, first_expert, second_expert, tiles_used, wg, wu, wd, g, b, alpha, tile):
    n_slots = x_sorted.shape[0]
    tiles = pl.BlockSpec((tile, TILE_ROWS, LANES), lambda t, ea, eb, nu: (t, 0, 0))
    up = lambda which: pl.BlockSpec((None, D_MODEL, D_EXPERT),
                                    lambda t, ea, eb, nu: ((ea, eb)[which][t], 0, 0))
    down = lambda which: pl.BlockSpec((None, D_EXPERT, D_MODEL),
                                      lambda t, ea, eb, nu: ((ea, eb)[which][t], 0, 0))
    const = lambda a: pl.BlockSpec(a.shape, lambda t, ea, eb, nu: (0,) * a.ndim)
    return pl.pallas_call(
        functools.partial(_experts_kernel, alpha=alpha),
        out_shape=jax.ShapeDtypeStruct((n_slots, TILE_ROWS, LANES), F32),
        grid_spec=pltpu.PrefetchScalarGridSpec(
            num_scalar_prefetch=3,
            grid=(n_slots // tile,),
            in_specs=[tiles, pl.BlockSpec((tile, 2), lambda t, ea, eb, nu: (t, 0)),
                      up(0), up(0), down(0), up(1), up(1), down(1), const(g), const(b)],
            out_specs=tiles),
        compiler_params=_params("arbitrary"),
        name="experts",
    )(first_expert, second_expert, tiles_used, x_sorted, w_sorted, wg, wu, wd, wg, wu, wd, g, b)


def _moe(x_tiles, logits, rb, wg, wu, wd, g, b, alpha, tile):
    n = x_tiles.shape[0]
    onehot, w2 = _route(logits.T, rb)
    prefix = _cumsum_rows(onehot)
    counts = prefix[:, -1].astype(jnp.int32)
    padded = (counts + tile - 1) // tile * tile
    ends = jnp.cumsum(padded)
    starts = ends - padded
    rank = jnp.sum(onehot * prefix, axis=0).astype(jnp.int32) - 1
    pos = jnp.sum(onehot * starts.astype(F32)[:, None], axis=0).astype(jnp.int32) + rank
    n_slots = n + N_BUCKETS * tile
    slot_token = jnp.zeros((n_slots,), jnp.int32).at[pos].set(jnp.arange(n, dtype=jnp.int32),
                                                              unique_indices=True)
    w_sorted = jnp.zeros((n_slots, 2), F32).at[pos].set(w2.T, unique_indices=True)
    tile_start = jnp.arange(n_slots // tile, dtype=jnp.int32) * tile
    bucket = jnp.sum(tile_start[:, None] >= ends[None, :N_BUCKETS], axis=1)
    bucket = jnp.minimum(bucket, N_BUCKETS - 1).astype(jnp.int32)
    group, pair = bucket // len(GROUP_PAIRS), bucket % len(GROUP_PAIRS)
    pair_first = jnp.asarray([p[0] for p in GROUP_PAIRS], jnp.int32)
    pair_second = jnp.asarray([p[1] for p in GROUP_PAIRS], jnp.int32)
    first_expert = group * EXPERTS_PER_GROUP + pair_first[pair]
    second_expert = group * EXPERTS_PER_GROUP + pair_second[pair]
    tiles_used = (ends[N_BUCKETS - 1] // tile).astype(jnp.int32)[None]

    x_sorted = _row_gather(x_tiles, slot_token)
    y_sorted = _experts(x_sorted, w_sorted, first_expert, second_expert, tiles_used,
                        wg, wu, wd, g, b, alpha, tile)
    return _row_gather(y_sorted, pos).reshape(n, D_MODEL)


IN_SIZES = (HEADS_WIDTH,) * 3 + (N_HEADS, MLA_Q_RANK, MLA_KV_RANK, MLA_ROPE) + (HEADS_WIDTH,) * 6


def _pad_cols(a, width):
    return jnp.pad(a, ((0, 0), (0, width - a.shape[1])))


def _prep_in_proj_weights(w_in, b_forget, q_norm, w_q_up, w_kv_up):
    offs = np.cumsum(IN_SIZES)[:-1].tolist()
    qa, ka, va, fa, cq, ckv, kr, qc, kc, vc, qd, kd, vd = jnp.split(w_in, offs, axis=1)
    w = jnp.concatenate([qa, ka, va, qc, kc, vc, qd, kd, vd, _pad_cols(cq, 256), ckv,
                         _pad_cols(kr, LANES), _pad_cols(fa, LANES)], axis=1).astype(BF16)
    bf = _pad_cols(b_forget[None, :], LANES)
    qn = _pad_cols(q_norm[None, :], 256)
    wq3 = w_q_up.reshape(MLA_Q_RANK, N_HEADS, MLA_NOPE + MLA_ROPE)
    wq = jnp.concatenate([wq3[:, :, :MLA_NOPE].reshape(MLA_Q_RANK, -1),
                          wq3[:, :, MLA_NOPE:].reshape(MLA_Q_RANK, -1)], axis=1)
    wq = jnp.pad(wq, ((0, 256 - MLA_Q_RANK), (0, 0))).astype(BF16)
    wkv = w_kv_up.reshape(MLA_KV_RANK, N_HEADS, MLA_NOPE + MLA_V)
    wcat = jnp.zeros((HEADS_WIDTH + N_HEADS * MLA_ROPE, N_HEADS * Q_CAT), F32)
    wv_pad = jnp.zeros((N_HEADS, Q_CAT, HEADS_WIDTH), F32)
    eye = jnp.eye(MLA_ROPE, dtype=F32)
    for h in range(N_HEADS):
        wcat = wcat.at[h * MLA_NOPE:(h + 1) * MLA_NOPE, h * Q_CAT:h * Q_CAT + MLA_KV_RANK].set(
            wkv[:, h, :MLA_NOPE].T)
        wcat = wcat.at[HEADS_WIDTH + h * MLA_ROPE:HEADS_WIDTH + (h + 1) * MLA_ROPE,
                       h * Q_CAT + MLA_KV_RANK:h * Q_CAT + MLA_KV_RANK + MLA_ROPE].set(eye)
        wv_pad = wv_pad.at[h, :MLA_KV_RANK, h * MLA_V:(h + 1) * MLA_V].set(wkv[:, h, MLA_NOPE:])
    return w, bf, qn, wq, wcat.astype(BF16), wv_pad.astype(BF16)


def _rope_tables(pos):
    half = MLA_ROPE // 2
    inv = ROPE_THETA ** (-jnp.arange(half, dtype=F32) / half)
    ang = pos.astype(F32)[:, None] * inv[None, :]
    cos = jnp.cos(ang)
    sin = jnp.sin(ang)
    reps = LANES // MLA_ROPE
    return (jnp.tile(jnp.concatenate([cos, cos], axis=1), (1, reps)),
            jnp.tile(jnp.concatenate([-sin, sin], axis=1), (1, reps)))


def _band_bias(rel_bias, q0, nq, k0, nk):
    period = nq + nk
    j = np.arange(period)
    d = np.where(j < nk, j, j - period)
    idx = np.clip(q0 - k0 - d, -REL_CLIP, REL_CLIP) + REL_CLIP
    u = rel_bias[:, idx]
    rows = jnp.tile(u, (1, nq))[:, :nq * (period - 1)].reshape(rel_bias.shape[0], nq, period - 1)
    return rows[:, :, :nk]


def _key_major(cache):
    l, b, p, h, d = cache.shape
    return jnp.transpose(cache, (0, 1, 3, 4, 2)).reshape(l, b, h * d, p)


def _from_key_major(state, n_heads):
    b, hd, t = state.shape
    return jnp.transpose(state.reshape(b, n_heads, hd // n_heads, t), (0, 3, 1, 2))


def kernel(x_prompt, x_sample, cache_fox_k, cache_fox_v, cache_fox_logf, cache_mla_latent, cache_mla_krope, cache_band_k, cache_band_v, cache_sb_k, cache_sb_v, w_in, b_forget, mla_q_norm, w_q_up, mla_kv_norm, w_kv_up, rel_bias, w_branch, w_gate, b_gate, w_out, ln1_g, ln1_b, w_router, router_bias, w_e_gate, w_e_up, w_e_down, ln2_g, ln2_b):
    batch, seq, _ = x_prompt.shape
    dec_batch, t_new, _ = x_sample.shape
    depth = w_in.shape[0]
    past = cache_fox_k.shape[2]
    n_band = cache_band_k.shape[2]
    n_p = batch * seq
    n_s = dec_batch * t_new
    alpha = (2 * depth) ** 0.25
    band_tile = BAND_CHUNKS * CHUNK
    assert seq % ATT_TILE == 0 and seq % band_tile == 0 and seq % TOKEN_TILE == 0
    assert n_p % MOE_TILE == 0 and n_s % MOE_TILE_SAMPLE == 0 and n_s % TOKEN_TILE == 0
    assert past % LANES == 0 and t_new <= LANES and t_new % 16 == 0 and ATT_TILE % SB_SUB == 0
    assert past % min(SAMPLE_KEY_TILE, past) == 0 and past % SB_SUB == 0

    xp = x_prompt.reshape(n_p, D_MODEL)
    xs = x_sample.reshape(n_s, D_MODEL)
    rope_p = _rope_tables(jnp.tile(jnp.arange(seq, dtype=jnp.int32), batch))
    rope_s = _rope_tables(jnp.tile(past + jnp.arange(t_new, dtype=jnp.int32), dec_batch))

    c_fox_k, c_fox_v = _key_major(cache_fox_k), _key_major(cache_fox_v)
    c_band_k, c_band_v = _key_major(cache_band_k), _key_major(cache_band_v)
    c_sb_k, c_sb_v = _key_major(cache_sb_k), _key_major(cache_sb_v)
    c_logf = jnp.swapaxes(cache_fox_logf, 2, 3).astype(F32)
    c_krope = jnp.swapaxes(cache_mla_krope, 2, 3)

    wr = _pad_cols(w_router, LANES).astype(BF16)
    rb = router_bias.astype(F32)[:, None]
    total = past + t_new
    padded = -(-total // LANES) * LANES

    prompt_states, sample_states = [], []
    for l in range(depth):
        w, bf, qn, wq, wcat, wv_pad = _prep_in_proj_weights(w_in[l], b_forget[l], mla_q_norm[l],
                                                             w_q_up[l], w_kv_up[l])
        proj_w = (w, bf, qn, wq, mla_kv_norm[l][None, :], wcat)
        (qkv_p, ka_p, va_p, kc_p, vc_p, kd_p, vd_p, logf_p, lat_p, kr_p, kcat_p, qcat_p,
         vat_p, vct_p, vdt_p, latt_p) = _in_proj(xp, *proj_w, *rope_p, batch=batch, seq=seq)
        (qkv_s, ka_s, va_s, kc_s, vc_s, kd_s, vd_s, logf_s, lat_s, kr_s, kcat_s, qcat_s) = _in_proj(
            xs, *proj_w, *rope_s)
        logf_s = logf_s[:, :N_HEADS].reshape(dec_batch, t_new, N_HEADS)
        kr_s = kr_s[:, :MLA_ROPE]

        c_p = _cumsum_rows(logf_p.reshape(batch * N_HEADS, seq)).reshape(batch, N_HEADS, seq)
        all_s = jnp.concatenate([c_logf[l], jnp.swapaxes(logf_s, 1, 2)], axis=2)
        all_s = jnp.pad(all_s, ((0, 0), (0, 0), (0, padded - total)))
        c_s = _cumsum_rows(all_s.reshape(dec_batch * N_HEADS, padded)).reshape(dec_batch, N_HEADS, padded)
        cq_s = c_s[:, :, past:past + t_new].reshape(dec_batch, N_HEADS * t_new, 1)

        bias_p = _band_bias(rel_bias[l], 0, band_tile, -band_tile, 2 * band_tile)
        bias_s = _band_bias(rel_bias[l], past, t_new, past - n_band, n_band + t_new)
        bias_s = bias_s.reshape(N_HEADS * t_new, n_band + t_new)

        branches_p = (_fox_prompt(qkv_p, vat_p, jnp.swapaxes(c_p, 1, 2), c_p, batch, seq),
                      _mla_prompt(qcat_p, kcat_p, latt_p, wv_pad, batch, seq),
                      _band_prompt(qkv_p, vct_p, jnp.swapaxes(bias_p, 1, 2), batch, seq),
                      _sb_prompt(qkv_p, vdt_p, batch, seq))
        branches_s = (_fox_sample(qkv_s, c_fox_k, c_fox_v, cq_s, c_s, l, dec_batch, t_new),
                      _mla_sample(qcat_s, kcat_s, cache_mla_latent, c_krope, wv_pad, l, dec_batch, t_new),
                      _band_sample(qkv_s, c_band_k, c_band_v, bias_s[:, :n_band], bias_s[:, n_band:], l,
                                   past, dec_batch, t_new),
                      _sb_sample(qkv_s, c_sb_k, c_sb_v, l, dec_batch, t_new))

        tail_w = (w_gate[l].astype(BF16), b_gate[l][:, None, :], w_branch[l].astype(BF16),
                  w_out[l].astype(BF16), ln1_g[l][None, :], ln1_b[l][None, :], wr, alpha)
        experts = (w_e_gate[l].astype(BF16), w_e_up[l].astype(BF16), w_e_down[l].astype(BF16),
                   ln2_g[l][None, :], ln2_b[l][None, :], alpha)
        x1_p, logits_p = _merge(xp, branches_p, *tail_w)
        x1_s, logits_s = _merge(xs, branches_s, *tail_w)
        xp = _moe(x1_p, logits_p[:, :N_EXPERTS], rb, *experts, MOE_TILE)
        xs = _moe(x1_s, logits_s[:, :N_EXPERTS], rb, *experts, MOE_TILE_SAMPLE)

        n_keep = min(band_tile, seq)
        prompt_states.append((
            _from_key_major(ka_p, N_HEADS), _from_key_major(va_p, N_HEADS), jnp.swapaxes(logf_p, 1, 2),
            lat_p.reshape(batch, seq, -1), jnp.swapaxes(kr_p, 1, 2),
            _from_key_major(kc_p[:, :, seq - n_keep:], N_HEADS),
            _from_key_major(vc_p[:, :, seq - n_keep:], N_HEADS),
            _from_key_major(kd_p, N_HEADS), _from_key_major(vd_p, N_HEADS)))
        heads = lambda a: a.reshape(dec_batch, t_new, N_HEADS, HEAD_DIM)
        sample_states.append((
            heads(ka_s), heads(va_s), logf_s, lat_s.reshape(dec_batch, t_new, -1),
            kr_s.reshape(dec_batch, t_new, -1), heads(kc_s), heads(vc_s), heads(kd_s), heads(vd_s)))

    stack = lambda states: [jnp.stack(z, axis=0) for z in zip(*states)]
    return (xp.reshape(batch, seq, D_MODEL), xs.reshape(dec_batch, t_new, D_MODEL),
            *stack(prompt_states), *stack(sample_states))
```

```python
import functools

import numpy as np
import jax
import jax.numpy as jnp
from jax import lax
from jax.experimental import pallas as pl
from jax.experimental.pallas import tpu as pltpu

F32 = jnp.float32
BF16 = jnp.bfloat16

D_MODEL = 1024
N_HEADS = 4
HEAD_DIM = 64
HEADS_WIDTH = N_HEADS * HEAD_DIM
CHUNK = 64
BAND_CHUNKS = 8
REL_CLIP = 128
MLA_NOPE = 64
MLA_ROPE = 32
MLA_V = 64
MLA_Q_RANK = 192
MLA_KV_RANK = 128
ROPE_THETA = 10000.0
N_BRANCH = 4
N_EXPERTS = 16
N_GROUPS = 4
EXPERTS_PER_GROUP = N_EXPERTS // N_GROUPS
D_EXPERT = 512
GROUP_PAIRS = tuple((i, j) for i in range(EXPERTS_PER_GROUP) for j in range(i + 1, EXPERTS_PER_GROUP))
N_BUCKETS = N_GROUPS * len(GROUP_PAIRS)
BUCKET_ROWS = 32
NEG_INF = -1e30
LANES = 128
TILE_ROWS = D_MODEL // LANES
VMEM_LIMIT = 56 * 1024 * 1024

QKV_WIDTH = 9 * HEADS_WIDTH
STATE_COLS = (1, 2, 4, 5, 7, 8)
VALUE_COLS = (2, 5, 8)
OFF_CQ = QKV_WIDTH
OFF_CKV = OFF_CQ + 256
OFF_KR = OFF_CKV + MLA_KV_RANK
OFF_FA = OFF_KR + LANES
PROJ_WIDTH = OFF_FA + LANES
Q_CAT = 256

TOKEN_TILE = 512
MOE_TILE = 256
MOE_TILE_SAMPLE = 128
ATT_TILE = 512
SB_SUB = 256
SAMPLE_KEY_TILE = 4096
SB_DEAD = -105.0


def _dot(a, b):
    return jnp.dot(a, b, preferred_element_type=F32)


def _dot_nt(a, b):
    return lax.dot_general(a, b, (((1,), (1,)), ((), ())), preferred_element_type=F32)


def _const_spec(shape):
    nd = len(shape)
    return pl.BlockSpec(shape, lambda *_: (0,) * nd, pipeline_mode=pl.Buffered(1))


def _params(*sem):
    return pltpu.CompilerParams(dimension_semantics=sem, vmem_limit_bytes=VMEM_LIMIT)


def _log_sigmoid(x):
    return jnp.minimum(x, 0.0) - jnp.log1p(jnp.exp(-jnp.abs(x)))


def _softplus(x):
    return jnp.maximum(x, 0.0) + jnp.log1p(jnp.exp(-jnp.abs(x)))


def _layer_norm(h, g, b):
    mu = jnp.mean(h, axis=-1, keepdims=True)
    d = h - mu
    var = jnp.mean(d * d, axis=-1, keepdims=True)
    return d * lax.rsqrt(var + 1e-5) * g + b


def _head_of_lane(shape):
    return lax.broadcasted_iota(jnp.int32, shape, len(shape) - 1) // HEAD_DIM


def _in_proj_kernel(x_ref, w_ref, bf_ref, qn_ref, wq_ref, kvn_ref, wcat_ref, cos_ref, sin_ref,
                    qkv_ref, ka_ref, va_ref, kc_ref, vc_ref, kd_ref, vd_ref,
                    logf_ref, lat_ref, kr_ref, kcat_ref, qcat_ref, *key_major_refs, key_major):
    xb = x_ref[...].astype(BF16)
    proj = _dot(xb, w_ref[...])
    qkv_ref[...] = proj[:, :QKV_WIDTH].astype(BF16)
    for ref, col in zip((ka_ref, va_ref, kc_ref, vc_ref, kd_ref, vd_ref), STATE_COLS):
        state = proj[:, col * HEADS_WIDTH:(col + 1) * HEADS_WIDTH]
        if key_major:
            state = state.T
            if col in VALUE_COLS:
                key_major_refs[VALUE_COLS.index(col)][...] = state.astype(BF16)
        ref[...] = state

    cos = cos_ref[...]
    sin = sin_ref[...]
    lane = lax.broadcasted_iota(jnp.int32, cos.shape, 1)
    first_half = (lane % MLA_ROPE) < (MLA_ROPE // 2)

    def rope(v):
        partner = jnp.where(first_half, pltpu.roll(v, LANES - MLA_ROPE // 2, 1),
                            pltpu.roll(v, MLA_ROPE // 2, 1))
        return v * cos + partner * sin

    cq = proj[:, OFF_CQ:OFF_CQ + 256]
    ms = jnp.sum(cq * cq, axis=-1, keepdims=True) * (1.0 / MLA_Q_RANK)
    cqn = cq * lax.rsqrt(ms + 1e-6) * qn_ref[...]
    qb = _dot(cqn.astype(BF16), wq_ref[...])
    q_rope = rope(qb[:, HEADS_WIDTH:])
    q_in = jnp.concatenate([qb[:, :HEADS_WIDTH], q_rope], axis=1) * ((MLA_NOPE + MLA_ROPE) ** -0.5)
    qcat_ref[...] = _dot(q_in.astype(BF16), wcat_ref[...]).astype(BF16)

    ckv = proj[:, OFF_CKV:OFF_CKV + MLA_KV_RANK]
    lat = ckv * lax.rsqrt(jnp.mean(ckv * ckv, axis=-1, keepdims=True) + 1e-6) * kvn_ref[...]
    lat_ref[...] = lat
    kr = rope(proj[:, OFF_KR:OFF_KR + LANES])
    kcat_ref[...] = jnp.concatenate([lat, kr], axis=1).astype(BF16)
    logf = _log_sigmoid(proj[:, OFF_FA:OFF_FA + LANES] + bf_ref[...])
    if key_major:
        kr_ref[...] = kr.T[:MLA_ROPE, :]
        logf_ref[...] = logf.T[:N_HEADS, :]
        key_major_refs[len(VALUE_COLS)][...] = lat.T.astype(BF16)
    else:
        kr_ref[...] = kr
        logf_ref[...] = logf


def _in_proj(x, w, bf, qn, wq, kvn, wcat, cos, sin, *, batch=None, seq=None):
    n = x.shape[0]
    tm = TOKEN_TILE
    key_major = batch is not None
    row = lambda width: pl.BlockSpec((tm, width), lambda i: (i, 0))
    if key_major:
        per = seq // tm
        tr = lambda height: pl.BlockSpec((None, height, tm), lambda i: (i // per, 0, i % per))
        tshape = lambda height: jax.ShapeDtypeStruct((batch, height, seq), F32)
        state_shapes = [tshape(HEADS_WIDTH)] * 6 + [tshape(N_HEADS), jax.ShapeDtypeStruct((n, LANES), F32),
                                                    tshape(MLA_ROPE)]
        state_specs = [tr(HEADS_WIDTH)] * 6 + [tr(N_HEADS), row(LANES), tr(MLA_ROPE)]
    else:
        state_shapes = ([jax.ShapeDtypeStruct((n, HEADS_WIDTH), F32)] * 6
                        + [jax.ShapeDtypeStruct((n, LANES), F32)] * 3)
        state_specs = [row(HEADS_WIDTH)] * 6 + [row(LANES)] * 3
    out_shape = ([jax.ShapeDtypeStruct((n, QKV_WIDTH), BF16)] + state_shapes
                 + [jax.ShapeDtypeStruct((n, 2 * LANES), BF16),
                    jax.ShapeDtypeStruct((n, N_HEADS * Q_CAT), BF16)])
    out_specs = [row(QKV_WIDTH)] + state_specs + [row(2 * LANES), row(N_HEADS * Q_CAT)]
    if key_major:
        out_shape += ([jax.ShapeDtypeStruct((batch, HEADS_WIDTH, seq), BF16)] * len(VALUE_COLS)
                      + [jax.ShapeDtypeStruct((batch, MLA_KV_RANK, seq), BF16)])
        out_specs += [tr(HEADS_WIDTH)] * len(VALUE_COLS) + [tr(MLA_KV_RANK)]
    return pl.pallas_call(
        functools.partial(_in_proj_kernel, key_major=key_major),
        out_shape=out_shape,
        grid=(n // tm,),
        in_specs=[row(D_MODEL), _const_spec(w.shape), _const_spec(bf.shape), _const_spec(qn.shape),
                  _const_spec(wq.shape), _const_spec(kvn.shape), _const_spec(wcat.shape),
                  row(LANES), row(LANES)],
        out_specs=out_specs,
        compiler_params=_params("parallel"),
        name="in_proj_prompt" if key_major else "in_proj_sample",
    )(x, w, bf, qn, wq, kvn, wcat, cos, sin)


def _cumsum_kernel(x_ref, o_ref):
    r, t = x_ref.shape
    row = lax.broadcasted_iota(jnp.int32, (LANES, LANES), 0)
    col = lax.broadcasted_iota(jnp.int32, (LANES, LANES), 1)
    tri = jnp.where(row <= col, 1.0, 0.0).astype(BF16)
    carry = jnp.zeros((r, 1), F32)
    for c in range(t // LANES):
        xc = x_ref[:, c * LANES:(c + 1) * LANES]
        hi = xc.astype(BF16)
        r1 = xc - hi.astype(F32)
        mid = r1.astype(BF16)
        lo = (r1 - mid.astype(F32)).astype(BF16)
        y = _dot(hi, tri) + _dot(mid, tri) + _dot(lo, tri) + carry
        o_ref[:, c * LANES:(c + 1) * LANES] = y
        carry = y[:, LANES - 1:LANES]


def _cumsum_rows(x):
    return pl.pallas_call(
        _cumsum_kernel,
        out_shape=jax.ShapeDtypeStruct(x.shape, F32),
        name="cumsum",
    )(x)


def _online_softmax_update(s, pv, m_ref, l_ref, acc_ref, h):
    m_prev = m_ref[h]
    m_new = jnp.maximum(m_prev, jnp.max(s, axis=1, keepdims=True))
    alpha = jnp.exp(m_prev - m_new)
    p = jnp.exp(s - m_new)
    l_ref[h] = alpha * l_ref[h] + jnp.sum(p, axis=1, keepdims=True)
    acc_ref[h] = alpha * acc_ref[h] + pv(p.astype(BF16))
    m_ref[h] = m_new


def _init_softmax_state(m_ref, l_ref, acc_ref):
    m_ref[...] = jnp.full(m_ref.shape, NEG_INF, F32)
    l_ref[...] = jnp.zeros(l_ref.shape, F32)
    acc_ref[...] = jnp.zeros(acc_ref.shape, F32)


def _store_head_masked_q(q_ref, qs_ref, rows):
    q = q_ref[...] * (HEAD_DIM ** -0.5)
    head = _head_of_lane(q.shape)
    for h in range(N_HEADS):
        qs_ref[h * rows:(h + 1) * rows, :] = jnp.where(head == h, q, jnp.zeros_like(q))


def _merge_heads(per_head):
    head = _head_of_lane(per_head[0].shape)
    out = jnp.zeros_like(per_head[0])
    for h in range(N_HEADS):
        out = jnp.where(head == h, per_head[h], out)
    return out


def _online_softmax_update_t(st, pv, m_ref, l_ref, acc_ref, h):
    m_prev = m_ref[h]
    m_new = jnp.maximum(m_prev, jnp.max(st, axis=0, keepdims=True))
    alpha = jnp.exp(m_prev - m_new)
    p = jnp.exp(st - m_new)
    l_ref[h] = alpha * l_ref[h] + jnp.sum(p, axis=0, keepdims=True)
    acc_ref[h] = alpha * acc_ref[h] + pv(p.astype(BF16))
    m_ref[h] = m_new


def _fox_prompt_kernel(q_ref, k_ref, vt_ref, cq_ref, ck_ref, o_ref, qt_ref, m_ref, l_ref, acc_ref):
    qi = pl.program_id(1)
    ki = pl.program_id(2)

    @pl.when(ki == 0)
    def _():
        _init_softmax_state(m_ref, l_ref, acc_ref)
        _store_head_masked_qt(q_ref, qt_ref)

    def step(diagonal):
        k = k_ref[...]
        vt = vt_ref[...]
        cq = cq_ref[...]
        ck = ck_ref[...]
        for h in range(N_HEADS):
            st = _dot(k, qt_ref[h]) + cq[h:h + 1, :] - ck[:, h:h + 1]
            if diagonal:
                key = lax.broadcasted_iota(jnp.int32, st.shape, 0)
                qry = lax.broadcasted_iota(jnp.int32, st.shape, 1)
                st = jnp.where(key <= qry, st, NEG_INF)
            vt_h = vt[h * HEAD_DIM:(h + 1) * HEAD_DIM, :]
            _online_softmax_update_t(st, lambda p: _dot(vt_h, p), m_ref, l_ref, acc_ref, h)

    pl.when(ki < qi)(lambda: step(False))
    pl.when(ki == qi)(lambda: step(True))

    @pl.when(ki == pl.num_programs(2) - 1)
    def _():
        ot = jnp.concatenate([acc_ref[h] / l_ref[h] for h in range(N_HEADS)], axis=0)
        o_ref[...] = ot.T.astype(BF16)


def _fox_prompt(qkv, vt, c_tok, c_head, batch, seq):
    t = ATT_TILE
    nt = seq // t
    return pl.pallas_call(
        _fox_prompt_kernel,
        out_shape=jax.ShapeDtypeStruct((batch * seq, HEADS_WIDTH), BF16),
        grid=(batch, nt, nt),
        in_specs=[pl.BlockSpec((t, HEADS_WIDTH), lambda b, i, j: (b * nt + i, 0)),
                  pl.BlockSpec((t, HEADS_WIDTH), lambda b, i, j: (b * nt + jnp.minimum(i, j), 1)),
                  pl.BlockSpec((None, HEADS_WIDTH, t), lambda b, i, j: (b, 0, jnp.minimum(i, j))),
                  pl.BlockSpec((None, N_HEADS, t), lambda b, i, j: (b, 0, i)),
                  pl.BlockSpec((None, t, N_HEADS), lambda b, i, j: (b, jnp.minimum(i, j), 0))],
        out_specs=pl.BlockSpec((t, HEADS_WIDTH), lambda b, i, j: (b * nt + i, 0)),
        scratch_shapes=[pltpu.VMEM((N_HEADS, HEADS_WIDTH, t), BF16),
                        pltpu.VMEM((N_HEADS, 1, t), F32), pltpu.VMEM((N_HEADS, 1, t), F32),
                        pltpu.VMEM((N_HEADS, HEAD_DIM, t), F32)],
        compiler_params=_params("parallel", "parallel", "arbitrary"),
        name="fox_prompt",
    )(qkv, qkv, vt, c_head, c_tok)


def _mla_prompt_kernel(q_ref, k_ref, latt_ref, wv_ref, o_ref, qt_ref, m_ref, l_ref, acc_ref):
    qi = pl.program_id(1)
    ki = pl.program_id(2)

    @pl.when(ki == 0)
    def _():
        _init_softmax_state(m_ref, l_ref, acc_ref)
        for h in range(N_HEADS):
            qt_ref[h] = q_ref[:, h * Q_CAT:(h + 1) * Q_CAT].astype(F32).T.astype(BF16)

    def step(diagonal):
        k = k_ref[...]
        latt = latt_ref[...]
        for h in range(N_HEADS):
            st = _dot(k, qt_ref[h])
            if diagonal:
                key = lax.broadcasted_iota(jnp.int32, st.shape, 0)
                qry = lax.broadcasted_iota(jnp.int32, st.shape, 1)
                st = jnp.where(key // CHUNK <= qry // CHUNK, st, NEG_INF)
            _online_softmax_update_t(st, lambda p: _dot(latt, p), m_ref, l_ref, acc_ref, h)

    pl.when(ki < qi)(lambda: step(False))
    pl.when(ki == qi)(lambda: step(True))

    @pl.when(ki == pl.num_programs(2) - 1)
    def _():
        out = None
        for h in range(N_HEADS):
            o_h = (acc_ref[h] / l_ref[h]).T.astype(BF16)
            term = _dot(o_h, wv_ref[h, :MLA_KV_RANK, :])
            out = term if out is None else out + term
        o_ref[...] = out.astype(BF16)


def _mla_prompt(qcat, kcat, latt, wv_pad, batch, seq):
    t = ATT_TILE
    nt = seq // t
    return pl.pallas_call(
        _mla_prompt_kernel,
        out_shape=jax.ShapeDtypeStruct((batch * seq, HEADS_WIDTH), BF16),
        grid=(batch, nt, nt),
        in_specs=[pl.BlockSpec((t, N_HEADS * Q_CAT), lambda b, i, j: (b * nt + i, 0)),
                  pl.BlockSpec((t, Q_CAT), lambda b, i, j: (b * nt + jnp.minimum(i, j), 0)),
                  pl.BlockSpec((None, MLA_KV_RANK, t), lambda b, i, j: (b, 0, jnp.minimum(i, j))),
                  _const_spec(wv_pad.shape)],
        out_specs=pl.BlockSpec((t, HEADS_WIDTH), lambda b, i, j: (b * nt + i, 0)),
        scratch_shapes=[pltpu.VMEM((N_HEADS, Q_CAT, t), BF16),
                        pltpu.VMEM((N_HEADS, 1, t), F32), pltpu.VMEM((N_HEADS, 1, t), F32),
                        pltpu.VMEM((N_HEADS, MLA_KV_RANK, t), F32)],
        compiler_params=_params("parallel", "parallel", "arbitrary"),
        name="mla_prompt",
    )(qcat, kcat, latt, wv_pad)


def _strict_upper_ones(n):
    row = lax.broadcasted_iota(jnp.int32, (n, n), 0)
    col = lax.broadcasted_iota(jnp.int32, (n, n), 1)
    return jnp.where(row > col, 1.0, 0.0).astype(BF16)


def _sb_block(z, u, rest, mask):
    sp = _softplus(z)
    log_1m = -sp
    if mask is not None:
        log_1m = jnp.where(mask, log_1m, 0.0)
    hi = log_1m.astype(BF16)
    lo = (log_1m - hi.astype(F32)).astype(BF16)
    later = _dot(hi, u) + _dot(lo, u) + rest
    a = jnp.exp(z - sp + later)
    if mask is not None:
        a = jnp.where(mask, a, 0.0)
    return a.astype(BF16), rest + jnp.sum(log_1m, axis=1, keepdims=True)


def _store_head_masked_qt(q_ref, qt_ref):
    qt = q_ref[...].astype(F32).T * (HEAD_DIM ** -0.5)
    head = lax.broadcasted_iota(jnp.int32, qt.shape, 0) // HEAD_DIM
    for h in range(N_HEADS):
        qt_ref[h] = jnp.where(head == h, qt, 0.0).astype(BF16)


def _sb_block_t(zt, u, rest, mask):
    sp = _softplus(zt)
    log_1m = -sp
    if mask is not None:
        log_1m = jnp.where(mask, log_1m, 0.0)
    hi = log_1m.astype(BF16)
    lo = (log_1m - hi.astype(F32)).astype(BF16)
    later = _dot(u, hi) + _dot(u, lo) + rest
    a = jnp.exp(zt - sp + later)
    if mask is not None:
        a = jnp.where(mask, a, 0.0)
    return a.astype(BF16), rest + jnp.sum(log_1m, axis=0, keepdims=True)


def _sb_prompt_kernel(q_ref, k_ref, vt_ref, o_ref, qt_ref, u_ref, rest_ref, acc_ref):
    tq = q_ref.shape[0]
    qi = pl.program_id(1)
    _store_head_masked_qt(q_ref, qt_ref)
    key_s = lax.broadcasted_iota(jnp.int32, (SB_SUB, SB_SUB), 0)
    key_j = lax.broadcasted_iota(jnp.int32, (SB_SUB, SB_SUB), 1)
    u_ref[...] = jnp.where(key_j > key_s, 1.0, 0.0).astype(BF16)
    rest_ref[...] = jnp.zeros(rest_ref.shape, F32)
    acc_ref[...] = jnp.zeros(acc_ref.shape, F32)

    def visit(start, first_key):
        k = k_ref[pl.ds(start, SB_SUB), :]
        vt = vt_ref[:, pl.ds(start, SB_SUB)]
        u = u_ref[...]
        live = None
        for h in range(N_HEADS):
            zt = _dot(k, qt_ref[h])
            mask = None
            if first_key is not None:
                key = lax.broadcasted_iota(jnp.int32, zt.shape, 0) + first_key
                qry = lax.broadcasted_iota(jnp.int32, zt.shape, 1)
                mask = key < qry
            a, rest = _sb_block_t(zt, u, rest_ref[h], mask)
            acc_ref[h] += _dot(vt[h * HEAD_DIM:(h + 1) * HEAD_DIM, :], a)
            rest_ref[h] = rest
            top = jnp.max(rest)
            live = top if live is None else jnp.maximum(live, top)
        return live

    live = None
    for sub in reversed(range(tq // SB_SUB)):
        live = visit(pl.multiple_of(qi * tq + sub * SB_SUB, SB_SUB), sub * SB_SUB)

    def more(carry):
        return jnp.logical_and(carry[0] >= 0, carry[1] > SB_DEAD)

    def older(carry):
        return carry[0] - 1, visit(pl.multiple_of(carry[0] * SB_SUB, SB_SUB), None)

    lax.while_loop(more, older, (qi * (tq // SB_SUB) - 1, live))
    o_ref[...] = jnp.concatenate([acc_ref[h] for h in range(N_HEADS)], axis=0).T.astype(BF16)


def _sb_prompt(qkv, vt, batch, seq):
    t = ATT_TILE
    nt = seq // t
    return pl.pallas_call(
        _sb_prompt_kernel,
        out_shape=jax.ShapeDtypeStruct((batch * seq, HEADS_WIDTH), BF16),
        grid=(batch, nt),
        in_specs=[pl.BlockSpec((t, HEADS_WIDTH), lambda b, i: (b * nt + i, 6)),
                  pl.BlockSpec((seq, HEADS_WIDTH), lambda b, i: (b, 7)),
                  pl.BlockSpec((None, HEADS_WIDTH, seq), lambda b, i: (b, 0, 0))],
        out_specs=pl.BlockSpec((t, HEADS_WIDTH), lambda b, i: (b * nt + i, 0)),
        scratch_shapes=[pltpu.VMEM((N_HEADS, HEADS_WIDTH, t), BF16),
                        pltpu.VMEM((SB_SUB, SB_SUB), BF16),
                        pltpu.VMEM((N_HEADS, 1, t), F32),
                        pltpu.VMEM((N_HEADS, HEAD_DIM, t), F32)],
        compiler_params=_params("parallel", "arbitrary"),
        name="sb_prompt",
    )(qkv, qkv, vt)


def _band_prompt_kernel(q_ref, kp_ref, kc_ref, vtp_ref, vtc_ref, bias_ref, o_ref, qt_ref):
    tq = q_ref.shape[0]
    qi = pl.program_id(1)
    _store_head_masked_qt(q_ref, qt_ref)
    k = jnp.concatenate([kp_ref[...], kc_ref[...]], axis=0)
    vt = jnp.concatenate([vtp_ref[...], vtc_ref[...]], axis=1)
    key = lax.broadcasted_iota(jnp.int32, (2 * tq, tq), 0)
    qry_chunk = lax.broadcasted_iota(jnp.int32, (2 * tq, tq), 1) // CHUNK
    key_chunk = key // CHUNK - tq // CHUNK
    valid = (key_chunk <= qry_chunk) & (key_chunk >= qry_chunk - BAND_CHUNKS)
    valid = valid & ((key >= tq) | (qi > 0))
    outs = []
    for h in range(N_HEADS):
        st = _dot(k, qt_ref[h]) + bias_ref[h]
        st = jnp.where(valid, st, NEG_INF)
        p = jnp.exp(st - jnp.max(st, axis=0, keepdims=True))
        denom = jnp.sum(p, axis=0, keepdims=True)
        outs.append(_dot(vt[h * HEAD_DIM:(h + 1) * HEAD_DIM, :], p.astype(BF16)) / denom)
    o_ref[...] = jnp.concatenate(outs, axis=0).T.astype(BF16)


def _band_prompt(qkv, vt, bias_tile, batch, seq):
    t = BAND_CHUNKS * CHUNK
    nt = seq // t
    prev = lambda i: jnp.maximum(i - 1, 0)
    return pl.pallas_call(
        _band_prompt_kernel,
        out_shape=jax.ShapeDtypeStruct((batch * seq, HEADS_WIDTH), BF16),
        grid=(batch, nt),
        in_specs=[pl.BlockSpec((t, HEADS_WIDTH), lambda b, i: (b * nt + i, 3)),
                  pl.BlockSpec((t, HEADS_WIDTH), lambda b, i: (b * nt + prev(i), 4)),
                  pl.BlockSpec((t, HEADS_WIDTH), lambda b, i: (b * nt + i, 4)),
                  pl.BlockSpec((None, HEADS_WIDTH, t), lambda b, i: (b, 0, prev(i))),
                  pl.BlockSpec((None, HEADS_WIDTH, t), lambda b, i: (b, 0, i)),
                  _const_spec(bias_tile.shape)],
        out_specs=pl.BlockSpec((t, HEADS_WIDTH), lambda b, i: (b * nt + i, 0)),
        scratch_shapes=[pltpu.VMEM((N_HEADS, HEADS_WIDTH, t), BF16)],
        compiler_params=_params("parallel", "parallel"),
        name="band_prompt",
    )(qkv, qkv, qkv, vt, vt, bias_tile)


def _rows_per_head(x, rows):
    return jnp.concatenate([jnp.broadcast_to(x[h:h + 1, :], (rows, x.shape[1])) for h in range(N_HEADS)],
                           axis=0)


def _unstack_heads(acc, rows):
    return _merge_heads([acc[h * rows:(h + 1) * rows, :] for h in range(N_HEADS)])


def _fox_sample_kernel(q_ref, kc_ref, vc_ref, kn_ref, vn_ref, cq_ref, ckc_ref, ckn_ref, o_ref,
                       qs_ref, m_ref, l_ref, acc_ref):
    t = q_ref.shape[0]
    ki = pl.program_id(1)

    @pl.when(ki == 0)
    def _():
        _init_softmax_state(m_ref, l_ref, acc_ref)
        _store_head_masked_q(q_ref, qs_ref, t)

    qs = qs_ref[...]
    cq = cq_ref[...]
    s = _dot(qs, kc_ref[...].astype(BF16)) + cq - _rows_per_head(ckc_ref[...], t)
    vc = vc_ref[...].astype(BF16)
    _online_softmax_update(s, lambda p: _dot_nt(p, vc), m_ref, l_ref, acc_ref, 0)

    @pl.when(ki == pl.num_programs(1) - 1)
    def _():
        s = _dot_nt(qs, kn_ref[...]) + cq - _rows_per_head(ckn_ref[:, :t], t)
        row = lax.broadcasted_iota(jnp.int32, s.shape, 0) % t
        col = lax.broadcasted_iota(jnp.int32, s.shape, 1)
        s = jnp.where(col <= row, s, NEG_INF)
        _online_softmax_update(s, lambda p: _dot(p, vn_ref[...]), m_ref, l_ref, acc_ref, 0)
        o_ref[...] = _unstack_heads(acc_ref[0] / l_ref[0], t).astype(BF16)


def _fox_sample(qkv, cache_kt, cache_vt, cq_rows, c_head, layer, dec_batch, t):
    past = cache_kt.shape[3]
    tk = min(SAMPLE_KEY_TILE, past)
    nk = past // tk
    return pl.pallas_call(
        _fox_sample_kernel,
        out_shape=jax.ShapeDtypeStruct((dec_batch * t, HEADS_WIDTH), BF16),
        grid=(dec_batch, nk),
        in_specs=[pl.BlockSpec((t, HEADS_WIDTH), lambda b, j: (b, 0)),
                  pl.BlockSpec((None, None, HEADS_WIDTH, tk), lambda b, j: (layer, b, 0, j)),
                  pl.BlockSpec((None, None, HEADS_WIDTH, tk), lambda b, j: (layer, b, 0, j)),
                  pl.BlockSpec((t, HEADS_WIDTH), lambda b, j: (b, 1)),
                  pl.BlockSpec((t, HEADS_WIDTH), lambda b, j: (b, 2)),
                  pl.BlockSpec((None, N_HEADS * t, 1), lambda b, j: (b, 0, 0)),
                  pl.BlockSpec((None, N_HEADS, tk), lambda b, j: (b, 0, j)),
                  pl.BlockSpec((None, N_HEADS, LANES), lambda b, j: (b, 0, past // LANES))],
        out_specs=pl.BlockSpec((t, HEADS_WIDTH), lambda b, j: (b, 0)),
        scratch_shapes=[pltpu.VMEM((N_HEADS * t, HEADS_WIDTH), BF16),
                        pltpu.VMEM((1, N_HEADS * t, 1), F32), pltpu.VMEM((1, N_HEADS * t, 1), F32),
                        pltpu.VMEM((1, N_HEADS * t, HEADS_WIDTH), F32)],
        compiler_params=_params("parallel", "arbitrary"),
        name="fox_sample",
    )(qkv, cache_kt, cache_vt, qkv, qkv, cq_rows, c_head, c_head)


def _mla_sample_kernel(q_ref, latc_ref, krc_ref, kn_ref, wv_ref, o_ref,
                       qa_ref, qr_ref, m_ref, l_ref, acc_ref, *, past):
    t = q_ref.shape[0]
    ki = pl.program_id(1)

    @pl.when(ki == 0)
    def _():
        _init_softmax_state(m_ref, l_ref, acc_ref)
        for h in range(N_HEADS):
            qa_ref[h * t:(h + 1) * t, :] = q_ref[:, h * Q_CAT:h * Q_CAT + MLA_KV_RANK]
            qr_ref[h * t:(h + 1) * t, :] = q_ref[:, h * Q_CAT + MLA_KV_RANK:
                                                 h * Q_CAT + MLA_KV_RANK + MLA_ROPE]

    lat = latc_ref[...].astype(BF16)
    s = _dot_nt(qa_ref[...], lat) + _dot(qr_ref[...], krc_ref[...].astype(BF16))
    _online_softmax_update(s, lambda p: _dot(p, lat), m_ref, l_ref, acc_ref, 0)

    @pl.when(ki == pl.num_programs(1) - 1)
    def _():
        lat_n = kn_ref[:, :MLA_KV_RANK]
        s = _dot_nt(qa_ref[...], lat_n) + _dot_nt(qr_ref[...],
                                                  kn_ref[:, MLA_KV_RANK:MLA_KV_RANK + MLA_ROPE])
        q_pos = past + lax.broadcasted_iota(jnp.int32, s.shape, 0) % t
        k_pos = past + lax.broadcasted_iota(jnp.int32, s.shape, 1)
        s = jnp.where(k_pos // CHUNK <= q_pos // CHUNK, s, NEG_INF)
        _online_softmax_update(s, lambda p: _dot(p, lat_n), m_ref, l_ref, acc_ref, 0)
        o = (acc_ref[0] / l_ref[0]).astype(BF16)
        out = None
        for h in range(N_HEADS):
            term = _dot(o[h * t:(h + 1) * t, :], wv_ref[h, :MLA_KV_RANK, :])
            out = term if out is None else out + term
        o_ref[...] = out.astype(BF16)


def _mla_sample(qcat, kcat, cache_lat, cache_krt, wv_pad, layer, dec_batch, t):
    past = cache_lat.shape[2]
    tk = min(SAMPLE_KEY_TILE, past)
    nk = past // tk
    return pl.pallas_call(
        functools.partial(_mla_sample_kernel, past=past),
        out_shape=jax.ShapeDtypeStruct((dec_batch * t, HEADS_WIDTH), BF16),
        grid=(dec_batch, nk),
        in_specs=[pl.BlockSpec((t, N_HEADS * Q_CAT), lambda b, j: (b, 0)),
                  pl.BlockSpec((None, None, tk, MLA_KV_RANK), lambda b, j: (layer, b, j, 0)),
                  pl.BlockSpec((None, None, MLA_ROPE, tk), lambda b, j: (layer, b, 0, j)),
                  pl.BlockSpec((t, Q_CAT), lambda b, j: (b, 0)),
                  _const_spec(wv_pad.shape)],
        out_specs=pl.BlockSpec((t, HEADS_WIDTH), lambda b, j: (b, 0)),
        scratch_shapes=[pltpu.VMEM((N_HEADS * t, MLA_KV_RANK), BF16),
                        pltpu.VMEM((N_HEADS * t, MLA_ROPE), BF16),
                        pltpu.VMEM((1, N_HEADS * t, 1), F32), pltpu.VMEM((1, N_HEADS * t, 1), F32),
                        pltpu.VMEM((1, N_HEADS * t, MLA_KV_RANK), F32)],
        compiler_params=_params("parallel", "arbitrary"),
        name="mla_sample",
    )(qcat, cache_lat, cache_krt, kcat, wv_pad)


def _band_sample_kernel(q_ref, kc_ref, vc_ref, kn_ref, vn_ref, biasc_ref, biasn_ref, o_ref, qs_ref,
                        *, past):
    t = q_ref.shape[0]
    n_band = kc_ref.shape[1]
    _store_head_masked_q(q_ref, qs_ref, t)
    qs = qs_ref[...]

    def masked(s, first_pos):
        q_chunk = (past + lax.broadcasted_iota(jnp.int32, s.shape, 0) % t) // CHUNK
        k_pos = first_pos + lax.broadcasted_iota(jnp.int32, s.shape, 1)
        k_chunk = k_pos // CHUNK
        valid = (k_pos >= 0) & (k_chunk <= q_chunk) & (k_chunk >= q_chunk - BAND_CHUNKS)
        return jnp.where(valid, s, NEG_INF)

    s_c = masked(_dot(qs, kc_ref[...].astype(BF16)) + biasc_ref[...], past - n_band)
    s_n = masked(_dot_nt(qs, kn_ref[...]) + biasn_ref[...], past)
    m = jnp.maximum(jnp.max(s_c, axis=1, keepdims=True), jnp.max(s_n, axis=1, keepdims=True))
    p_c = jnp.exp(s_c - m)
    p_n = jnp.exp(s_n - m)
    denom = jnp.sum(p_c, axis=1, keepdims=True) + jnp.sum(p_n, axis=1, keepdims=True)
    acc = _dot_nt(p_c.astype(BF16), vc_ref[...].astype(BF16)) + _dot(p_n.astype(BF16), vn_ref[...])
    o_ref[...] = _unstack_heads(acc / denom, t).astype(BF16)


def _band_sample(qkv, cache_kt, cache_vt, bias_cache, bias_new, layer, past, dec_batch, t):
    n_band = cache_kt.shape[3]
    return pl.pallas_call(
        functools.partial(_band_sample_kernel, past=past),
        out_shape=jax.ShapeDtypeStruct((dec_batch * t, HEADS_WIDTH), BF16),
        grid=(dec_batch,),
        in_specs=[pl.BlockSpec((t, HEADS_WIDTH), lambda b: (b, 3)),
                  pl.BlockSpec((None, None, HEADS_WIDTH, n_band), lambda b: (layer, b, 0, 0)),
                  pl.BlockSpec((None, None, HEADS_WIDTH, n_band), lambda b: (layer, b, 0, 0)),
                  pl.BlockSpec((t, HEADS_WIDTH), lambda b: (b, 4)),
                  pl.BlockSpec((t, HEADS_WIDTH), lambda b: (b, 5)),
                  _const_spec(bias_cache.shape), _const_spec(bias_new.shape)],
        out_specs=pl.BlockSpec((t, HEADS_WIDTH), lambda b: (b, 0)),
        scratch_shapes=[pltpu.VMEM((N_HEADS * t, HEADS_WIDTH), BF16)],
        compiler_params=_params("parallel"),
        name="band_sample",
    )(qkv, cache_kt, cache_vt, qkv, qkv, bias_cache, bias_new)


def _sb_sample_kernel(q_ref, kc_ref, vc_ref, kn_ref, vn_ref, o_ref, qs_ref, u_ref, rest_ref, acc_ref):
    t = q_ref.shape[0]
    past = kc_ref.shape[1]
    _store_head_masked_q(q_ref, qs_ref, t)
    u = _strict_upper_ones(SB_SUB)
    u_ref[...] = u
    qs = qs_ref[...]

    z = _dot_nt(qs, kn_ref[...])
    row = lax.broadcasted_iota(jnp.int32, z.shape, 0) % t
    col = lax.broadcasted_iota(jnp.int32, z.shape, 1)
    a, rest = _sb_block(z, u[:t, :t], jnp.zeros((z.shape[0], 1), F32), col < row)
    acc_ref[...] = _dot(a, vn_ref[...])
    rest_ref[...] = rest

    def more(carry):
        return jnp.logical_and(carry[0] >= 0, carry[1] > SB_DEAD)

    def older(carry):
        start = pl.multiple_of(carry[0] * SB_SUB, SB_SUB)
        k = kc_ref[:, pl.ds(start, SB_SUB)].astype(BF16)
        v = vc_ref[:, pl.ds(start, SB_SUB)].astype(BF16)
        a, rest = _sb_block(_dot(qs_ref[...], k), u_ref[...], rest_ref[...], None)
        acc_ref[...] += _dot_nt(a, v)
        rest_ref[...] = rest
        return carry[0] - 1, jnp.max(rest)

    lax.while_loop(more, older, (past // SB_SUB - 1, jnp.max(rest)))
    o_ref[...] = _unstack_heads(acc_ref[...], t).astype(BF16)


def _sb_sample(qkv, cache_kt, cache_vt, layer, dec_batch, t):
    past = cache_kt.shape[3]
    return pl.pallas_call(
        _sb_sample_kernel,
        out_shape=jax.ShapeDtypeStruct((dec_batch * t, HEADS_WIDTH), BF16),
        grid=(dec_batch,),
        in_specs=[pl.BlockSpec((t, HEADS_WIDTH), lambda b: (b, 6)),
                  pl.BlockSpec((None, None, HEADS_WIDTH, past), lambda b: (layer, b, 0, 0)),
                  pl.BlockSpec((None, None, HEADS_WIDTH, past), lambda b: (layer, b, 0, 0)),
                  pl.BlockSpec((t, HEADS_WIDTH), lambda b: (b, 7)),
                  pl.BlockSpec((t, HEADS_WIDTH), lambda b: (b, 8))],
        out_specs=pl.BlockSpec((t, HEADS_WIDTH), lambda b: (b, 0)),
        scratch_shapes=[pltpu.VMEM((N_HEADS * t, HEADS_WIDTH), BF16),
                        pltpu.VMEM((SB_SUB, SB_SUB), BF16),
                        pltpu.VMEM((N_HEADS * t, 1), F32),
                        pltpu.VMEM((N_HEADS * t, HEADS_WIDTH), F32)],
        compiler_params=_params("parallel"),
        name="sb_sample",
    )(qkv, cache_kt, cache_vt, qkv, qkv)


def _store_token_tiles(ref, value):
    for s in range(TILE_ROWS):
        ref[:, s, :] = value[:, s * LANES:(s + 1) * LANES]


def _load_token_tiles(ref):
    return jnp.concatenate([ref[:, s, :] for s in range(TILE_ROWS)], axis=1)


def _merge_kernel(x_ref, oa_ref, ob_ref, oc_ref, od_ref, wg_ref, bg_ref, wb_ref, wo_ref,
                  g_ref, b_ref, wr_ref, y_ref, logit_ref, *, alpha):
    x = x_ref[...]
    xb = x.astype(BF16)
    merged = None
    for n, o_ref in enumerate((oa_ref, ob_ref, oc_ref, od_ref)):
        gate = jax.nn.sigmoid(_dot(xb, wg_ref[n]) + bg_ref[n])
        term = gate * _dot(o_ref[...], wb_ref[n])
        merged = term if merged is None else merged + term
    h = alpha * x + _dot(merged.astype(BF16), wo_ref[...])
    y = _layer_norm(h, g_ref[...], b_ref[...])
    _store_token_tiles(y_ref, y)
    logit_ref[...] = _dot(y.astype(BF16), wr_ref[...])


def _merge(x, branches, wg, bg, wb, wo, g, b, wr, alpha):
    n = x.shape[0]
    tm = TOKEN_TILE
    row = lambda width: pl.BlockSpec((tm, width), lambda i: (i, 0))
    return pl.pallas_call(
        functools.partial(_merge_kernel, alpha=alpha),
        out_shape=[jax.ShapeDtypeStruct((n, TILE_ROWS, LANES), F32), jax.ShapeDtypeStruct((n, LANES), F32)],
        grid=(n // tm,),
        in_specs=[row(D_MODEL)] + [row(HEADS_WIDTH)] * 4
                 + [_const_spec(a.shape) for a in (wg, bg, wb, wo, g, b, wr)],
        out_specs=[pl.BlockSpec((tm, TILE_ROWS, LANES), lambda i: (i, 0, 0)), row(LANES)],
        compiler_params=_params("parallel"),
        name="merge",
    )(x, *branches, wg, bg, wb, wo, g, b, wr)


def _route_kernel(logit_ref, bias_ref, onehot_ref, w_ref):
    scores = jax.nn.sigmoid(logit_ref[...])
    sel = scores + bias_ref[...]
    rows = [sel[e:e + 1, :] for e in range(N_EXPERTS)]
    best = None
    best_group = None
    for g in range(N_GROUPS):
        m = rows[g * EXPERTS_PER_GROUP:(g + 1) * EXPERTS_PER_GROUP]
        top2 = None
        for i, j in GROUP_PAIRS:
            pair = m[i] + m[j]
            top2 = pair if top2 is None else jnp.maximum(top2, pair)
        if best is None:
            best, best_group = top2, jnp.zeros(top2.shape, jnp.int32)
        else:
            better = top2 > best
            best = jnp.where(better, top2, best)
            best_group = jnp.where(better, g, best_group)
    chosen = []
    weights = []
    total = None
    for e in range(N_EXPERTS):
        g, i = divmod(e, EXPERTS_PER_GROUP)
        rank = jnp.zeros(best.shape, jnp.int32)
        for j in range(EXPERTS_PER_GROUP):
            if j == i:
                continue
            other = rows[g * EXPERTS_PER_GROUP + j]
            ahead = (other >= rows[e]) if j < i else (other > rows[e])
            rank = rank + ahead.astype(jnp.int32)
        chosen.append((best_group == g) & (rank < 2))
        w = jnp.where(chosen[e], scores[e:e + 1, :], 0.0)
        weights.append(w)
        total = w if total is None else total + w
    onehot = []
    w_first = jnp.zeros(best.shape, F32)
    w_second = jnp.zeros(best.shape, F32)
    for g in range(N_GROUPS):
        for i, j in GROUP_PAIRS:
            first, second = g * EXPERTS_PER_GROUP + i, g * EXPERTS_PER_GROUP + j
            both = chosen[first] & chosen[second]
            onehot.append(jnp.where(both, 1.0, 0.0))
            w_first = jnp.where(both, weights[first] / total, w_first)
            w_second = jnp.where(both, weights[second] / total, w_second)
    pad = jnp.zeros((onehot_ref.shape[0] - N_BUCKETS, best.shape[1]), F32)
    onehot_ref[...] = jnp.concatenate(onehot + [pad], axis=0)
    w_ref[...] = jnp.concatenate([w_first, w_second], axis=0)


def _route(logits_t, bias_col):
    n = logits_t.shape[1]
    return pl.pallas_call(
        _route_kernel,
        out_shape=[jax.ShapeDtypeStruct((BUCKET_ROWS, n), F32), jax.ShapeDtypeStruct((2, n), F32)],
        compiler_params=pltpu.CompilerParams(vmem_limit_bytes=VMEM_LIMIT),
        name="route",
    )(logits_t, bias_col)


def _experts_kernel(tok_ref, live_ref, ea_ref, eb_ref, used_ref, x_hbm, w_ref, wga_ref, wua_ref, wda_ref,
                    wgb_ref, wub_ref, wdb_ref, g_ref, b_ref, out_hbm, xbuf, ybuf, sem_in, sem_out,
                    *, alpha, tile):
    del ea_ref, eb_ref
    t = pl.program_id(0)
    base = t * tile

    def row_in(r):
        return pltpu.make_async_copy(x_hbm.at[tok_ref[base + r]], xbuf.at[r], sem_in)

    def row_out(r):
        return pltpu.make_async_copy(ybuf.at[r], out_hbm.at[tok_ref[base + r]], sem_out)

    def each_row(action, only_live):
        def body(r, carry):
            if only_live:
                pl.when(live_ref[base + r] != 0)(lambda: action(r))
            else:
                action(r)
            return carry
        lax.fori_loop(0, tile, body, 0, unroll=8)

    @pl.when(t < used_ref[0])
    def _():
        each_row(lambda r: row_in(r).start(), False)
        each_row(lambda r: row_in(r).wait(), False)
        x = _load_token_tiles(xbuf)
        xb = x.astype(BF16)

        def expert(wg_ref, wu_ref, wd_ref):
            hidden = jax.nn.silu(_dot(xb, wg_ref[...])) * _dot(xb, wu_ref[...])
            return _dot(hidden.astype(BF16), wd_ref[...])

        w = w_ref[...]
        y = w[:, 0:1] * expert(wga_ref, wua_ref, wda_ref) + w[:, 1:2] * expert(wgb_ref, wub_ref, wdb_ref)
        _store_token_tiles(ybuf, _layer_norm(alpha * x + y, g_ref[...], b_ref[...]))
        each_row(lambda r: row_out(r).start(), True)
        each_row(lambda r: row_out(r).wait(), True)


def _experts(x_tiles, slot_token, slot_live, w_sorted, first_expert, second_expert, tiles_used,
             wg, wu, wd, g, b, alpha, tile):
    n_slots = slot_token.shape[0]
    idx = lambda fn: (lambda t, tok, live, ea, eb, nu: fn(t, ea, eb))
    up = lambda which: pl.BlockSpec((None, D_MODEL, D_EXPERT), idx(lambda t, ea, eb: ((ea, eb)[which][t], 0, 0)))
    down = lambda which: pl.BlockSpec((None, D_EXPERT, D_MODEL), idx(lambda t, ea, eb: ((ea, eb)[which][t], 0, 0)))
    const = lambda a: pl.BlockSpec(a.shape, idx(lambda t, ea, eb: (0,) * a.ndim))
    return pl.pallas_call(
        functools.partial(_experts_kernel, alpha=alpha, tile=tile),
        out_shape=jax.ShapeDtypeStruct(x_tiles.shape, F32),
        grid_spec=pltpu.PrefetchScalarGridSpec(
            num_scalar_prefetch=5,
            grid=(n_slots // tile,),
            in_specs=[pl.BlockSpec(memory_space=pl.ANY),
                      pl.BlockSpec((tile, 2), idx(lambda t, ea, eb: (t, 0))),
                      up(0), up(0), down(0), up(1), up(1), down(1), const(g), const(b)],
            out_specs=pl.BlockSpec(memory_space=pl.ANY),
            scratch_shapes=[pltpu.VMEM((tile, TILE_ROWS, LANES), F32), pltpu.VMEM((tile, TILE_ROWS, LANES), F32),
                            pltpu.SemaphoreType.DMA, pltpu.SemaphoreType.DMA]),
        compiler_params=_params("arbitrary"),
        name="experts",
    )(slot_token, slot_live, first_expert, second_expert, tiles_used, x_tiles, w_sorted,
      wg, wu, wd, wg, wu, wd, g, b)


def _moe(x_tiles, logits, rb, wg, wu, wd, g, b, alpha, tile):
    n = x_tiles.shape[0]
    onehot, w2 = _route(logits.T, rb)
    prefix = _cumsum_rows(onehot)
    counts = prefix[:, -1].astype(jnp.int32)
    padded = (counts + tile - 1) // tile * tile
    ends = jnp.cumsum(padded)
    starts = ends - padded
    rank = jnp.sum(onehot * prefix, axis=0).astype(jnp.int32) - 1
    pos = jnp.sum(onehot * starts.astype(F32)[:, None], axis=0).astype(jnp.int32) + rank
    n_slots = n + N_BUCKETS * tile
    slot_token = jnp.zeros((n_slots,), jnp.int32).at[pos].set(jnp.arange(n, dtype=jnp.int32),
                                                              unique_indices=True)
    slot_live = jnp.zeros((n_slots,), jnp.int32).at[pos].set(1, unique_indices=True)
    w_sorted = jnp.zeros((n_slots, 2), F32).at[pos].set(w2.T, unique_indices=True)
    tile_start = jnp.arange(n_slots // tile, dtype=jnp.int32) * tile
    bucket = jnp.sum(tile_start[:, None] >= ends[None, :N_BUCKETS], axis=1)
    bucket = jnp.minimum(bucket, N_BUCKETS - 1).astype(jnp.int32)
    group, pair = bucket // len(GROUP_PAIRS), bucket % len(GROUP_PAIRS)
    pair_first = jnp.asarray([p[0] for p in GROUP_PAIRS], jnp.int32)
    pair_second = jnp.asarray([p[1] for p in GROUP_PAIRS], jnp.int32)
    first_expert = group * EXPERTS_PER_GROUP + pair_first[pair]
    second_expert = group * EXPERTS_PER_GROUP + pair_second[pair]
    tiles_used = (ends[N_BUCKETS - 1] // tile).astype(jnp.int32)[None]

    out_tiles = _experts(x_tiles, slot_token, slot_live, w_sorted, first_expert, second_expert, tiles_used,
                         wg, wu, wd, g, b, alpha, tile)
    return out_tiles.reshape(n, D_MODEL)


IN_SIZES = (HEADS_WIDTH,) * 3 + (N_HEADS, MLA_Q_RANK, MLA_KV_RANK, MLA_ROPE) + (HEADS_WIDTH,) * 6


def _pad_cols(a, width):
    return jnp.pad(a, ((0, 0), (0, width - a.shape[1])))


def _prep_in_proj_weights(w_in, b_forget, q_norm, w_q_up, w_kv_up):
    offs = np.cumsum(IN_SIZES)[:-1].tolist()
    qa, ka, va, fa, cq, ckv, kr, qc, kc, vc, qd, kd, vd = jnp.split(w_in, offs, axis=1)
    w = jnp.concatenate([qa, ka, va, qc, kc, vc, qd, kd, vd, _pad_cols(cq, 256), ckv,
                         _pad_cols(kr, LANES), _pad_cols(fa, LANES)], axis=1).astype(BF16)
    bf = _pad_cols(b_forget[None, :], LANES)
    qn = _pad_cols(q_norm[None, :], 256)
    wq3 = w_q_up.reshape(MLA_Q_RANK, N_HEADS, MLA_NOPE + MLA_ROPE)
    wq = jnp.concatenate([wq3[:, :, :MLA_NOPE].reshape(MLA_Q_RANK, -1),
                          wq3[:, :, MLA_NOPE:].reshape(MLA_Q_RANK, -1)], axis=1)
    wq = jnp.pad(wq, ((0, 256 - MLA_Q_RANK), (0, 0))).astype(BF16)
    wkv = w_kv_up.reshape(MLA_KV_RANK, N_HEADS, MLA_NOPE + MLA_V)
    wcat = jnp.zeros((HEADS_WIDTH + N_HEADS * MLA_ROPE, N_HEADS * Q_CAT), F32)
    wv_pad = jnp.zeros((N_HEADS, Q_CAT, HEADS_WIDTH), F32)
    eye = jnp.eye(MLA_ROPE, dtype=F32)
    for h in range(N_HEADS):
        wcat = wcat.at[h * MLA_NOPE:(h + 1) * MLA_NOPE, h * Q_CAT:h * Q_CAT + MLA_KV_RANK].set(
            wkv[:, h, :MLA_NOPE].T)
        wcat = wcat.at[HEADS_WIDTH + h * MLA_ROPE:HEADS_WIDTH + (h + 1) * MLA_ROPE,
                       h * Q_CAT + MLA_KV_RANK:h * Q_CAT + MLA_KV_RANK + MLA_ROPE].set(eye)
        wv_pad = wv_pad.at[h, :MLA_KV_RANK, h * MLA_V:(h + 1) * MLA_V].set(wkv[:, h, MLA_NOPE:])
    return w, bf, qn, wq, wcat.astype(BF16), wv_pad.astype(BF16)


def _rope_tables(pos):
    half = MLA_ROPE // 2
    inv = ROPE_THETA ** (-jnp.arange(half, dtype=F32) / half)
    ang = pos.astype(F32)[:, None] * inv[None, :]
    cos = jnp.cos(ang)
    sin = jnp.sin(ang)
    reps = LANES // MLA_ROPE
    return (jnp.tile(jnp.concatenate([cos, cos], axis=1), (1, reps)),
            jnp.tile(jnp.concatenate([-sin, sin], axis=1), (1, reps)))


def _band_bias(rel_bias, q0, nq, k0, nk):
    period = nq + nk
    j = np.arange(period)
    d = np.where(j < nk, j, j - period)
    idx = np.clip(q0 - k0 - d, -REL_CLIP, REL_CLIP) + REL_CLIP
    u = rel_bias[:, idx]
    rows = jnp.tile(u, (1, nq))[:, :nq * (period - 1)].reshape(rel_bias.shape[0], nq, period - 1)
    return rows[:, :, :nk]


def _key_major(cache):
    l, b, p, h, d = cache.shape
    return jnp.transpose(cache, (0, 1, 3, 4, 2)).reshape(l, b, h * d, p)


def _from_key_major(state, n_heads):
    b, hd, t = state.shape
    return jnp.transpose(state.reshape(b, n_heads, hd // n_heads, t), (0, 3, 1, 2))


def kernel(x_prompt, x_sample, cache_fox_k, cache_fox_v, cache_fox_logf, cache_mla_latent, cache_mla_krope, cache_band_k, cache_band_v, cache_sb_k, cache_sb_v, w_in, b_forget, mla_q_norm, w_q_up, mla_kv_norm, w_kv_up, rel_bias, w_branch, w_gate, b_gate, w_out, ln1_g, ln1_b, w_router, router_bias, w_e_gate, w_e_up, w_e_down, ln2_g, ln2_b):
    batch, seq, _ = x_prompt.shape
    dec_batch, t_new, _ = x_sample.shape
    depth = w_in.shape[0]
    past = cache_fox_k.shape[2]
    n_band = cache_band_k.shape[2]
    n_p = batch * seq
    n_s = dec_batch * t_new
    alpha = (2 * depth) ** 0.25
    band_tile = BAND_CHUNKS * CHUNK
    assert seq % ATT_TILE == 0 and seq % band_tile == 0 and seq % TOKEN_TILE == 0
    assert n_p % MOE_TILE == 0 and n_s % MOE_TILE_SAMPLE == 0 and n_s % TOKEN_TILE == 0
    assert past % LANES == 0 and t_new <= LANES and t_new % 16 == 0 and ATT_TILE % SB_SUB == 0
    assert past % min(SAMPLE_KEY_TILE, past) == 0 and past % SB_SUB == 0

    xp = x_prompt.reshape(n_p, D_MODEL)
    xs = x_sample.reshape(n_s, D_MODEL)
    rope_p = _rope_tables(jnp.tile(jnp.arange(seq, dtype=jnp.int32), batch))
    rope_s = _rope_tables(jnp.tile(past + jnp.arange(t_new, dtype=jnp.int32), dec_batch))

    c_fox_k, c_fox_v = _key_major(cache_fox_k), _key_major(cache_fox_v)
    c_band_k, c_band_v = _key_major(cache_band_k), _key_major(cache_band_v)
    c_sb_k, c_sb_v = _key_major(cache_sb_k), _key_major(cache_sb_v)
    c_logf = jnp.swapaxes(cache_fox_logf, 2, 3).astype(F32)
    c_krope = jnp.swapaxes(cache_mla_krope, 2, 3)

    wr = _pad_cols(w_router, LANES).astype(BF16)
    rb = router_bias.astype(F32)[:, None]
    total = past + t_new
    padded = -(-total // LANES) * LANES

    prompt_states, sample_states = [], []
    for l in range(depth):
        w, bf, qn, wq, wcat, wv_pad = _prep_in_proj_weights(w_in[l], b_forget[l], mla_q_norm[l],
                                                             w_q_up[l], w_kv_up[l])
        proj_w = (w, bf, qn, wq, mla_kv_norm[l][None, :], wcat)
        (qkv_p, ka_p, va_p, kc_p, vc_p, kd_p, vd_p, logf_p, lat_p, kr_p, kcat_p, qcat_p,
         vat_p, vct_p, vdt_p, latt_p) = _in_proj(xp, *proj_w, *rope_p, batch=batch, seq=seq)
        (qkv_s, ka_s, va_s, kc_s, vc_s, kd_s, vd_s, logf_s, lat_s, kr_s, kcat_s, qcat_s) = _in_proj(
            xs, *proj_w, *rope_s)
        logf_s = logf_s[:, :N_HEADS].reshape(dec_batch, t_new, N_HEADS)
        kr_s = kr_s[:, :MLA_ROPE]

        c_p = _cumsum_rows(logf_p.reshape(batch * N_HEADS, seq)).reshape(batch, N_HEADS, seq)
        all_s = jnp.concatenate([c_logf[l], jnp.swapaxes(logf_s, 1, 2)], axis=2)
        all_s = jnp.pad(all_s, ((0, 0), (0, 0), (0, padded - total)))
        c_s = _cumsum_rows(all_s.reshape(dec_batch * N_HEADS, padded)).reshape(dec_batch, N_HEADS, padded)
        cq_s = c_s[:, :, past:past + t_new].reshape(dec_batch, N_HEADS * t_new, 1)

        bias_p = _band_bias(rel_bias[l], 0, band_tile, -band_tile, 2 * band_tile)
        bias_s = _band_bias(rel_bias[l], past, t_new, past - n_band, n_band + t_new)
        bias_s = bias_s.reshape(N_HEADS * t_new, n_band + t_new)

        branches_p = (_fox_prompt(qkv_p, vat_p, jnp.swapaxes(c_p, 1, 2), c_p, batch, seq),
                      _mla_prompt(qcat_p, kcat_p, latt_p, wv_pad, batch, seq),
                      _band_prompt(qkv_p, vct_p, jnp.swapaxes(bias_p, 1, 2), batch, seq),
                      _sb_prompt(qkv_p, vdt_p, batch, seq))
        branches_s = (_fox_sample(qkv_s, c_fox_k, c_fox_v, cq_s, c_s, l, dec_batch, t_new),
                      _mla_sample(qcat_s, kcat_s, cache_mla_latent, c_krope, wv_pad, l, dec_batch, t_new),
                      _band_sample(qkv_s, c_band_k, c_band_v, bias_s[:, :n_band], bias_s[:, n_band:], l,
                                   past, dec_batch, t_new),
                      _sb_sample(qkv_s, c_sb_k, c_sb_v, l, dec_batch, t_new))

        tail_w = (w_gate[l].astype(BF16), b_gate[l][:, None, :], w_branch[l].astype(BF16),
                  w_out[l].astype(BF16), ln1_g[l][None, :], ln1_b[l][None, :], wr, alpha)
        experts = (w_e_gate[l].astype(BF16), w_e_up[l].astype(BF16), w_e_down[l].astype(BF16),
                   ln2_g[l][None, :], ln2_b[l][None, :], alpha)
        x1_p, logits_p = _merge(xp, branches_p, *tail_w)
        x1_s, logits_s = _merge(xs, branches_s, *tail_w)
        xp = _moe(x1_p, logits_p[:, :N_EXPERTS], rb, *experts, MOE_TILE)
        xs = _moe(x1_s, logits_s[:, :N_EXPERTS], rb, *experts, MOE_TILE_SAMPLE)

        n_keep = min(band_tile, seq)
        prompt_states.append((
            _from_key_major(ka_p, N_HEADS), _from_key_major(va_p, N_HEADS), jnp.swapaxes(logf_p, 1, 2),
            lat_p.reshape(batch, seq, -1), jnp.swapaxes(kr_p, 1, 2),
            _from_key_major(kc_p[:, :, seq - n_keep:], N_HEADS),
            _from_key_major(vc_p[:, :, seq - n_keep:], N_HEADS),
            _from_key_major(kd_p, N_HEADS), _from_key_major(vd_p, N_HEADS)))
        heads = lambda a: a.reshape(dec_batch, t_new, N_HEADS, HEAD_DIM)
        sample_states.append((
            heads(ka_s), heads(va_s), logf_s, lat_s.reshape(dec_batch, t_new, -1),
            kr_s.reshape(dec_batch, t_new, -1), heads(kc_s), heads(vc_s), heads(kd_s), heads(vd_s)))

    stack = lambda states: [jnp.stack(z, axis=0) for z in zip(*states)]
    return (xp.reshape(batch, seq, D_MODEL), xs.reshape(dec_batch, t_new, D_MODEL),
            *stack(prompt_states), *stack(sample_states))
```

```python
import functools

import numpy as np
import jax
import jax.numpy as jnp
from jax import lax
from jax.experimental import pallas as pl
from jax.experimental.pallas import tpu as pltpu

F32 = jnp.float32
BF16 = jnp.bfloat16

D_MODEL = 1024
N_HEADS = 4
HEAD_DIM = 64
HEADS_WIDTH = N_HEADS * HEAD_DIM
CHUNK = 64
BAND_CHUNKS = 8
REL_CLIP = 128
MLA_NOPE = 64
MLA_ROPE = 32
MLA_V = 64
MLA_Q_RANK = 192
MLA_KV_RANK = 128
ROPE_THETA = 10000.0
N_BRANCH = 4
N_EXPERTS = 16
N_GROUPS = 4
EXPERTS_PER_GROUP = N_EXPERTS // N_GROUPS
D_EXPERT = 512
GROUP_PAIRS = tuple((i, j) for i in range(EXPERTS_PER_GROUP) for j in range(i + 1, EXPERTS_PER_GROUP))
N_BUCKETS = N_GROUPS * len(GROUP_PAIRS)
BUCKET_ROWS = 32
NEG_INF = -1e30
LANES = 128
TILE_ROWS = D_MODEL // LANES
VMEM_LIMIT = 56 * 1024 * 1024

QKV_WIDTH = 9 * HEADS_WIDTH
STATE_COLS = (1, 2, 4, 5, 7, 8)
VALUE_COLS = (2, 5, 8)
OFF_CQ = QKV_WIDTH
OFF_CKV = OFF_CQ + 256
OFF_KR = OFF_CKV + MLA_KV_RANK
OFF_FA = OFF_KR + LANES
PROJ_WIDTH = OFF_FA + LANES
Q_CAT = 256

TOKEN_TILE = 512
MOE_TILE = 256
MOE_TILE_SAMPLE = 128
ATT_TILE = 512
SB_SUB = 256
SAMPLE_KEY_TILE = 4096
SB_DEAD = -105.0


def _dot(a, b):
    return jnp.dot(a, b, preferred_element_type=F32)


def _dot_nt(a, b):
    return lax.dot_general(a, b, (((1,), (1,)), ((), ())), preferred_element_type=F32)


def _const_spec(shape):
    nd = len(shape)
    return pl.BlockSpec(shape, lambda *_: (0,) * nd, pipeline_mode=pl.Buffered(1))


def _params(*sem):
    return pltpu.CompilerParams(dimension_semantics=sem, vmem_limit_bytes=VMEM_LIMIT)


def _log_sigmoid(x):
    return jnp.minimum(x, 0.0) - jnp.log1p(jnp.exp(-jnp.abs(x)))


def _softplus(x):
    return jnp.maximum(x, 0.0) + jnp.log1p(jnp.exp(-jnp.abs(x)))


def _layer_norm(h, g, b):
    mu = jnp.mean(h, axis=-1, keepdims=True)
    d = h - mu
    var = jnp.mean(d * d, axis=-1, keepdims=True)
    return d * lax.rsqrt(var + 1e-5) * g + b


def _head_of_lane(shape):
    return lax.broadcasted_iota(jnp.int32, shape, len(shape) - 1) // HEAD_DIM


def _in_proj_kernel(x_ref, w_ref, bf_ref, qn_ref, wq_ref, kvn_ref, wcat_ref, cos_ref, sin_ref,
                    qkv_ref, ka_ref, va_ref, kc_ref, vc_ref, kd_ref, vd_ref,
                    logf_ref, lat_ref, kr_ref, kcat_ref, qcat_ref, *key_major_refs, key_major):
    xb = x_ref[...].astype(BF16)
    proj = _dot(xb, w_ref[...])
    qkv_ref[...] = proj[:, :QKV_WIDTH].astype(BF16)
    for ref, col in zip((ka_ref, va_ref, kc_ref, vc_ref, kd_ref, vd_ref), STATE_COLS):
        state = proj[:, col * HEADS_WIDTH:(col + 1) * HEADS_WIDTH]
        if key_major:
            state = state.T
            if col in VALUE_COLS:
                key_major_refs[VALUE_COLS.index(col)][...] = state.astype(BF16)
        ref[...] = state

    cos = cos_ref[...]
    sin = sin_ref[...]
    lane = lax.broadcasted_iota(jnp.int32, cos.shape, 1)
    first_half = (lane % MLA_ROPE) < (MLA_ROPE // 2)

    def rope(v):
        partner = jnp.where(first_half, pltpu.roll(v, LANES - MLA_ROPE // 2, 1),
                            pltpu.roll(v, MLA_ROPE // 2, 1))
        return v * cos + partner * sin

    cq = proj[:, OFF_CQ:OFF_CQ + 256]
    ms = jnp.sum(cq * cq, axis=-1, keepdims=True) * (1.0 / MLA_Q_RANK)
    cqn = cq * lax.rsqrt(ms + 1e-6) * qn_ref[...]
    qb = _dot(cqn.astype(BF16), wq_ref[...])
    q_rope = rope(qb[:, HEADS_WIDTH:])
    q_in = jnp.concatenate([qb[:, :HEADS_WIDTH], q_rope], axis=1) * ((MLA_NOPE + MLA_ROPE) ** -0.5)
    qcat_ref[...] = _dot(q_in.astype(BF16), wcat_ref[...]).astype(BF16)

    ckv = proj[:, OFF_CKV:OFF_CKV + MLA_KV_RANK]
    lat = ckv * lax.rsqrt(jnp.mean(ckv * ckv, axis=-1, keepdims=True) + 1e-6) * kvn_ref[...]
    lat_ref[...] = lat
    kr = rope(proj[:, OFF_KR:OFF_KR + LANES])
    kcat_ref[...] = jnp.concatenate([lat, kr], axis=1).astype(BF16)
    logf = _log_sigmoid(proj[:, OFF_FA:OFF_FA + LANES] + bf_ref[...])
    if key_major:
        kr_ref[...] = kr.T[:MLA_ROPE, :]
        logf_ref[...] = logf.T[:N_HEADS, :]
        key_major_refs[len(VALUE_COLS)][...] = lat.T.astype(BF16)
    else:
        kr_ref[...] = kr
        logf_ref[...] = logf


def _in_proj(x, w, bf, qn, wq, kvn, wcat, cos, sin, *, batch=None, seq=None):
    n = x.shape[0]
    tm = TOKEN_TILE
    key_major = batch is not None
    row = lambda width: pl.BlockSpec((tm, width), lambda i: (i, 0))
    if key_major:
        per = seq // tm
        tr = lambda height: pl.BlockSpec((None, height, tm), lambda i: (i // per, 0, i % per))
        tshape = lambda height: jax.ShapeDtypeStruct((batch, height, seq), F32)
        state_shapes = [tshape(HEADS_WIDTH)] * 6 + [tshape(N_HEADS), jax.ShapeDtypeStruct((n, LANES), F32),
                                                    tshape(MLA_ROPE)]
        state_specs = [tr(HEADS_WIDTH)] * 6 + [tr(N_HEADS), row(LANES), tr(MLA_ROPE)]
    else:
        state_shapes = ([jax.ShapeDtypeStruct((n, HEADS_WIDTH), F32)] * 6
                        + [jax.ShapeDtypeStruct((n, LANES), F32)] * 3)
        state_specs = [row(HEADS_WIDTH)] * 6 + [row(LANES)] * 3
    out_shape = ([jax.ShapeDtypeStruct((n, QKV_WIDTH), BF16)] + state_shapes
                 + [jax.ShapeDtypeStruct((n, 2 * LANES), BF16),
                    jax.ShapeDtypeStruct((n, N_HEADS * Q_CAT), BF16)])
    out_specs = [row(QKV_WIDTH)] + state_specs + [row(2 * LANES), row(N_HEADS * Q_CAT)]
    if key_major:
        out_shape += ([jax.ShapeDtypeStruct((batch, HEADS_WIDTH, seq), BF16)] * len(VALUE_COLS)
                      + [jax.ShapeDtypeStruct((batch, MLA_KV_RANK, seq), BF16)])
        out_specs += [tr(HEADS_WIDTH)] * len(VALUE_COLS) + [tr(MLA_KV_RANK)]
    return pl.pallas_call(
        functools.partial(_in_proj_kernel, key_major=key_major),
        out_shape=out_shape,
        grid=(n // tm,),
        in_specs=[row(D_MODEL), _const_spec(w.shape), _const_spec(bf.shape), _const_spec(qn.shape),
                  _const_spec(wq.shape), _const_spec(kvn.shape), _const_spec(wcat.shape),
                  row(LANES), row(LANES)],
        out_specs=out_specs,
        compiler_params=_params("parallel"),
        name="in_proj_prompt" if key_major else "in_proj_sample",
    )(x, w, bf, qn, wq, kvn, wcat, cos, sin)


def _cumsum_kernel(x_ref, o_ref):
    r, t = x_ref.shape
    row = lax.broadcasted_iota(jnp.int32, (LANES, LANES), 0)
    col = lax.broadcasted_iota(jnp.int32, (LANES, LANES), 1)
    tri = jnp.where(row <= col, 1.0, 0.0).astype(BF16)
    carry = jnp.zeros((r, 1), F32)
    for c in range(t // LANES):
        xc = x_ref[:, c * LANES:(c + 1) * LANES]
        hi = xc.astype(BF16)
        r1 = xc - hi.astype(F32)
        mid = r1.astype(BF16)
        lo = (r1 - mid.astype(F32)).astype(BF16)
        y = _dot(hi, tri) + _dot(mid, tri) + _dot(lo, tri) + carry
        o_ref[:, c * LANES:(c + 1) * LANES] = y
        carry = y[:, LANES - 1:LANES]


def _cumsum_rows(x):
    return pl.pallas_call(
        _cumsum_kernel,
        out_shape=jax.ShapeDtypeStruct(x.shape, F32),
        name="cumsum",
    )(x)


def _online_softmax_update(s, pv, m_ref, l_ref, acc_ref, h):
    m_prev = m_ref[h]
    m_new = jnp.maximum(m_prev, jnp.max(s, axis=1, keepdims=True))
    alpha = jnp.exp(m_prev - m_new)
    p = jnp.exp(s - m_new)
    l_ref[h] = alpha * l_ref[h] + jnp.sum(p, axis=1, keepdims=True)
    acc_ref[h] = alpha * acc_ref[h] + pv(p.astype(BF16))
    m_ref[h] = m_new


def _init_softmax_state(m_ref, l_ref, acc_ref):
    m_ref[...] = jnp.full(m_ref.shape, NEG_INF, F32)
    l_ref[...] = jnp.zeros(l_ref.shape, F32)
    acc_ref[...] = jnp.zeros(acc_ref.shape, F32)


def _store_head_masked_q(q_ref, qs_ref, rows):
    q = q_ref[...] * (HEAD_DIM ** -0.5)
    head = _head_of_lane(q.shape)
    for h in range(N_HEADS):
        qs_ref[h * rows:(h + 1) * rows, :] = jnp.where(head == h, q, jnp.zeros_like(q))


def _merge_heads(per_head):
    head = _head_of_lane(per_head[0].shape)
    out = jnp.zeros_like(per_head[0])
    for h in range(N_HEADS):
        out = jnp.where(head == h, per_head[h], out)
    return out


def _online_softmax_update_t(st, pv, m_ref, l_ref, acc_ref, h):
    m_prev = m_ref[h]
    m_new = jnp.maximum(m_prev, jnp.max(st, axis=0, keepdims=True))
    alpha = jnp.exp(m_prev - m_new)
    p = jnp.exp(st - m_new)
    l_ref[h] = alpha * l_ref[h] + jnp.sum(p, axis=0, keepdims=True)
    acc_ref[h] = alpha * acc_ref[h] + pv(p.astype(BF16))
    m_ref[h] = m_new


def _fox_prompt_kernel(q_ref, k_ref, vt_ref, cq_ref, ck_ref, o_ref, qt_ref, m_ref, l_ref, acc_ref):
    qi = pl.program_id(1)
    ki = pl.program_id(2)

    @pl.when(ki == 0)
    def _():
        _init_softmax_state(m_ref, l_ref, acc_ref)
        _store_head_masked_qt(q_ref, qt_ref)

    def step(diagonal):
        k = k_ref[...]
        vt = vt_ref[...]
        cq = cq_ref[...]
        ck = ck_ref[...]
        for h in range(N_HEADS):
            st = _dot(k, qt_ref[h]) + cq[h:h + 1, :] - ck[:, h:h + 1]
            if diagonal:
                key = lax.broadcasted_iota(jnp.int32, st.shape, 0)
                qry = lax.broadcasted_iota(jnp.int32, st.shape, 1)
                st = jnp.where(key <= qry, st, NEG_INF)
            vt_h = vt[h * HEAD_DIM:(h + 1) * HEAD_DIM, :]
            _online_softmax_update_t(st, lambda p: _dot(vt_h, p), m_ref, l_ref, acc_ref, h)

    pl.when(ki < qi)(lambda: step(False))
    pl.when(ki == qi)(lambda: step(True))

    @pl.when(ki == pl.num_programs(2) - 1)
    def _():
        ot = jnp.concatenate([acc_ref[h] / l_ref[h] for h in range(N_HEADS)], axis=0)
        o_ref[...] = ot.T.astype(BF16)


def _fox_prompt(qkv, vt, c_tok, c_head, batch, seq):
    t = ATT_TILE
    nt = seq // t
    return pl.pallas_call(
        _fox_prompt_kernel,
        out_shape=jax.ShapeDtypeStruct((batch * seq, HEADS_WIDTH), BF16),
        grid=(batch, nt, nt),
        in_specs=[pl.BlockSpec((t, HEADS_WIDTH), lambda b, i, j: (b * nt + i, 0)),
                  pl.BlockSpec((t, HEADS_WIDTH), lambda b, i, j: (b * nt + jnp.minimum(i, j), 1)),
                  pl.BlockSpec((None, HEADS_WIDTH, t), lambda b, i, j: (b, 0, jnp.minimum(i, j))),
                  pl.BlockSpec((None, N_HEADS, t), lambda b, i, j: (b, 0, i)),
                  pl.BlockSpec((None, t, N_HEADS), lambda b, i, j: (b, jnp.minimum(i, j), 0))],
        out_specs=pl.BlockSpec((t, HEADS_WIDTH), lambda b, i, j: (b * nt + i, 0)),
        scratch_shapes=[pltpu.VMEM((N_HEADS, HEADS_WIDTH, t), BF16),
                        pltpu.VMEM((N_HEADS, 1, t), F32), pltpu.VMEM((N_HEADS, 1, t), F32),
                        pltpu.VMEM((N_HEADS, HEAD_DIM, t), F32)],
        compiler_params=_params("parallel", "parallel", "arbitrary"),
        name="fox_prompt",
    )(qkv, qkv, vt, c_head, c_tok)


def _mla_prompt_kernel(q_ref, k_ref, latt_ref, wv_ref, o_ref, qt_ref, m_ref, l_ref, acc_ref):
    qi = pl.program_id(1)
    ki = pl.program_id(2)

    @pl.when(ki == 0)
    def _():
        _init_softmax_state(m_ref, l_ref, acc_ref)
        for h in range(N_HEADS):
            qt_ref[h] = q_ref[:, h * Q_CAT:(h + 1) * Q_CAT].astype(F32).T.astype(BF16)

    def step(diagonal):
        k = k_ref[...]
        latt = latt_ref[...]
        for h in range(N_HEADS):
            st = _dot(k, qt_ref[h])
            if diagonal:
                key = lax.broadcasted_iota(jnp.int32, st.shape, 0)
                qry = lax.broadcasted_iota(jnp.int32, st.shape, 1)
                st = jnp.where(key // CHUNK <= qry // CHUNK, st, NEG_INF)
            _online_softmax_update_t(st, lambda p: _dot(latt, p), m_ref, l_ref, acc_ref, h)

    pl.when(ki < qi)(lambda: step(False))
    pl.when(ki == qi)(lambda: step(True))

    @pl.when(ki == pl.num_programs(2) - 1)
    def _():
        out = None
        for h in range(N_HEADS):
            o_h = (acc_ref[h] / l_ref[h]).T.astype(BF16)
            term = _dot(o_h, wv_ref[h, :MLA_KV_RANK, :])
            out = term if out is None else out + term
        o_ref[...] = out.astype(BF16)


def _mla_prompt(qcat, kcat, latt, wv_pad, batch, seq):
    t = ATT_TILE
    nt = seq // t
    return pl.pallas_call(
        _mla_prompt_kernel,
        out_shape=jax.ShapeDtypeStruct((batch * seq, HEADS_WIDTH), BF16),
        grid=(batch, nt, nt),
        in_specs=[pl.BlockSpec((t, N_HEADS * Q_CAT), lambda b, i, j: (b * nt + i, 0)),
                  pl.BlockSpec((t, Q_CAT), lambda b, i, j: (b * nt + jnp.minimum(i, j), 0)),
                  pl.BlockSpec((None, MLA_KV_RANK, t), lambda b, i, j: (b, 0, jnp.minimum(i, j))),
                  _const_spec(wv_pad.shape)],
        out_specs=pl.BlockSpec((t, HEADS_WIDTH), lambda b, i, j: (b * nt + i, 0)),
        scratch_shapes=[pltpu.VMEM((N_HEADS, Q_CAT, t), BF16),
                        pltpu.VMEM((N_HEADS, 1, t), F32), pltpu.VMEM((N_HEADS, 1, t), F32),
                        pltpu.VMEM((N_HEADS, MLA_KV_RANK, t), F32)],
        compiler_params=_params("parallel", "parallel", "arbitrary"),
        name="mla_prompt",
    )(qcat, kcat, latt, wv_pad)


def _strict_upper_ones(n):
    row = lax.broadcasted_iota(jnp.int32, (n, n), 0)
    col = lax.broadcasted_iota(jnp.int32, (n, n), 1)
    return jnp.where(row > col, 1.0, 0.0).astype(BF16)


def _sb_block(z, u, rest, mask):
    sp = _softplus(z)
    log_1m = -sp
    if mask is not None:
        log_1m = jnp.where(mask, log_1m, 0.0)
    hi = log_1m.astype(BF16)
    lo = (log_1m - hi.astype(F32)).astype(BF16)
    later = _dot(hi, u) + _dot(lo, u) + rest
    a = jnp.exp(z - sp + later)
    if mask is not None:
        a = jnp.where(mask, a, 0.0)
    return a.astype(BF16), rest + jnp.sum(log_1m, axis=1, keepdims=True)


def _store_head_masked_qt(q_ref, qt_ref):
    qt = q_ref[...].astype(F32).T * (HEAD_DIM ** -0.5)
    head = lax.broadcasted_iota(jnp.int32, qt.shape, 0) // HEAD_DIM
    for h in range(N_HEADS):
        qt_ref[h] = jnp.where(head == h, qt, 0.0).astype(BF16)


def _sb_block_t(zt, u, rest, mask):
    sp = _softplus(zt)
    log_1m = -sp
    if mask is not None:
        log_1m = jnp.where(mask, log_1m, 0.0)
    hi = log_1m.astype(BF16)
    lo = (log_1m - hi.astype(F32)).astype(BF16)
    later = _dot(u, hi) + _dot(u, lo) + rest
    a = jnp.exp(zt - sp + later)
    if mask is not None:
        a = jnp.where(mask, a, 0.0)
    return a.astype(BF16), rest + jnp.sum(log_1m, axis=0, keepdims=True)


def _sb_prompt_kernel(q_ref, k_ref, vt_ref, o_ref, qt_ref, u_ref, rest_ref, acc_ref):
    tq = q_ref.shape[0]
    qi = pl.program_id(1)
    _store_head_masked_qt(q_ref, qt_ref)
    key_s = lax.broadcasted_iota(jnp.int32, (SB_SUB, SB_SUB), 0)
    key_j = lax.broadcasted_iota(jnp.int32, (SB_SUB, SB_SUB), 1)
    u_ref[...] = jnp.where(key_j > key_s, 1.0, 0.0).astype(BF16)
    rest_ref[...] = jnp.zeros(rest_ref.shape, F32)
    acc_ref[...] = jnp.zeros(acc_ref.shape, F32)

    def visit(start, first_key):
        k = k_ref[pl.ds(start, SB_SUB), :]
        vt = vt_ref[:, pl.ds(start, SB_SUB)]
        u = u_ref[...]
        live = None
        for h in range(N_HEADS):
            zt = _dot(k, qt_ref[h])
            mask = None
            if first_key is not None:
                key = lax.broadcasted_iota(jnp.int32, zt.shape, 0) + first_key
                qry = lax.broadcasted_iota(jnp.int32, zt.shape, 1)
                mask = key < qry
            a, rest = _sb_block_t(zt, u, rest_ref[h], mask)
            acc_ref[h] += _dot(vt[h * HEAD_DIM:(h + 1) * HEAD_DIM, :], a)
            rest_ref[h] = rest
            top = jnp.max(rest)
            live = top if live is None else jnp.maximum(live, top)
        return live

    live = None
    for sub in reversed(range(tq // SB_SUB)):
        live = visit(pl.multiple_of(qi * tq + sub * SB_SUB, SB_SUB), sub * SB_SUB)

    def more(carry):
        return jnp.logical_and(carry[0] >= 0, carry[1] > SB_DEAD)

    def older(carry):
        return carry[0] - 1, visit(pl.multiple_of(carry[0] * SB_SUB, SB_SUB), None)

    lax.while_loop(more, older, (qi * (tq // SB_SUB) - 1, live))
    o_ref[...] = jnp.concatenate([acc_ref[h] for h in range(N_HEADS)], axis=0).T.astype(BF16)


def _sb_prompt(qkv, vt, batch, seq):
    t = ATT_TILE
    nt = seq // t
    return pl.pallas_call(
        _sb_prompt_kernel,
        out_shape=jax.ShapeDtypeStruct((batch * seq, HEADS_WIDTH), BF16),
        grid=(batch, nt),
        in_specs=[pl.BlockSpec((t, HEADS_WIDTH), lambda b, i: (b * nt + i, 6)),
                  pl.BlockSpec((seq, HEADS_WIDTH), lambda b, i: (b, 7)),
                  pl.BlockSpec((None, HEADS_WIDTH, seq), lambda b, i: (b, 0, 0))],
        out_specs=pl.BlockSpec((t, HEADS_WIDTH), lambda b, i: (b * nt + i, 0)),
        scratch_shapes=[pltpu.VMEM((N_HEADS, HEADS_WIDTH, t), BF16),
                        pltpu.VMEM((SB_SUB, SB_SUB), BF16),
                        pltpu.VMEM((N_HEADS, 1, t), F32),
                        pltpu.VMEM((N_HEADS, HEAD_DIM, t), F32)],
        compiler_params=_params("parallel", "arbitrary"),
        name="sb_prompt",
    )(qkv, qkv, vt)


def _band_prompt_kernel(q_ref, kp_ref, kc_ref, vtp_ref, vtc_ref, bias_ref, o_ref, qt_ref):
    tq = q_ref.shape[0]
    qi = pl.program_id(1)
    _store_head_masked_qt(q_ref, qt_ref)
    k = jnp.concatenate([kp_ref[...], kc_ref[...]], axis=0)
    vt = jnp.concatenate([vtp_ref[...], vtc_ref[...]], axis=1)
    key = lax.broadcasted_iota(jnp.int32, (2 * tq, tq), 0)
    qry_chunk = lax.broadcasted_iota(jnp.int32, (2 * tq, tq), 1) // CHUNK
    key_chunk = key // CHUNK - tq // CHUNK
    valid = (key_chunk <= qry_chunk) & (key_chunk >= qry_chunk - BAND_CHUNKS)
    valid = valid & ((key >= tq) | (qi > 0))
    outs = []
    for h in range(N_HEADS):
        st = _dot(k, qt_ref[h]) + bias_ref[h]
        st = jnp.where(valid, st, NEG_INF)
        p = jnp.exp(st - jnp.max(st, axis=0, keepdims=True))
        denom = jnp.sum(p, axis=0, keepdims=True)
        outs.append(_dot(vt[h * HEAD_DIM:(h + 1) * HEAD_DIM, :], p.astype(BF16)) / denom)
    o_ref[...] = jnp.concatenate(outs, axis=0).T.astype(BF16)


def _band_prompt(qkv, vt, bias_tile, batch, seq):
    t = BAND_CHUNKS * CHUNK
    nt = seq // t
    prev = lambda i: jnp.maximum(i - 1, 0)
    return pl.pallas_call(
        _band_prompt_kernel,
        out_shape=jax.ShapeDtypeStruct((batch * seq, HEADS_WIDTH), BF16),
        grid=(batch, nt),
        in_specs=[pl.BlockSpec((t, HEADS_WIDTH), lambda b, i: (b * nt + i, 3)),
                  pl.BlockSpec((t, HEADS_WIDTH), lambda b, i: (b * nt + prev(i), 4)),
                  pl.BlockSpec((t, HEADS_WIDTH), lambda b, i: (b * nt + i, 4)),
                  pl.BlockSpec((None, HEADS_WIDTH, t), lambda b, i: (b, 0, prev(i))),
                  pl.BlockSpec((None, HEADS_WIDTH, t), lambda b, i: (b, 0, i)),
                  _const_spec(bias_tile.shape)],
        out_specs=pl.BlockSpec((t, HEADS_WIDTH), lambda b, i: (b * nt + i, 0)),
        scratch_shapes=[pltpu.VMEM((N_HEADS, HEADS_WIDTH, t), BF16)],
        compiler_params=_params("parallel", "parallel"),
        name="band_prompt",
    )(qkv, qkv, qkv, vt, vt, bias_tile)


def _rows_per_head(x, rows):
    return jnp.concatenate([jnp.broadcast_to(x[h:h + 1, :], (rows, x.shape[1])) for h in range(N_HEADS)],
                           axis=0)


def _unstack_heads(acc, rows):
    return _merge_heads([acc[h * rows:(h + 1) * rows, :] for h in range(N_HEADS)])


def _fox_sample_kernel(q_ref, kc_ref, vc_ref, kn_ref, vn_ref, cq_ref, ckc_ref, ckn_ref, o_ref,
                       qs_ref, m_ref, l_ref, acc_ref):
    t = q_ref.shape[0]
    ki = pl.program_id(1)

    @pl.when(ki == 0)
    def _():
        _init_softmax_state(m_ref, l_ref, acc_ref)
        _store_head_masked_q(q_ref, qs_ref, t)

    qs = qs_ref[...]
    cq = cq_ref[...]
    s = _dot(qs, kc_ref[...].astype(BF16)) + cq - _rows_per_head(ckc_ref[...], t)
    vc = vc_ref[...].astype(BF16)
    _online_softmax_update(s, lambda p: _dot_nt(p, vc), m_ref, l_ref, acc_ref, 0)

    @pl.when(ki == pl.num_programs(1) - 1)
    def _():
        s = _dot_nt(qs, kn_ref[...]) + cq - _rows_per_head(ckn_ref[:, :t], t)
        row = lax.broadcasted_iota(jnp.int32, s.shape, 0) % t
        col = lax.broadcasted_iota(jnp.int32, s.shape, 1)
        s = jnp.where(col <= row, s, NEG_INF)
        _online_softmax_update(s, lambda p: _dot(p, vn_ref[...]), m_ref, l_ref, acc_ref, 0)
        o_ref[...] = _unstack_heads(acc_ref[0] / l_ref[0], t).astype(BF16)


def _fox_sample(qkv, cache_kt, cache_vt, cq_rows, c_head, layer, dec_batch, t):
    past = cache_kt.shape[3]
    tk = min(SAMPLE_KEY_TILE, past)
    nk = past // tk
    return pl.pallas_call(
        _fox_sample_kernel,
        out_shape=jax.ShapeDtypeStruct((dec_batch * t, HEADS_WIDTH), BF16),
        grid=(dec_batch, nk),
        in_specs=[pl.BlockSpec((t, HEADS_WIDTH), lambda b, j: (b, 0)),
                  pl.BlockSpec((None, None, HEADS_WIDTH, tk), lambda b, j: (layer, b, 0, j)),
                  pl.BlockSpec((None, None, HEADS_WIDTH, tk), lambda b, j: (layer, b, 0, j)),
                  pl.BlockSpec((t, HEADS_WIDTH), lambda b, j: (b, 1)),
                  pl.BlockSpec((t, HEADS_WIDTH), lambda b, j: (b, 2)),
                  pl.BlockSpec((None, N_HEADS * t, 1), lambda b, j: (b, 0, 0)),
                  pl.BlockSpec((None, N_HEADS, tk), lambda b, j: (b, 0, j)),
                  pl.BlockSpec((None, N_HEADS, LANES), lambda b, j: (b, 0, past // LANES))],
        out_specs=pl.BlockSpec((t, HEADS_WIDTH), lambda b, j: (b, 0)),
        scratch_shapes=[pltpu.VMEM((N_HEADS * t, HEADS_WIDTH), BF16),
                        pltpu.VMEM((1, N_HEADS * t, 1), F32), pltpu.VMEM((1, N_HEADS * t, 1), F32),
                        pltpu.VMEM((1, N_HEADS * t, HEADS_WIDTH), F32)],
        compiler_params=_params("parallel", "arbitrary"),
        name="fox_sample",
    )(qkv, cache_kt, cache_vt, qkv, qkv, cq_rows, c_head, c_head)


def _mla_sample_kernel(q_ref, latc_ref, krc_ref, kn_ref, wv_ref, o_ref,
                       qa_ref, qr_ref, m_ref, l_ref, acc_ref, *, past):
    t = q_ref.shape[0]
    ki = pl.program_id(1)

    @pl.when(ki == 0)
    def _():
        _init_softmax_state(m_ref, l_ref, acc_ref)
        for h in range(N_HEADS):
            qa_ref[h * t:(h + 1) * t, :] = q_ref[:, h * Q_CAT:h * Q_CAT + MLA_KV_RANK]
            qr_ref[h * t:(h + 1) * t, :] = q_ref[:, h * Q_CAT + MLA_KV_RANK:
                                                 h * Q_CAT + MLA_KV_RANK + MLA_ROPE]

    lat = latc_ref[...].astype(BF16)
    s = _dot_nt(qa_ref[...], lat) + _dot(qr_ref[...], krc_ref[...].astype(BF16))
    _online_softmax_update(s, lambda p: _dot(p, lat), m_ref, l_ref, acc_ref, 0)

    @pl.when(ki == pl.num_programs(1) - 1)
    def _():
        lat_n = kn_ref[:, :MLA_KV_RANK]
        s = _dot_nt(qa_ref[...], lat_n) + _dot_nt(qr_ref[...],
                                                  kn_ref[:, MLA_KV_RANK:MLA_KV_RANK + MLA_ROPE])
        q_pos = past + lax.broadcasted_iota(jnp.int32, s.shape, 0) % t
        k_pos = past + lax.broadcasted_iota(jnp.int32, s.shape, 1)
        s = jnp.where(k_pos // CHUNK <= q_pos // CHUNK, s, NEG_INF)
        _online_softmax_update(s, lambda p: _dot(p, lat_n), m_ref, l_ref, acc_ref, 0)
        o = (acc_ref[0] / l_ref[0]).astype(BF16)
        out = None
        for h in range(N_HEADS):
            term = _dot(o[h * t:(h + 1) * t, :], wv_ref[h, :MLA_KV_RANK, :])
            out = term if out is None else out + term
        o_ref[...] = out.astype(BF16)


def _mla_sample(qcat, kcat, cache_lat, cache_krt, wv_pad, layer, dec_batch, t):
    past = cache_lat.shape[2]
    tk = min(SAMPLE_KEY_TILE, past)
    nk = past // tk
    return pl.pallas_call(
        functools.partial(_mla_sample_kernel, past=past),
        out_shape=jax.ShapeDtypeStruct((dec_batch * t, HEADS_WIDTH), BF16),
        grid=(dec_batch, nk),
        in_specs=[pl.BlockSpec((t, N_HEADS * Q_CAT), lambda b, j: (b, 0)),
                  pl.BlockSpec((None, None, tk, MLA_KV_RANK), lambda b, j: (layer, b, j, 0)),
                  pl.BlockSpec((None, None, MLA_ROPE, tk), lambda b, j: (layer, b, 0, j)),
                  pl.BlockSpec((t, Q_CAT), lambda b, j: (b, 0)),
                  _const_spec(wv_pad.shape)],
        out_specs=pl.BlockSpec((t, HEADS_WIDTH), lambda b, j: (b, 0)),
        scratch_shapes=[pltpu.VMEM((N_HEADS * t, MLA_KV_RANK), BF16),
                        pltpu.VMEM((N_HEADS * t, MLA_ROPE), BF16),
                        pltpu.VMEM((1, N_HEADS * t, 1), F32), pltpu.VMEM((1, N_HEADS * t, 1), F32),
                        pltpu.VMEM((1, N_HEADS * t, MLA_KV_RANK), F32)],
        compiler_params=_params("parallel", "arbitrary"),
        name="mla_sample",
    )(qcat, cache_lat, cache_krt, kcat, wv_pad)


def _band_sample_kernel(q_ref, kc_ref, vc_ref, kn_ref, vn_ref, biasc_ref, biasn_ref, o_ref, qs_ref,
                        *, past):
    t = q_ref.shape[0]
    n_band = kc_ref.shape[1]
    _store_head_masked_q(q_ref, qs_ref, t)
    qs = qs_ref[...]

    def masked(s, first_pos):
        q_chunk = (past + lax.broadcasted_iota(jnp.int32, s.shape, 0) % t) // CHUNK
        k_pos = first_pos + lax.broadcasted_iota(jnp.int32, s.shape, 1)
        k_chunk = k_pos // CHUNK
        valid = (k_pos >= 0) & (k_chunk <= q_chunk) & (k_chunk >= q_chunk - BAND_CHUNKS)
        return jnp.where(valid, s, NEG_INF)

    s_c = masked(_dot(qs, kc_ref[...].astype(BF16)) + biasc_ref[...], past - n_band)
    s_n = masked(_dot_nt(qs, kn_ref[...]) + biasn_ref[...], past)
    m = jnp.maximum(jnp.max(s_c, axis=1, keepdims=True), jnp.max(s_n, axis=1, keepdims=True))
    p_c = jnp.exp(s_c - m)
    p_n = jnp.exp(s_n - m)
    denom = jnp.sum(p_c, axis=1, keepdims=True) + jnp.sum(p_n, axis=1, keepdims=True)
    acc = _dot_nt(p_c.astype(BF16), vc_ref[...].astype(BF16)) + _dot(p_n.astype(BF16), vn_ref[...])
    o_ref[...] = _unstack_heads(acc / denom, t).astype(BF16)


def _band_sample(qkv, cache_kt, cache_vt, bias_cache, bias_new, layer, past, dec_batch, t):
    n_band = cache_kt.shape[3]
    return pl.pallas_call(
        functools.partial(_band_sample_kernel, past=past),
        out_shape=jax.ShapeDtypeStruct((dec_batch * t, HEADS_WIDTH), BF16),
        grid=(dec_batch,),
        in_specs=[pl.BlockSpec((t, HEADS_WIDTH), lambda b: (b, 3)),
                  pl.BlockSpec((None, None, HEADS_WIDTH, n_band), lambda b: (layer, b, 0, 0)),
                  pl.BlockSpec((None, None, HEADS_WIDTH, n_band), lambda b: (layer, b, 0, 0)),
                  pl.BlockSpec((t, HEADS_WIDTH), lambda b: (b, 4)),
                  pl.BlockSpec((t, HEADS_WIDTH), lambda b: (b, 5)),
                  _const_spec(bias_cache.shape), _const_spec(bias_new.shape)],
        out_specs=pl.BlockSpec((t, HEADS_WIDTH), lambda b: (b, 0)),
        scratch_shapes=[pltpu.VMEM((N_HEADS * t, HEADS_WIDTH), BF16)],
        compiler_params=_params("parallel"),
        name="band_sample",
    )(qkv, cache_kt, cache_vt, qkv, qkv, bias_cache, bias_new)


def _sb_sample_kernel(q_ref, kc_ref, vc_ref, kn_ref, vn_ref, o_ref, qs_ref, u_ref, rest_ref, acc_ref):
    t = q_ref.shape[0]
    past = kc_ref.shape[1]
    _store_head_masked_q(q_ref, qs_ref, t)
    u = _strict_upper_ones(SB_SUB)
    u_ref[...] = u
    qs = qs_ref[...]

    z = _dot_nt(qs, kn_ref[...])
    row = lax.broadcasted_iota(jnp.int32, z.shape, 0) % t
    col = lax.broadcasted_iota(jnp.int32, z.shape, 1)
    a, rest = _sb_block(z, u[:t, :t], jnp.zeros((z.shape[0], 1), F32), col < row)
    acc_ref[...] = _dot(a, vn_ref[...])
    rest_ref[...] = rest

    def more(carry):
        return jnp.logical_and(carry[0] >= 0, carry[1] > SB_DEAD)

    def older(carry):
        start = pl.multiple_of(carry[0] * SB_SUB, SB_SUB)
        k = kc_ref[:, pl.ds(start, SB_SUB)].astype(BF16)
        v = vc_ref[:, pl.ds(start, SB_SUB)].astype(BF16)
        a, rest = _sb_block(_dot(qs_ref[...], k), u_ref[...], rest_ref[...], None)
        acc_ref[...] += _dot_nt(a, v)
        rest_ref[...] = rest
        return carry[0] - 1, jnp.max(rest)

    lax.while_loop(more, older, (past // SB_SUB - 1, jnp.max(rest)))
    o_ref[...] = _unstack_heads(acc_ref[...], t).astype(BF16)


def _sb_sample(qkv, cache_kt, cache_vt, layer, dec_batch, t):
    past = cache_kt.shape[3]
    return pl.pallas_call(
        _sb_sample_kernel,
        out_shape=jax.ShapeDtypeStruct((dec_batch * t, HEADS_WIDTH), BF16),
        grid=(dec_batch,),
        in_specs=[pl.BlockSpec((t, HEADS_WIDTH), lambda b: (b, 6)),
                  pl.BlockSpec((None, None, HEADS_WIDTH, past), lambda b: (layer, b, 0, 0)),
                  pl.BlockSpec((None, None, HEADS_WIDTH, past), lambda b: (layer, b, 0, 0)),
                  pl.BlockSpec((t, HEADS_WIDTH), lambda b: (b, 7)),
                  pl.BlockSpec((t, HEADS_WIDTH), lambda b: (b, 8))],
        out_specs=pl.BlockSpec((t, HEADS_WIDTH), lambda b: (b, 0)),
        scratch_shapes=[pltpu.VMEM((N_HEADS * t, HEADS_WIDTH), BF16),
                        pltpu.VMEM((SB_SUB, SB_SUB), BF16),
                        pltpu.VMEM((N_HEADS * t, 1), F32),
                        pltpu.VMEM((N_HEADS * t, HEADS_WIDTH), F32)],
        compiler_params=_params("parallel"),
        name="sb_sample",
    )(qkv, cache_kt, cache_vt, qkv, qkv)


def _store_token_tiles(ref, value):
    for s in range(TILE_ROWS):
        ref[:, s, :] = value[:, s * LANES:(s + 1) * LANES]


def _load_token_tiles(ref):
    return jnp.concatenate([ref[:, s, :] for s in range(TILE_ROWS)], axis=1)


def _merge_kernel(x_ref, oa_ref, ob_ref, oc_ref, od_ref, wg_ref, bg_ref, wb_ref, wo_ref,
                  g_ref, b_ref, wr_ref, y_ref, logit_ref, *, alpha):
    x = x_ref[...]
    xb = x.astype(BF16)
    merged = None
    for n, o_ref in enumerate((oa_ref, ob_ref, oc_ref, od_ref)):
        gate = jax.nn.sigmoid(_dot(xb, wg_ref[n]) + bg_ref[n])
        term = gate * _dot(o_ref[...], wb_ref[n])
        merged = term if merged is None else merged + term
    h = alpha * x + _dot(merged.astype(BF16), wo_ref[...])
    y = _layer_norm(h, g_ref[...], b_ref[...])
    _store_token_tiles(y_ref, y)
    logit_ref[...] = _dot(y.astype(BF16), wr_ref[...])


def _merge(x, branches, wg, bg, wb, wo, g, b, wr, alpha):
    n = x.shape[0]
    tm = TOKEN_TILE
    row = lambda width: pl.BlockSpec((tm, width), lambda i: (i, 0))
    return pl.pallas_call(
        functools.partial(_merge_kernel, alpha=alpha),
        out_shape=[jax.ShapeDtypeStruct((n, TILE_ROWS, LANES), F32), jax.ShapeDtypeStruct((n, LANES), F32)],
        grid=(n // tm,),
        in_specs=[row(D_MODEL)] + [row(HEADS_WIDTH)] * 4
                 + [_const_spec(a.shape) for a in (wg, bg, wb, wo, g, b, wr)],
        out_specs=[pl.BlockSpec((tm, TILE_ROWS, LANES), lambda i: (i, 0, 0)), row(LANES)],
        compiler_params=_params("parallel"),
        name="merge",
    )(x, *branches, wg, bg, wb, wo, g, b, wr)


def _route_kernel(logit_ref, bias_ref, onehot_ref, w_ref):
    scores = jax.nn.sigmoid(logit_ref[...])
    sel = scores + bias_ref[...]
    rows = [sel[e:e + 1, :] for e in range(N_EXPERTS)]
    best = None
    best_group = None
    for g in range(N_GROUPS):
        m = rows[g * EXPERTS_PER_GROUP:(g + 1) * EXPERTS_PER_GROUP]
        top2 = None
        for i, j in GROUP_PAIRS:
            pair = m[i] + m[j]
            top2 = pair if top2 is None else jnp.maximum(top2, pair)
        if best is None:
            best, best_group = top2, jnp.zeros(top2.shape, jnp.int32)
        else:
            better = top2 > best
            best = jnp.where(better, top2, best)
            best_group = jnp.where(better, g, best_group)
    chosen = []
    weights = []
    total = None
    for e in range(N_EXPERTS):
        g, i = divmod(e, EXPERTS_PER_GROUP)
        rank = jnp.zeros(best.shape, jnp.int32)
        for j in range(EXPERTS_PER_GROUP):
            if j == i:
                continue
            other = rows[g * EXPERTS_PER_GROUP + j]
            ahead = (other >= rows[e]) if j < i else (other > rows[e])
            rank = rank + ahead.astype(jnp.int32)
        chosen.append((best_group == g) & (rank < 2))
        w = jnp.where(chosen[e], scores[e:e + 1, :], 0.0)
        weights.append(w)
        total = w if total is None else total + w
    onehot = []
    w_first = jnp.zeros(best.shape, F32)
    w_second = jnp.zeros(best.shape, F32)
    for g in range(N_GROUPS):
        for i, j in GROUP_PAIRS:
            first, second = g * EXPERTS_PER_GROUP + i, g * EXPERTS_PER_GROUP + j
            both = chosen[first] & chosen[second]
            onehot.append(jnp.where(both, 1.0, 0.0))
            w_first = jnp.where(both, weights[first] / total, w_first)
            w_second = jnp.where(both, weights[second] / total, w_second)
    pad = jnp.zeros((onehot_ref.shape[0] - N_BUCKETS, best.shape[1]), F32)
    onehot_ref[...] = jnp.concatenate(onehot + [pad], axis=0)
    w_ref[...] = jnp.concatenate([w_first, w_second], axis=0)


def _route(logits_t, bias_col):
    n = logits_t.shape[1]
    return pl.pallas_call(
        _route_kernel,
        out_shape=[jax.ShapeDtypeStruct((BUCKET_ROWS, n), F32), jax.ShapeDtypeStruct((2, n), F32)],
        compiler_params=pltpu.CompilerParams(vmem_limit_bytes=VMEM_LIMIT),
        name="route",
    )(logits_t, bias_col)


def _experts_kernel(tok_ref, live_ref, ea_ref, eb_ref, used_ref, x_hbm, w_ref, wga_ref, wua_ref, wda_ref,
                    wgb_ref, wub_ref, wdb_ref, g_ref, b_ref, out_hbm, xbuf, ybuf, sem_in, sem_out,
                    *, alpha, tile):
    del ea_ref, eb_ref
    t = pl.program_id(0)
    used = used_ref[0]
    buf = t % 2

    def row_in(tile_idx, r):
        return pltpu.make_async_copy(x_hbm.at[tok_ref[tile_idx * tile + r]],
                                     xbuf.at[tile_idx % 2, r], sem_in.at[tile_idx % 2])

    def row_out(tile_idx, r):
        return pltpu.make_async_copy(ybuf.at[r], out_hbm.at[tok_ref[tile_idx * tile + r]], sem_out)

    def each_row(action, live_of=None):
        def body(r, carry):
            if live_of is None:
                action(r)
            else:
                pl.when(live_ref[live_of * tile + r] != 0)(lambda: action(r))
            return carry
        lax.fori_loop(0, tile, body, 0, unroll=8)

    @pl.when(jnp.logical_and(t == 0, used > 0))
    def _():
        each_row(lambda r: row_in(t, r).start())

    @pl.when(t < used)
    def _():
        @pl.when(t + 1 < used)
        def _():
            each_row(lambda r: row_in(t + 1, r).start())

        each_row(lambda r: row_in(t, r).wait())
        x = _load_token_tiles(xbuf.at[buf])
        xb = x.astype(BF16)

        def expert(wg_ref, wu_ref, wd_ref):
            hidden = jax.nn.silu(_dot(xb, wg_ref[...])) * _dot(xb, wu_ref[...])
            return _dot(hidden.astype(BF16), wd_ref[...])

        w = w_ref[...]
        y = w[:, 0:1] * expert(wga_ref, wua_ref, wda_ref) + w[:, 1:2] * expert(wgb_ref, wub_ref, wdb_ref)
        y = _layer_norm(alpha * x + y, g_ref[...], b_ref[...])

        @pl.when(t > 0)
        def _():
            each_row(lambda r: row_out(t - 1, r).wait(), live_of=t - 1)

        _store_token_tiles(ybuf, y)
        each_row(lambda r: row_out(t, r).start(), live_of=t)

        @pl.when(t == used - 1)
        def _():
            each_row(lambda r: row_out(t, r).wait(), live_of=t)


def _experts(x_tiles, slot_token, slot_live, w_sorted, first_expert, second_expert, tiles_used,
             wg, wu, wd, g, b, alpha, tile):
    n_slots = slot_token.shape[0]
    idx = lambda fn: (lambda t, tok, live, ea, eb, nu: fn(t, ea, eb))
    up = lambda which: pl.BlockSpec((None, D_MODEL, D_EXPERT), idx(lambda t, ea, eb: ((ea, eb)[which][t], 0, 0)))
    down = lambda which: pl.BlockSpec((None, D_EXPERT, D_MODEL), idx(lambda t, ea, eb: ((ea, eb)[which][t], 0, 0)))
    const = lambda a: pl.BlockSpec(a.shape, idx(lambda t, ea, eb: (0,) * a.ndim))
    return pl.pallas_call(
        functools.partial(_experts_kernel, alpha=alpha, tile=tile),
        out_shape=jax.ShapeDtypeStruct(x_tiles.shape, F32),
        grid_spec=pltpu.PrefetchScalarGridSpec(
            num_scalar_prefetch=5,
            grid=(n_slots // tile,),
            in_specs=[pl.BlockSpec(memory_space=pl.ANY),
                      pl.BlockSpec((tile, 2), idx(lambda t, ea, eb: (t, 0))),
                      up(0), up(0), down(0), up(1), up(1), down(1), const(g), const(b)],
            out_specs=pl.BlockSpec(memory_space=pl.ANY),
            scratch_shapes=[pltpu.VMEM((2, tile, TILE_ROWS, LANES), F32),
                            pltpu.VMEM((tile, TILE_ROWS, LANES), F32),
                            pltpu.SemaphoreType.DMA((2,)), pltpu.SemaphoreType.DMA]),
        compiler_params=_params("arbitrary"),
        name="experts",
    )(slot_token, slot_live, first_expert, second_expert, tiles_used, x_tiles, w_sorted,
      wg, wu, wd, wg, wu, wd, g, b)


def _moe(x_tiles, logits, rb, wg, wu, wd, g, b, alpha, tile):
    n = x_tiles.shape[0]
    onehot, w2 = _route(logits.T, rb)
    prefix = _cumsum_rows(onehot)
    counts = prefix[:, -1].astype(jnp.int32)
    padded = (counts + tile - 1) // tile * tile
    ends = jnp.cumsum(padded)
    starts = ends - padded
    rank = jnp.sum(onehot * prefix, axis=0).astype(jnp.int32) - 1
    pos = jnp.sum(onehot * starts.astype(F32)[:, None], axis=0).astype(jnp.int32) + rank
    n_slots = n + N_BUCKETS * tile
    per_token = jnp.stack([jnp.arange(n, dtype=F32), jnp.ones((n,), F32), w2[0], w2[1]], axis=1)
    per_slot = jnp.zeros((n_slots, 4), F32).at[pos].set(per_token, unique_indices=True)
    slot_token = per_slot[:, 0].astype(jnp.int32)
    slot_live = per_slot[:, 1].astype(jnp.int32)
    w_sorted = per_slot[:, 2:]
    tile_start = jnp.arange(n_slots // tile, dtype=jnp.int32) * tile
    bucket = jnp.sum(tile_start[:, None] >= ends[None, :N_BUCKETS], axis=1)
    bucket = jnp.minimum(bucket, N_BUCKETS - 1).astype(jnp.int32)
    group, pair = bucket // len(GROUP_PAIRS), bucket % len(GROUP_PAIRS)
    pair_first = jnp.asarray([p[0] for p in GROUP_PAIRS], jnp.int32)
    pair_second = jnp.asarray([p[1] for p in GROUP_PAIRS], jnp.int32)
    first_expert = group * EXPERTS_PER_GROUP + pair_first[pair]
    second_expert = group * EXPERTS_PER_GROUP + pair_second[pair]
    tiles_used = (ends[N_BUCKETS - 1] // tile).astype(jnp.int32)[None]

    out_tiles = _experts(x_tiles, slot_token, slot_live, w_sorted, first_expert, second_expert, tiles_used,
                         wg, wu, wd, g, b, alpha, tile)
    return out_tiles.reshape(n, D_MODEL)


IN_SIZES = (HEADS_WIDTH,) * 3 + (N_HEADS, MLA_Q_RANK, MLA_KV_RANK, MLA_ROPE) + (HEADS_WIDTH,) * 6


def _pad_cols(a, width):
    return jnp.pad(a, ((0, 0), (0, width - a.shape[1])))


def _prep_in_proj_weights(w_in, b_forget, q_norm, w_q_up, w_kv_up):
    offs = np.cumsum(IN_SIZES)[:-1].tolist()
    qa, ka, va, fa, cq, ckv, kr, qc, kc, vc, qd, kd, vd = jnp.split(w_in, offs, axis=1)
    w = jnp.concatenate([qa, ka, va, qc, kc, vc, qd, kd, vd, _pad_cols(cq, 256), ckv,
                         _pad_cols(kr, LANES), _pad_cols(fa, LANES)], axis=1).astype(BF16)
    bf = _pad_cols(b_forget[None, :], LANES)
    qn = _pad_cols(q_norm[None, :], 256)
    wq3 = w_q_up.reshape(MLA_Q_RANK, N_HEADS, MLA_NOPE + MLA_ROPE)
    wq = jnp.concatenate([wq3[:, :, :MLA_NOPE].reshape(MLA_Q_RANK, -1),
                          wq3[:, :, MLA_NOPE:].reshape(MLA_Q_RANK, -1)], axis=1)
    wq = jnp.pad(wq, ((0, 256 - MLA_Q_RANK), (0, 0))).astype(BF16)
    wkv = w_kv_up.reshape(MLA_KV_RANK, N_HEADS, MLA_NOPE + MLA_V)
    wcat = jnp.zeros((HEADS_WIDTH + N_HEADS * MLA_ROPE, N_HEADS * Q_CAT), F32)
    wv_pad = jnp.zeros((N_HEADS, Q_CAT, HEADS_WIDTH), F32)
    eye = jnp.eye(MLA_ROPE, dtype=F32)
    for h in range(N_HEADS):
        wcat = wcat.at[h * MLA_NOPE:(h + 1) * MLA_NOPE, h * Q_CAT:h * Q_CAT + MLA_KV_RANK].set(
            wkv[:, h, :MLA_NOPE].T)
        wcat = wcat.at[HEADS_WIDTH + h * MLA_ROPE:HEADS_WIDTH + (h + 1) * MLA_ROPE,
                       h * Q_CAT + MLA_KV_RANK:h * Q_CAT + MLA_KV_RANK + MLA_ROPE].set(eye)
        wv_pad = wv_pad.at[h, :MLA_KV_RANK, h * MLA_V:(h + 1) * MLA_V].set(wkv[:, h, MLA_NOPE:])
    return w, bf, qn, wq, wcat.astype(BF16), wv_pad.astype(BF16)


def _rope_tables(pos):
    half = MLA_ROPE // 2
    inv = ROPE_THETA ** (-jnp.arange(half, dtype=F32) / half)
    ang = pos.astype(F32)[:, None] * inv[None, :]
    cos = jnp.cos(ang)
    sin = jnp.sin(ang)
    reps = LANES // MLA_ROPE
    return (jnp.tile(jnp.concatenate([cos, cos], axis=1), (1, reps)),
            jnp.tile(jnp.concatenate([-sin, sin], axis=1), (1, reps)))


def _band_bias(rel_bias, q0, nq, k0, nk):
    period = nq + nk
    j = np.arange(period)
    d = np.where(j < nk, j, j - period)
    idx = np.clip(q0 - k0 - d, -REL_CLIP, REL_CLIP) + REL_CLIP
    u = rel_bias[:, idx]
    rows = jnp.tile(u, (1, nq))[:, :nq * (period - 1)].reshape(rel_bias.shape[0], nq, period - 1)
    return rows[:, :, :nk]


def _key_major(cache):
    l, b, p, h, d = cache.shape
    return jnp.transpose(cache, (0, 1, 3, 4, 2)).reshape(l, b, h * d, p)


def _from_key_major(state, n_heads):
    b, hd, t = state.shape
    return jnp.transpose(state.reshape(b, n_heads, hd // n_heads, t), (0, 3, 1, 2))


def kernel(x_prompt, x_sample, cache_fox_k, cache_fox_v, cache_fox_logf, cache_mla_latent, cache_mla_krope, cache_band_k, cache_band_v, cache_sb_k, cache_sb_v, w_in, b_forget, mla_q_norm, w_q_up, mla_kv_norm, w_kv_up, rel_bias, w_branch, w_gate, b_gate, w_out, ln1_g, ln1_b, w_router, router_bias, w_e_gate, w_e_up, w_e_down, ln2_g, ln2_b):
    batch, seq, _ = x_prompt.shape
    dec_batch, t_new, _ = x_sample.shape
    depth = w_in.shape[0]
    past = cache_fox_k.shape[2]
    n_band = cache_band_k.shape[2]
    n_p = batch * seq
    n_s = dec_batch * t_new
    alpha = (2 * depth) ** 0.25
    band_tile = BAND_CHUNKS * CHUNK
    assert seq % ATT_TILE == 0 and seq % band_tile == 0 and seq % TOKEN_TILE == 0
    assert n_p % MOE_TILE == 0 and n_s % MOE_TILE_SAMPLE == 0 and n_s % TOKEN_TILE == 0
    assert past % LANES == 0 and t_new <= LANES and t_new % 16 == 0 and ATT_TILE % SB_SUB == 0
    assert past % min(SAMPLE_KEY_TILE, past) == 0 and past % SB_SUB == 0

    xp = x_prompt.reshape(n_p, D_MODEL)
    xs = x_sample.reshape(n_s, D_MODEL)
    rope_p = _rope_tables(jnp.tile(jnp.arange(seq, dtype=jnp.int32), batch))
    rope_s = _rope_tables(jnp.tile(past + jnp.arange(t_new, dtype=jnp.int32), dec_batch))

    c_fox_k, c_fox_v = _key_major(cache_fox_k), _key_major(cache_fox_v)
    c_band_k, c_band_v = _key_major(cache_band_k), _key_major(cache_band_v)
    c_sb_k, c_sb_v = _key_major(cache_sb_k), _key_major(cache_sb_v)
    c_logf = jnp.swapaxes(cache_fox_logf, 2, 3).astype(F32)
    c_krope = jnp.swapaxes(cache_mla_krope, 2, 3)

    wr = _pad_cols(w_router, LANES).astype(BF16)
    rb = router_bias.astype(F32)[:, None]
    total = past + t_new
    padded = -(-total // LANES) * LANES

    prompt_states, sample_states = [], []
    for l in range(depth):
        w, bf, qn, wq, wcat, wv_pad = _prep_in_proj_weights(w_in[l], b_forget[l], mla_q_norm[l],
                                                             w_q_up[l], w_kv_up[l])
        proj_w = (w, bf, qn, wq, mla_kv_norm[l][None, :], wcat)
        (qkv_p, ka_p, va_p, kc_p, vc_p, kd_p, vd_p, logf_p, lat_p, kr_p, kcat_p, qcat_p,
         vat_p, vct_p, vdt_p, latt_p) = _in_proj(xp, *proj_w, *rope_p, batch=batch, seq=seq)
        (qkv_s, ka_s, va_s, kc_s, vc_s, kd_s, vd_s, logf_s, lat_s, kr_s, kcat_s, qcat_s) = _in_proj(
            xs, *proj_w, *rope_s)
        logf_s = logf_s[:, :N_HEADS].reshape(dec_batch, t_new, N_HEADS)
        kr_s = kr_s[:, :MLA_ROPE]

        c_p = _cumsum_rows(logf_p.reshape(batch * N_HEADS, seq)).reshape(batch, N_HEADS, seq)
        all_s = jnp.concatenate([c_logf[l], jnp.swapaxes(logf_s, 1, 2)], axis=2)
        all_s = jnp.pad(all_s, ((0, 0), (0, 0), (0, padded - total)))
        c_s = _cumsum_rows(all_s.reshape(dec_batch * N_HEADS, padded)).reshape(dec_batch, N_HEADS, padded)
        cq_s = c_s[:, :, past:past + t_new].reshape(dec_batch, N_HEADS * t_new, 1)

        bias_p = _band_bias(rel_bias[l], 0, band_tile, -band_tile, 2 * band_tile)
        bias_s = _band_bias(rel_bias[l], past, t_new, past - n_band, n_band + t_new)
        bias_s = bias_s.reshape(N_HEADS * t_new, n_band + t_new)

        branches_p = (_fox_prompt(qkv_p, vat_p, jnp.swapaxes(c_p, 1, 2), c_p, batch, seq),
                      _mla_prompt(qcat_p, kcat_p, latt_p, wv_pad, batch, seq),
                      _band_prompt(qkv_p, vct_p, jnp.swapaxes(bias_p, 1, 2), batch, seq),
                      _sb_prompt(qkv_p, vdt_p, batch, seq))
        branches_s = (_fox_sample(qkv_s, c_fox_k, c_fox_v, cq_s, c_s, l, dec_batch, t_new),
                      _mla_sample(qcat_s, kcat_s, cache_mla_latent, c_krope, wv_pad, l, dec_batch, t_new),
                      _band_sample(qkv_s, c_band_k, c_band_v, bias_s[:, :n_band], bias_s[:, n_band:], l,
                                   past, dec_batch, t_new),
                      _sb_sample(qkv_s, c_sb_k, c_sb_v, l, dec_batch, t_new))

        tail_w = (w_gate[l].astype(BF16), b_gate[l][:, None, :], w_branch[l].astype(BF16),
                  w_out[l].astype(BF16), ln1_g[l][None, :], ln1_b[l][None, :], wr, alpha)
        experts = (w_e_gate[l].astype(BF16), w_e_up[l].astype(BF16), w_e_down[l].astype(BF16),
                   ln2_g[l][None, :], ln2_b[l][None, :], alpha)
        x1_p, logits_p = _merge(xp, branches_p, *tail_w)
        x1_s, logits_s = _merge(xs, branches_s, *tail_w)
        xp = _moe(x1_p, logits_p[:, :N_EXPERTS], rb, *experts, MOE_TILE)
        xs = _moe(x1_s, logits_s[:, :N_EXPERTS], rb, *experts, MOE_TILE_SAMPLE)

        n_keep = min(band_tile, seq)
        prompt_states.append((
            _from_key_major(ka_p, N_HEADS), _from_key_major(va_p, N_HEADS), jnp.swapaxes(logf_p, 1, 2),
            lat_p.reshape(batch, seq, -1), jnp.swapaxes(kr_p, 1, 2),
            _from_key_major(kc_p[:, :, seq - n_keep:], N_HEADS),
            _from_key_major(vc_p[:, :, seq - n_keep:], N_HEADS),
            _from_key_major(kd_p, N_HEADS), _from_key_major(vd_p, N_HEADS)))
        heads = lambda a: a.reshape(dec_batch, t_new, N_HEADS, HEAD_DIM)
        sample_states.append((
            heads(ka_s), heads(va_s), logf_s, lat_s.reshape(dec_batch, t_new, -1),
            kr_s.reshape(dec_batch, t_new, -1), heads(kc_s), heads(vc_s), heads(kd_s), heads(vd_s)))

    stack = lambda states: [jnp.stack(z, axis=0) for z in zip(*states)]
    return (xp.reshape(batch, seq, D_MODEL), xs.reshape(dec_batch, t_new, D_MODEL),
            *stack(prompt_states), *stack(sample_states))
```

```python
import functools

import numpy as np
import jax
import jax.numpy as jnp
from jax import lax
from jax.experimental import pallas as pl
from jax.experimental.pallas import tpu as pltpu

F32 = jnp.float32
BF16 = jnp.bfloat16

D_MODEL = 1024
N_HEADS = 4
HEAD_DIM = 64
HEADS_WIDTH = N_HEADS * HEAD_DIM
CHUNK = 64
BAND_CHUNKS = 8
REL_CLIP = 128
MLA_NOPE = 64
MLA_ROPE = 32
MLA_V = 64
MLA_Q_RANK = 192
MLA_KV_RANK = 128
ROPE_THETA = 10000.0
N_BRANCH = 4
N_EXPERTS = 16
N_GROUPS = 4
EXPERTS_PER_GROUP = N_EXPERTS // N_GROUPS
D_EXPERT = 512
GROUP_PAIRS = tuple((i, j) for i in range(EXPERTS_PER_GROUP) for j in range(i + 1, EXPERTS_PER_GROUP))
N_BUCKETS = N_GROUPS * len(GROUP_PAIRS)
BUCKET_ROWS = 32
NEG_INF = -1e30
LANES = 128
TILE_ROWS = D_MODEL // LANES
VMEM_LIMIT = 56 * 1024 * 1024

QKV_WIDTH = 9 * HEADS_WIDTH
STATE_COLS = (1, 2, 4, 5, 7, 8)
VALUE_COLS = (2, 5, 8)
OFF_CQ = QKV_WIDTH
OFF_CKV = OFF_CQ + 256
OFF_KR = OFF_CKV + MLA_KV_RANK
OFF_FA = OFF_KR + LANES
PROJ_WIDTH = OFF_FA + LANES
Q_CAT = 256

TOKEN_TILE = 512
MOE_TILE = 256
MOE_TILE_SAMPLE = 128
ATT_TILE = 512
SB_SUB = 256
SAMPLE_KEY_TILE = 4096
SB_DEAD = -105.0


def _dot(a, b):
    return jnp.dot(a, b, preferred_element_type=F32)


def _dot_nt(a, b):
    return lax.dot_general(a, b, (((1,), (1,)), ((), ())), preferred_element_type=F32)


def _const_spec(shape):
    nd = len(shape)
    return pl.BlockSpec(shape, lambda *_: (0,) * nd, pipeline_mode=pl.Buffered(1))


def _params(*sem):
    return pltpu.CompilerParams(dimension_semantics=sem, vmem_limit_bytes=VMEM_LIMIT)


def _log_sigmoid(x):
    return jnp.minimum(x, 0.0) - jnp.log1p(jnp.exp(-jnp.abs(x)))


def _softplus(x):
    return jnp.maximum(x, 0.0) + jnp.log1p(jnp.exp(-jnp.abs(x)))


def _layer_norm(h, g, b):
    mu = jnp.mean(h, axis=-1, keepdims=True)
    d = h - mu
    var = jnp.mean(d * d, axis=-1, keepdims=True)
    return d * lax.rsqrt(var + 1e-5) * g + b


def _head_of_lane(shape):
    return lax.broadcasted_iota(jnp.int32, shape, len(shape) - 1) // HEAD_DIM


def _in_proj_kernel(x_ref, w_ref, bf_ref, qn_ref, wq_ref, kvn_ref, wcat_ref, cos_ref, sin_ref,
                    qkv_ref, ka_ref, va_ref, kc_ref, vc_ref, kd_ref, vd_ref,
                    logf_ref, lat_ref, kr_ref, kcat_ref, qcat_ref, *key_major_refs, key_major):
    xb = x_ref[...].astype(BF16)
    proj = _dot(xb, w_ref[...])
    qkv_ref[...] = proj[:, :QKV_WIDTH].astype(BF16)
    for ref, col in zip((ka_ref, va_ref, kc_ref, vc_ref, kd_ref, vd_ref), STATE_COLS):
        state = proj[:, col * HEADS_WIDTH:(col + 1) * HEADS_WIDTH]
        if key_major:
            state = state.T
            if col in VALUE_COLS:
                key_major_refs[VALUE_COLS.index(col)][...] = state.astype(BF16)
        ref[...] = state

    cos = cos_ref[...]
    sin = sin_ref[...]
    lane = lax.broadcasted_iota(jnp.int32, cos.shape, 1)
    first_half = (lane % MLA_ROPE) < (MLA_ROPE // 2)

    def rope(v):
        partner = jnp.where(first_half, pltpu.roll(v, LANES - MLA_ROPE // 2, 1),
                            pltpu.roll(v, MLA_ROPE // 2, 1))
        return v * cos + partner * sin

    cq = proj[:, OFF_CQ:OFF_CQ + 256]
    ms = jnp.sum(cq * cq, axis=-1, keepdims=True) * (1.0 / MLA_Q_RANK)
    cqn = cq * lax.rsqrt(ms + 1e-6) * qn_ref[...]
    qb = _dot(cqn.astype(BF16), wq_ref[...])
    q_rope = rope(qb[:, HEADS_WIDTH:])
    q_in = jnp.concatenate([qb[:, :HEADS_WIDTH], q_rope], axis=1) * ((MLA_NOPE + MLA_ROPE) ** -0.5)
    qcat_ref[...] = _dot(q_in.astype(BF16), wcat_ref[...]).astype(BF16)

    ckv = proj[:, OFF_CKV:OFF_CKV + MLA_KV_RANK]
    lat = ckv * lax.rsqrt(jnp.mean(ckv * ckv, axis=-1, keepdims=True) + 1e-6) * kvn_ref[...]
    lat_ref[...] = lat
    kr = rope(proj[:, OFF_KR:OFF_KR + LANES])
    kcat_ref[...] = jnp.concatenate([lat, kr], axis=1).astype(BF16)
    logf = _log_sigmoid(proj[:, OFF_FA:OFF_FA + LANES] + bf_ref[...])
    if key_major:
        kr_ref[...] = kr.T[:MLA_ROPE, :]
        logf_ref[...] = logf.T[:N_HEADS, :]
        key_major_refs[len(VALUE_COLS)][...] = lat.T.astype(BF16)
    else:
        kr_ref[...] = kr
        logf_ref[...] = logf


def _in_proj(x, w, bf, qn, wq, kvn, wcat, cos, sin, *, batch=None, seq=None):
    n = x.shape[0]
    tm = TOKEN_TILE
    key_major = batch is not None
    row = lambda width: pl.BlockSpec((tm, width), lambda i: (i, 0))
    if key_major:
        per = seq // tm
        tr = lambda height: pl.BlockSpec((None, height, tm), lambda i: (i // per, 0, i % per))
        tshape = lambda height: jax.ShapeDtypeStruct((batch, height, seq), F32)
        state_shapes = [tshape(HEADS_WIDTH)] * 6 + [tshape(N_HEADS), jax.ShapeDtypeStruct((n, LANES), F32),
                                                    tshape(MLA_ROPE)]
        state_specs = [tr(HEADS_WIDTH)] * 6 + [tr(N_HEADS), row(LANES), tr(MLA_ROPE)]
    else:
        state_shapes = ([jax.ShapeDtypeStruct((n, HEADS_WIDTH), F32)] * 6
                        + [jax.ShapeDtypeStruct((n, LANES), F32)] * 3)
        state_specs = [row(HEADS_WIDTH)] * 6 + [row(LANES)] * 3
    out_shape = ([jax.ShapeDtypeStruct((n, QKV_WIDTH), BF16)] + state_shapes
                 + [jax.ShapeDtypeStruct((n, 2 * LANES), BF16),
                    jax.ShapeDtypeStruct((n, N_HEADS * Q_CAT), BF16)])
    out_specs = [row(QKV_WIDTH)] + state_specs + [row(2 * LANES), row(N_HEADS * Q_CAT)]
    if key_major:
        out_shape += ([jax.ShapeDtypeStruct((batch, HEADS_WIDTH, seq), BF16)] * len(VALUE_COLS)
                      + [jax.ShapeDtypeStruct((batch, MLA_KV_RANK, seq), BF16)])
        out_specs += [tr(HEADS_WIDTH)] * len(VALUE_COLS) + [tr(MLA_KV_RANK)]
    return pl.pallas_call(
        functools.partial(_in_proj_kernel, key_major=key_major),
        out_shape=out_shape,
        grid=(n // tm,),
        in_specs=[row(D_MODEL), _const_spec(w.shape), _const_spec(bf.shape), _const_spec(qn.shape),
                  _const_spec(wq.shape), _const_spec(kvn.shape), _const_spec(wcat.shape),
                  row(LANES), row(LANES)],
        out_specs=out_specs,
        compiler_params=_params("parallel"),
        name="in_proj_prompt" if key_major else "in_proj_sample",
    )(x, w, bf, qn, wq, kvn, wcat, cos, sin)


def _cumsum_kernel(x_ref, o_ref):
    r, t = x_ref.shape
    row = lax.broadcasted_iota(jnp.int32, (LANES, LANES), 0)
    col = lax.broadcasted_iota(jnp.int32, (LANES, LANES), 1)
    tri = jnp.where(row <= col, 1.0, 0.0).astype(BF16)
    carry = jnp.zeros((r, 1), F32)
    for c in range(t // LANES):
        xc = x_ref[:, c * LANES:(c + 1) * LANES]
        hi = xc.astype(BF16)
        r1 = xc - hi.astype(F32)
        mid = r1.astype(BF16)
        lo = (r1 - mid.astype(F32)).astype(BF16)
        y = _dot(hi, tri) + _dot(mid, tri) + _dot(lo, tri) + carry
        o_ref[:, c * LANES:(c + 1) * LANES] = y
        carry = y[:, LANES - 1:LANES]


def _cumsum_rows(x):
    return pl.pallas_call(
        _cumsum_kernel,
        out_shape=jax.ShapeDtypeStruct(x.shape, F32),
        name="cumsum",
    )(x)


def _online_softmax_update(s, pv, m_ref, l_ref, acc_ref, h):
    m_prev = m_ref[h]
    m_new = jnp.maximum(m_prev, jnp.max(s, axis=1, keepdims=True))
    alpha = jnp.exp(m_prev - m_new)
    p = jnp.exp(s - m_new)
    l_ref[h] = alpha * l_ref[h] + jnp.sum(p, axis=1, keepdims=True)
    acc_ref[h] = alpha * acc_ref[h] + pv(p.astype(BF16))
    m_ref[h] = m_new


def _init_softmax_state(m_ref, l_ref, acc_ref):
    m_ref[...] = jnp.full(m_ref.shape, NEG_INF, F32)
    l_ref[...] = jnp.zeros(l_ref.shape, F32)
    acc_ref[...] = jnp.zeros(acc_ref.shape, F32)


def _store_head_masked_q(q_ref, qs_ref, rows):
    q = q_ref[...] * (HEAD_DIM ** -0.5)
    head = _head_of_lane(q.shape)
    for h in range(N_HEADS):
        qs_ref[h * rows:(h + 1) * rows, :] = jnp.where(head == h, q, jnp.zeros_like(q))


def _merge_heads(per_head):
    head = _head_of_lane(per_head[0].shape)
    out = jnp.zeros_like(per_head[0])
    for h in range(N_HEADS):
        out = jnp.where(head == h, per_head[h], out)
    return out


def _online_softmax_update_t(st, pv, m_ref, l_ref, acc_ref, h):
    m_prev = m_ref[h]
    m_new = jnp.maximum(m_prev, jnp.max(st, axis=0, keepdims=True))
    alpha = jnp.exp(m_prev - m_new)
    p = jnp.exp(st - m_new)
    l_ref[h] = alpha * l_ref[h] + jnp.sum(p, axis=0, keepdims=True)
    acc_ref[h] = alpha * acc_ref[h] + pv(p.astype(BF16))
    m_ref[h] = m_new


def _fox_prompt_kernel(q_ref, k_ref, vt_ref, cq_ref, ck_ref, o_ref, qt_ref, m_ref, l_ref, acc_ref):
    qi = pl.program_id(1)
    ki = pl.program_id(2)

    @pl.when(ki == 0)
    def _():
        _init_softmax_state(m_ref, l_ref, acc_ref)
        _store_head_masked_qt(q_ref, qt_ref)

    def step(diagonal):
        k = k_ref[...]
        vt = vt_ref[...]
        cq = cq_ref[...]
        ck = ck_ref[...]
        for h in range(N_HEADS):
            st = _dot(k, qt_ref[h]) + cq[h:h + 1, :] - ck[:, h:h + 1]
            if diagonal:
                key = lax.broadcasted_iota(jnp.int32, st.shape, 0)
                qry = lax.broadcasted_iota(jnp.int32, st.shape, 1)
                st = jnp.where(key <= qry, st, NEG_INF)
            vt_h = vt[h * HEAD_DIM:(h + 1) * HEAD_DIM, :]
            _online_softmax_update_t(st, lambda p: _dot(vt_h, p), m_ref, l_ref, acc_ref, h)

    pl.when(ki < qi)(lambda: step(False))
    pl.when(ki == qi)(lambda: step(True))

    @pl.when(ki == pl.num_programs(2) - 1)
    def _():
        ot = jnp.concatenate([acc_ref[h] / l_ref[h] for h in range(N_HEADS)], axis=0)
        o_ref[...] = ot.T.astype(BF16)


def _fox_prompt(qkv, vt, c_tok, c_head, batch, seq):
    t = ATT_TILE
    nt = seq // t
    return pl.pallas_call(
        _fox_prompt_kernel,
        out_shape=jax.ShapeDtypeStruct((batch * seq, HEADS_WIDTH), BF16),
        grid=(batch, nt, nt),
        in_specs=[pl.BlockSpec((t, HEADS_WIDTH), lambda b, i, j: (b * nt + i, 0)),
                  pl.BlockSpec((t, HEADS_WIDTH), lambda b, i, j: (b * nt + jnp.minimum(i, j), 1)),
                  pl.BlockSpec((None, HEADS_WIDTH, t), lambda b, i, j: (b, 0, jnp.minimum(i, j))),
                  pl.BlockSpec((None, N_HEADS, t), lambda b, i, j: (b, 0, i)),
                  pl.BlockSpec((None, t, N_HEADS), lambda b, i, j: (b, jnp.minimum(i, j), 0))],
        out_specs=pl.BlockSpec((t, HEADS_WIDTH), lambda b, i, j: (b * nt + i, 0)),
        scratch_shapes=[pltpu.VMEM((N_HEADS, HEADS_WIDTH, t), BF16),
                        pltpu.VMEM((N_HEADS, 1, t), F32), pltpu.VMEM((N_HEADS, 1, t), F32),
                        pltpu.VMEM((N_HEADS, HEAD_DIM, t), F32)],
        compiler_params=_params("parallel", "parallel", "arbitrary"),
        name="fox_prompt",
    )(qkv, qkv, vt, c_head, c_tok)


def _mla_prompt_kernel(q_ref, k_ref, latt_ref, wv_ref, o_ref, qt_ref, m_ref, l_ref, acc_ref):
    qi = pl.program_id(1)
    ki = pl.program_id(2)

    @pl.when(ki == 0)
    def _():
        _init_softmax_state(m_ref, l_ref, acc_ref)
        for h in range(N_HEADS):
            qt_ref[h] = q_ref[:, h * Q_CAT:(h + 1) * Q_CAT].astype(F32).T.astype(BF16)

    def step(diagonal):
        k = k_ref[...]
        latt = latt_ref[...]
        for h in range(N_HEADS):
            st = _dot(k, qt_ref[h])
            if diagonal:
                key = lax.broadcasted_iota(jnp.int32, st.shape, 0)
                qry = lax.broadcasted_iota(jnp.int32, st.shape, 1)
                st = jnp.where(key // CHUNK <= qry // CHUNK, st, NEG_INF)
            _online_softmax_update_t(st, lambda p: _dot(latt, p), m_ref, l_ref, acc_ref, h)

    pl.when(ki < qi)(lambda: step(False))
    pl.when(ki == qi)(lambda: step(True))

    @pl.when(ki == pl.num_programs(2) - 1)
    def _():
        out = None
        for h in range(N_HEADS):
            o_h = (acc_ref[h] / l_ref[h]).T.astype(BF16)
            term = _dot(o_h, wv_ref[h, :MLA_KV_RANK, :])
            out = term if out is None else out + term
        o_ref[...] = out.astype(BF16)


def _mla_prompt(qcat, kcat, latt, wv_pad, batch, seq):
    t = ATT_TILE
    nt = seq // t
    return pl.pallas_call(
        _mla_prompt_kernel,
        out_shape=jax.ShapeDtypeStruct((batch * seq, HEADS_WIDTH), BF16),
        grid=(batch, nt, nt),
        in_specs=[pl.BlockSpec((t, N_HEADS * Q_CAT), lambda b, i, j: (b * nt + i, 0)),
                  pl.BlockSpec((t, Q_CAT), lambda b, i, j: (b * nt + jnp.minimum(i, j), 0)),
                  pl.BlockSpec((None, MLA_KV_RANK, t), lambda b, i, j: (b, 0, jnp.minimum(i, j))),
                  _const_spec(wv_pad.shape)],
        out_specs=pl.BlockSpec((t, HEADS_WIDTH), lambda b, i, j: (b * nt + i, 0)),
        scratch_shapes=[pltpu.VMEM((N_HEADS, Q_CAT, t), BF16),
                        pltpu.VMEM((N_HEADS, 1, t), F32), pltpu.VMEM((N_HEADS, 1, t), F32),
                        pltpu.VMEM((N_HEADS, MLA_KV_RANK, t), F32)],
        compiler_params=_params("parallel", "parallel", "arbitrary"),
        name="mla_prompt",
    )(qcat, kcat, latt, wv_pad)


def _strict_upper_ones(n):
    row = lax.broadcasted_iota(jnp.int32, (n, n), 0)
    col = lax.broadcasted_iota(jnp.int32, (n, n), 1)
    return jnp.where(row > col, 1.0, 0.0).astype(BF16)


def _sb_block(z, u, rest, mask):
    sp = _softplus(z)
    log_1m = -sp
    if mask is not None:
        log_1m = jnp.where(mask, log_1m, 0.0)
    hi = log_1m.astype(BF16)
    lo = (log_1m - hi.astype(F32)).astype(BF16)
    later = _dot(hi, u) + _dot(lo, u) + rest
    a = jnp.exp(z - sp + later)
    if mask is not None:
        a = jnp.where(mask, a, 0.0)
    return a.astype(BF16), rest + jnp.sum(log_1m, axis=1, keepdims=True)


def _store_head_masked_qt(q_ref, qt_ref):
    qt = q_ref[...].astype(F32).T * (HEAD_DIM ** -0.5)
    head = lax.broadcasted_iota(jnp.int32, qt.shape, 0) // HEAD_DIM
    for h in range(N_HEADS):
        qt_ref[h] = jnp.where(head == h, qt, 0.0).astype(BF16)


def _sb_block_t(zt, u, rest, mask):
    sp = _softplus(zt)
    log_1m = -sp
    if mask is not None:
        log_1m = jnp.where(mask, log_1m, 0.0)
    hi = log_1m.astype(BF16)
    lo = (log_1m - hi.astype(F32)).astype(BF16)
    later = _dot(u, hi) + _dot(u, lo) + rest
    a = jnp.exp(zt - sp + later)
    if mask is not None:
        a = jnp.where(mask, a, 0.0)
    return a.astype(BF16), rest + jnp.sum(log_1m, axis=0, keepdims=True)


def _sb_prompt_kernel(q_ref, k_ref, vt_ref, o_ref, qt_ref, u_ref, rest_ref, acc_ref):
    tq = q_ref.shape[0]
    qi = pl.program_id(1)
    _store_head_masked_qt(q_ref, qt_ref)
    key_s = lax.broadcasted_iota(jnp.int32, (SB_SUB, SB_SUB), 0)
    key_j = lax.broadcasted_iota(jnp.int32, (SB_SUB, SB_SUB), 1)
    u_ref[...] = jnp.where(key_j > key_s, 1.0, 0.0).astype(BF16)
    rest_ref[...] = jnp.zeros(rest_ref.shape, F32)
    acc_ref[...] = jnp.zeros(acc_ref.shape, F32)

    def visit(start, first_key):
        k = k_ref[pl.ds(start, SB_SUB), :]
        vt = vt_ref[:, pl.ds(start, SB_SUB)]
        u = u_ref[...]
        live = None
        for h in range(N_HEADS):
            zt = _dot(k, qt_ref[h])
            mask = None
            if first_key is not None:
                key = lax.broadcasted_iota(jnp.int32, zt.shape, 0) + first_key
                qry = lax.broadcasted_iota(jnp.int32, zt.shape, 1)
                mask = key < qry
            a, rest = _sb_block_t(zt, u, rest_ref[h], mask)
            acc_ref[h] += _dot(vt[h * HEAD_DIM:(h + 1) * HEAD_DIM, :], a)
            rest_ref[h] = rest
            top = jnp.max(rest)
            live = top if live is None else jnp.maximum(live, top)
        return live

    live = None
    for sub in reversed(range(tq // SB_SUB)):
        live = visit(pl.multiple_of(qi * tq + sub * SB_SUB, SB_SUB), sub * SB_SUB)

    def more(carry):
        return jnp.logical_and(carry[0] >= 0, carry[1] > SB_DEAD)

    def older(carry):
        return carry[0] - 1, visit(pl.multiple_of(carry[0] * SB_SUB, SB_SUB), None)

    lax.while_loop(more, older, (qi * (tq // SB_SUB) - 1, live))
    o_ref[...] = jnp.concatenate([acc_ref[h] for h in range(N_HEADS)], axis=0).T.astype(BF16)


def _sb_prompt(qkv, vt, batch, seq):
    t = ATT_TILE
    nt = seq // t
    return pl.pallas_call(
        _sb_prompt_kernel,
        out_shape=jax.ShapeDtypeStruct((batch * seq, HEADS_WIDTH), BF16),
        grid=(batch, nt),
        in_specs=[pl.BlockSpec((t, HEADS_WIDTH), lambda b, i: (b * nt + i, 6)),
                  pl.BlockSpec((seq, HEADS_WIDTH), lambda b, i: (b, 7)),
                  pl.BlockSpec((None, HEADS_WIDTH, seq), lambda b, i: (b, 0, 0))],
        out_specs=pl.BlockSpec((t, HEADS_WIDTH), lambda b, i: (b * nt + i, 0)),
        scratch_shapes=[pltpu.VMEM((N_HEADS, HEADS_WIDTH, t), BF16),
                        pltpu.VMEM((SB_SUB, SB_SUB), BF16),
                        pltpu.VMEM((N_HEADS, 1, t), F32),
                        pltpu.VMEM((N_HEADS, HEAD_DIM, t), F32)],
        compiler_params=_params("parallel", "arbitrary"),
        name="sb_prompt",
    )(qkv, qkv, vt)


def _band_prompt_kernel(q_ref, kp_ref, kc_ref, vtp_ref, vtc_ref, bias_ref, o_ref, qt_ref):
    tq = q_ref.shape[0]
    qi = pl.program_id(1)
    _store_head_masked_qt(q_ref, qt_ref)
    k = jnp.concatenate([kp_ref[...], kc_ref[...]], axis=0)
    vt = jnp.concatenate([vtp_ref[...], vtc_ref[...]], axis=1)
    key = lax.broadcasted_iota(jnp.int32, (2 * tq, tq), 0)
    qry_chunk = lax.broadcasted_iota(jnp.int32, (2 * tq, tq), 1) // CHUNK
    key_chunk = key // CHUNK - tq // CHUNK
    valid = (key_chunk <= qry_chunk) & (key_chunk >= qry_chunk - BAND_CHUNKS)
    valid = valid & ((key >= tq) | (qi > 0))
    outs = []
    for h in range(N_HEADS):
        st = _dot(k, qt_ref[h]) + bias_ref[h]
        st = jnp.where(valid, st, NEG_INF)
        p = jnp.exp(st - jnp.max(st, axis=0, keepdims=True))
        denom = jnp.sum(p, axis=0, keepdims=True)
        outs.append(_dot(vt[h * HEAD_DIM:(h + 1) * HEAD_DIM, :], p.astype(BF16)) / denom)
    o_ref[...] = jnp.concatenate(outs, axis=0).T.astype(BF16)


def _band_prompt(qkv, vt, bias_tile, batch, seq):
    t = BAND_CHUNKS * CHUNK
    nt = seq // t
    prev = lambda i: jnp.maximum(i - 1, 0)
    return pl.pallas_call(
        _band_prompt_kernel,
        out_shape=jax.ShapeDtypeStruct((batch * seq, HEADS_WIDTH), BF16),
        grid=(batch, nt),
        in_specs=[pl.BlockSpec((t, HEADS_WIDTH), lambda b, i: (b * nt + i, 3)),
                  pl.BlockSpec((t, HEADS_WIDTH), lambda b, i: (b * nt + prev(i), 4)),
                  pl.BlockSpec((t, HEADS_WIDTH), lambda b, i: (b * nt + i, 4)),
                  pl.BlockSpec((None, HEADS_WIDTH, t), lambda b, i: (b, 0, prev(i))),
                  pl.BlockSpec((None, HEADS_WIDTH, t), lambda b, i: (b, 0, i)),
                  _const_spec(bias_tile.shape)],
        out_specs=pl.BlockSpec((t, HEADS_WIDTH), lambda b, i: (b * nt + i, 0)),
        scratch_shapes=[pltpu.VMEM((N_HEADS, HEADS_WIDTH, t), BF16)],
        compiler_params=_params("parallel", "parallel"),
        name="band_prompt",
    )(qkv, qkv, qkv, vt, vt, bias_tile)


def _rows_per_head(x, rows):
    return jnp.concatenate([jnp.broadcast_to(x[h:h + 1, :], (rows, x.shape[1])) for h in range(N_HEADS)],
                           axis=0)


def _unstack_heads(acc, rows):
    return _merge_heads([acc[h * rows:(h + 1) * rows, :] for h in range(N_HEADS)])


def _fox_sample_kernel(q_ref, kc_ref, vc_ref, kn_ref, vn_ref, cq_ref, ckc_ref, ckn_ref, o_ref,
                       qs_ref, m_ref, l_ref, acc_ref):
    t = q_ref.shape[0]
    ki = pl.program_id(1)

    @pl.when(ki == 0)
    def _():
        _init_softmax_state(m_ref, l_ref, acc_ref)
        _store_head_masked_q(q_ref, qs_ref, t)

    qs = qs_ref[...]
    cq = cq_ref[...]
    s = _dot(qs, kc_ref[...].astype(BF16)) + cq - _rows_per_head(ckc_ref[...], t)
    vc = vc_ref[...].astype(BF16)
    _online_softmax_update(s, lambda p: _dot_nt(p, vc), m_ref, l_ref, acc_ref, 0)

    @pl.when(ki == pl.num_programs(1) - 1)
    def _():
        s = _dot_nt(qs, kn_ref[...]) + cq - _rows_per_head(ckn_ref[:, :t], t)
        row = lax.broadcasted_iota(jnp.int32, s.shape, 0) % t
        col = lax.broadcasted_iota(jnp.int32, s.shape, 1)
        s = jnp.where(col <= row, s, NEG_INF)
        _online_softmax_update(s, lambda p: _dot(p, vn_ref[...]), m_ref, l_ref, acc_ref, 0)
        o_ref[...] = _unstack_heads(acc_ref[0] / l_ref[0], t).astype(BF16)


def _fox_sample(qkv, cache_kt, cache_vt, cq_rows, c_head, layer, dec_batch, t):
    past = cache_kt.shape[3]
    tk = min(SAMPLE_KEY_TILE, past)
    nk = past // tk
    return pl.pallas_call(
        _fox_sample_kernel,
        out_shape=jax.ShapeDtypeStruct((dec_batch * t, HEADS_WIDTH), BF16),
        grid=(dec_batch, nk),
        in_specs=[pl.BlockSpec((t, HEADS_WIDTH), lambda b, j: (b, 0)),
                  pl.BlockSpec((None, None, HEADS_WIDTH, tk), lambda b, j: (layer, b, 0, j)),
                  pl.BlockSpec((None, None, HEADS_WIDTH, tk), lambda b, j: (layer, b, 0, j)),
                  pl.BlockSpec((t, HEADS_WIDTH), lambda b, j: (b, 1)),
                  pl.BlockSpec((t, HEADS_WIDTH), lambda b, j: (b, 2)),
                  pl.BlockSpec((None, N_HEADS * t, 1), lambda b, j: (b, 0, 0)),
                  pl.BlockSpec((None, N_HEADS, tk), lambda b, j: (b, 0, j)),
                  pl.BlockSpec((None, N_HEADS, LANES), lambda b, j: (b, 0, past // LANES))],
        out_specs=pl.BlockSpec((t, HEADS_WIDTH), lambda b, j: (b, 0)),
        scratch_shapes=[pltpu.VMEM((N_HEADS * t, HEADS_WIDTH), BF16),
                        pltpu.VMEM((1, N_HEADS * t, 1), F32), pltpu.VMEM((1, N_HEADS * t, 1), F32),
                        pltpu.VMEM((1, N_HEADS * t, HEADS_WIDTH), F32)],
        compiler_params=_params("parallel", "arbitrary"),
        name="fox_sample",
    )(qkv, cache_kt, cache_vt, qkv, qkv, cq_rows, c_head, c_head)


def _mla_sample_kernel(q_ref, latc_ref, krc_ref, kn_ref, wv_ref, o_ref,
                       qa_ref, qr_ref, m_ref, l_ref, acc_ref, *, past):
    t = q_ref.shape[0]
    ki = pl.program_id(1)

    @pl.when(ki == 0)
    def _():
        _init_softmax_state(m_ref, l_ref, acc_ref)
        for h in range(N_HEADS):
            qa_ref[h * t:(h + 1) * t, :] = q_ref[:, h * Q_CAT:h * Q_CAT + MLA_KV_RANK]
            qr_ref[h * t:(h + 1) * t, :] = q_ref[:, h * Q_CAT + MLA_KV_RANK:
                                                 h * Q_CAT + MLA_KV_RANK + MLA_ROPE]

    lat = latc_ref[...].astype(BF16)
    s = _dot_nt(qa_ref[...], lat) + _dot(qr_ref[...], krc_ref[...].astype(BF16))
    _online_softmax_update(s, lambda p: _dot(p, lat), m_ref, l_ref, acc_ref, 0)

    @pl.when(ki == pl.num_programs(1) - 1)
    def _():
        lat_n = kn_ref[:, :MLA_KV_RANK]
        s = _dot_nt(qa_ref[...], lat_n) + _dot_nt(qr_ref[...],
                                                  kn_ref[:, MLA_KV_RANK:MLA_KV_RANK + MLA_ROPE])
        q_pos = past + lax.broadcasted_iota(jnp.int32, s.shape, 0) % t
        k_pos = past + lax.broadcasted_iota(jnp.int32, s.shape, 1)
        s = jnp.where(k_pos // CHUNK <= q_pos // CHUNK, s, NEG_INF)
        _online_softmax_update(s, lambda p: _dot(p, lat_n), m_ref, l_ref, acc_ref, 0)
        o = (acc_ref[0] / l_ref[0]).astype(BF16)
        out = None
        for h in range(N_HEADS):
            term = _dot(o[h * t:(h + 1) * t, :], wv_ref[h, :MLA_KV_RANK, :])
            out = term if out is None else out + term
        o_ref[...] = out.astype(BF16)


def _mla_sample(qcat, kcat, cache_lat, cache_krt, wv_pad, layer, dec_batch, t):
    past = cache_lat.shape[2]
    tk = min(SAMPLE_KEY_TILE, past)
    nk = past // tk
    return pl.pallas_call(
        functools.partial(_mla_sample_kernel, past=past),
        out_shape=jax.ShapeDtypeStruct((dec_batch * t, HEADS_WIDTH), BF16),
        grid=(dec_batch, nk),
        in_specs=[pl.BlockSpec((t, N_HEADS * Q_CAT), lambda b, j: (b, 0)),
                  pl.BlockSpec((None, None, tk, MLA_KV_RANK), lambda b, j: (layer, b, j, 0)),
                  pl.BlockSpec((None, None, MLA_ROPE, tk), lambda b, j: (layer, b, 0, j)),
                  pl.BlockSpec((t, Q_CAT), lambda b, j: (b, 0)),
                  _const_spec(wv_pad.shape)],
        out_specs=pl.BlockSpec((t, HEADS_WIDTH), lambda b, j: (b, 0)),
        scratch_shapes=[pltpu.VMEM((N_HEADS * t, MLA_KV_RANK), BF16),
                        pltpu.VMEM((N_HEADS * t, MLA_ROPE), BF16),
                        pltpu.VMEM((1, N_HEADS * t, 1), F32), pltpu.VMEM((1, N_HEADS * t, 1), F32),
                        pltpu.VMEM((1, N_HEADS * t, MLA_KV_RANK), F32)],
        compiler_params=_params("parallel", "arbitrary"),
        name="mla_sample",
    )(qcat, cache_lat, cache_krt, kcat, wv_pad)


def _band_sample_kernel(q_ref, kc_ref, vc_ref, kn_ref, vn_ref, biasc_ref, biasn_ref, o_ref, qs_ref,
                        *, past):
    t = q_ref.shape[0]
    n_band = kc_ref.shape[1]
    _store_head_masked_q(q_ref, qs_ref, t)
    qs = qs_ref[...]

    def masked(s, first_pos):
        q_chunk = (past + lax.broadcasted_iota(jnp.int32, s.shape, 0) % t) // CHUNK
        k_pos = first_pos + lax.broadcasted_iota(jnp.int32, s.shape, 1)
        k_chunk = k_pos // CHUNK
        valid = (k_pos >= 0) & (k_chunk <= q_chunk) & (k_chunk >= q_chunk - BAND_CHUNKS)
        return jnp.where(valid, s, NEG_INF)

    s_c = masked(_dot(qs, kc_ref[...].astype(BF16)) + biasc_ref[...], past - n_band)
    s_n = masked(_dot_nt(qs, kn_ref[...]) + biasn_ref[...], past)
    m = jnp.maximum(jnp.max(s_c, axis=1, keepdims=True), jnp.max(s_n, axis=1, keepdims=True))
    p_c = jnp.exp(s_c - m)
    p_n = jnp.exp(s_n - m)
    denom = jnp.sum(p_c, axis=1, keepdims=True) + jnp.sum(p_n, axis=1, keepdims=True)
    acc = _dot_nt(p_c.astype(BF16), vc_ref[...].astype(BF16)) + _dot(p_n.astype(BF16), vn_ref[...])
    o_ref[...] = _unstack_heads(acc / denom, t).astype(BF16)


def _band_sample(qkv, cache_kt, cache_vt, bias_cache, bias_new, layer, past, dec_batch, t):
    n_band = cache_kt.shape[3]
    return pl.pallas_call(
        functools.partial(_band_sample_kernel, past=past),
        out_shape=jax.ShapeDtypeStruct((dec_batch * t, HEADS_WIDTH), BF16),
        grid=(dec_batch,),
        in_specs=[pl.BlockSpec((t, HEADS_WIDTH), lambda b: (b, 3)),
                  pl.BlockSpec((None, None, HEADS_WIDTH, n_band), lambda b: (layer, b, 0, 0)),
                  pl.BlockSpec((None, None, HEADS_WIDTH, n_band), lambda b: (layer, b, 0, 0)),
                  pl.BlockSpec((t, HEADS_WIDTH), lambda b: (b, 4)),
                  pl.BlockSpec((t, HEADS_WIDTH), lambda b: (b, 5)),
                  _const_spec(bias_cache.shape), _const_spec(bias_new.shape)],
        out_specs=pl.BlockSpec((t, HEADS_WIDTH), lambda b: (b, 0)),
        scratch_shapes=[pltpu.VMEM((N_HEADS * t, HEADS_WIDTH), BF16)],
        compiler_params=_params("parallel"),
        name="band_sample",
    )(qkv, cache_kt, cache_vt, qkv, qkv, bias_cache, bias_new)


def _sb_sample_kernel(q_ref, kc_ref, vc_ref, kn_ref, vn_ref, o_ref, qs_ref, u_ref, rest_ref, acc_ref):
    t = q_ref.shape[0]
    past = kc_ref.shape[1]
    _store_head_masked_q(q_ref, qs_ref, t)
    u = _strict_upper_ones(SB_SUB)
    u_ref[...] = u
    qs = qs_ref[...]

    z = _dot_nt(qs, kn_ref[...])
    row = lax.broadcasted_iota(jnp.int32, z.shape, 0) % t
    col = lax.broadcasted_iota(jnp.int32, z.shape, 1)
    a, rest = _sb_block(z, u[:t, :t], jnp.zeros((z.shape[0], 1), F32), col < row)
    acc_ref[...] = _dot(a, vn_ref[...])
    rest_ref[...] = rest

    def more(carry):
        return jnp.logical_and(carry[0] >= 0, carry[1] > SB_DEAD)

    def older(carry):
        start = pl.multiple_of(carry[0] * SB_SUB, SB_SUB)
        k = kc_ref[:, pl.ds(start, SB_SUB)].astype(BF16)
        v = vc_ref[:, pl.ds(start, SB_SUB)].astype(BF16)
        a, rest = _sb_block(_dot(qs_ref[...], k), u_ref[...], rest_ref[...], None)
        acc_ref[...] += _dot_nt(a, v)
        rest_ref[...] = rest
        return carry[0] - 1, jnp.max(rest)

    lax.while_loop(more, older, (past // SB_SUB - 1, jnp.max(rest)))
    o_ref[...] = _unstack_heads(acc_ref[...], t).astype(BF16)


def _sb_sample(qkv, cache_kt, cache_vt, layer, dec_batch, t):
    past = cache_kt.shape[3]
    return pl.pallas_call(
        _sb_sample_kernel,
        out_shape=jax.ShapeDtypeStruct((dec_batch * t, HEADS_WIDTH), BF16),
        grid=(dec_batch,),
        in_specs=[pl.BlockSpec((t, HEADS_WIDTH), lambda b: (b, 6)),
                  pl.BlockSpec((None, None, HEADS_WIDTH, past), lambda b: (layer, b, 0, 0)),
                  pl.BlockSpec((None, None, HEADS_WIDTH, past), lambda b: (layer, b, 0, 0)),
                  pl.BlockSpec((t, HEADS_WIDTH), lambda b: (b, 7)),
                  pl.BlockSpec((t, HEADS_WIDTH), lambda b: (b, 8))],
        out_specs=pl.BlockSpec((t, HEADS_WIDTH), lambda b: (b, 0)),
        scratch_shapes=[pltpu.VMEM((N_HEADS * t, HEADS_WIDTH), BF16),
                        pltpu.VMEM((SB_SUB, SB_SUB), BF16),
                        pltpu.VMEM((N_HEADS * t, 1), F32),
                        pltpu.VMEM((N_HEADS * t, HEADS_WIDTH), F32)],
        compiler_params=_params("parallel"),
        name="sb_sample",
    )(qkv, cache_kt, cache_vt, qkv, qkv)


def _store_token_tiles(ref, value):
    for s in range(TILE_ROWS):
        ref[:, s, :] = value[:, s * LANES:(s + 1) * LANES]


def _load_token_tiles(ref):
    return jnp.concatenate([ref[:, s, :] for s in range(TILE_ROWS)], axis=1)


def _merge_kernel(x_ref, oa_ref, ob_ref, oc_ref, od_ref, wg_ref, bg_ref, wb_ref, wo_ref,
                  g_ref, b_ref, wr_ref, y_ref, logit_ref, *, alpha):
    x = x_ref[...]
    xb = x.astype(BF16)
    merged = None
    for n, o_ref in enumerate((oa_ref, ob_ref, oc_ref, od_ref)):
        gate = jax.nn.sigmoid(_dot(xb, wg_ref[n]) + bg_ref[n])
        term = gate * _dot(o_ref[...], wb_ref[n])
        merged = term if merged is None else merged + term
    h = alpha * x + _dot(merged.astype(BF16), wo_ref[...])
    y = _layer_norm(h, g_ref[...], b_ref[...])
    _store_token_tiles(y_ref, y)
    logit_ref[...] = _dot(y.astype(BF16), wr_ref[...])


def _merge(x, branches, wg, bg, wb, wo, g, b, wr, alpha):
    n = x.shape[0]
    tm = TOKEN_TILE
    row = lambda width: pl.BlockSpec((tm, width), lambda i: (i, 0))
    return pl.pallas_call(
        functools.partial(_merge_kernel, alpha=alpha),
        out_shape=[jax.ShapeDtypeStruct((n, TILE_ROWS, LANES), F32), jax.ShapeDtypeStruct((n, LANES), F32)],
        grid=(n // tm,),
        in_specs=[row(D_MODEL)] + [row(HEADS_WIDTH)] * 4
                 + [_const_spec(a.shape) for a in (wg, bg, wb, wo, g, b, wr)],
        out_specs=[pl.BlockSpec((tm, TILE_ROWS, LANES), lambda i: (i, 0, 0)), row(LANES)],
        compiler_params=_params("parallel"),
        name="merge",
    )(x, *branches, wg, bg, wb, wo, g, b, wr)


def _route_kernel(logit_ref, bias_ref, onehot_ref, w_ref):
    scores = jax.nn.sigmoid(logit_ref[...])
    sel = scores + bias_ref[...]
    rows = [sel[e:e + 1, :] for e in range(N_EXPERTS)]
    best = None
    best_group = None
    for g in range(N_GROUPS):
        m = rows[g * EXPERTS_PER_GROUP:(g + 1) * EXPERTS_PER_GROUP]
        top2 = None
        for i, j in GROUP_PAIRS:
            pair = m[i] + m[j]
            top2 = pair if top2 is None else jnp.maximum(top2, pair)
        if best is None:
            best, best_group = top2, jnp.zeros(top2.shape, jnp.int32)
        else:
            better = top2 > best
            best = jnp.where(better, top2, best)
            best_group = jnp.where(better, g, best_group)
    chosen = []
    weights = []
    total = None
    for e in range(N_EXPERTS):
        g, i = divmod(e, EXPERTS_PER_GROUP)
        rank = jnp.zeros(best.shape, jnp.int32)
        for j in range(EXPERTS_PER_GROUP):
            if j == i:
                continue
            other = rows[g * EXPERTS_PER_GROUP + j]
            ahead = (other >= rows[e]) if j < i else (other > rows[e])
            rank = rank + ahead.astype(jnp.int32)
        chosen.append((best_group == g) & (rank < 2))
        w = jnp.where(chosen[e], scores[e:e + 1, :], 0.0)
        weights.append(w)
        total = w if total is None else total + w
    onehot = []
    w_first = jnp.zeros(best.shape, F32)
    w_second = jnp.zeros(best.shape, F32)
    for g in range(N_GROUPS):
        for i, j in GROUP_PAIRS:
            first, second = g * EXPERTS_PER_GROUP + i, g * EXPERTS_PER_GROUP + j
            both = chosen[first] & chosen[second]
            onehot.append(jnp.where(both, 1.0, 0.0))
            w_first = jnp.where(both, weights[first] / total, w_first)
            w_second = jnp.where(both, weights[second] / total, w_second)
    pad = jnp.zeros((onehot_ref.shape[0] - N_BUCKETS, best.shape[1]), F32)
    onehot_ref[...] = jnp.concatenate(onehot + [pad], axis=0)
    w_ref[...] = jnp.concatenate([w_first, w_second], axis=0)


def _route(logits_t, bias_col):
    n = logits_t.shape[1]
    return pl.pallas_call(
        _route_kernel,
        out_shape=[jax.ShapeDtypeStruct((BUCKET_ROWS, n), F32), jax.ShapeDtypeStruct((2, n), F32)],
        compiler_params=pltpu.CompilerParams(vmem_limit_bytes=VMEM_LIMIT),
        name="route",
    )(logits_t, bias_col)


def _experts_kernel(src_ref, dst_ref, ea_ref, eb_ref, used_ref, x_hbm, w_ref, wga_ref, wua_ref, wda_ref,
                    wgb_ref, wub_ref, wdb_ref, g_ref, b_ref, out_hbm, xbuf, ybuf, sem_in, sem_out,
                    *, alpha, tile):
    del ea_ref, eb_ref
    t = pl.program_id(0)
    used = used_ref[0]
    buf = t % 2

    def row_in(tile_idx, r):
        return pltpu.make_async_copy(x_hbm.at[src_ref[tile_idx * tile + r]],
                                     xbuf.at[tile_idx % 2, r], sem_in.at[tile_idx % 2])

    def row_out(tile_idx, r):
        return pltpu.make_async_copy(ybuf.at[r], out_hbm.at[dst_ref[tile_idx * tile + r]], sem_out)

    def each_row(action):
        def body(r, carry):
            action(r)
            return carry
        lax.fori_loop(0, tile, body, 0, unroll=8)

    @pl.when(jnp.logical_and(t == 0, used > 0))
    def _():
        each_row(lambda r: row_in(t, r).start())

    @pl.when(t < used)
    def _():
        @pl.when(t + 1 < used)
        def _():
            each_row(lambda r: row_in(t + 1, r).start())

        each_row(lambda r: row_in(t, r).wait())
        x = _load_token_tiles(xbuf.at[buf])
        xb = x.astype(BF16)

        def expert(wg_ref, wu_ref, wd_ref):
            hidden = jax.nn.silu(_dot(xb, wg_ref[...])) * _dot(xb, wu_ref[...])
            return _dot(hidden.astype(BF16), wd_ref[...])

        w = w_ref[...]
        y = w[:, 0:1] * expert(wga_ref, wua_ref, wda_ref) + w[:, 1:2] * expert(wgb_ref, wub_ref, wdb_ref)
        y = _layer_norm(alpha * x + y, g_ref[...], b_ref[...])

        @pl.when(t > 0)
        def _():
            each_row(lambda r: row_out(t - 1, r).wait())

        _store_token_tiles(ybuf, y)
        each_row(lambda r: row_out(t, r).start())

        @pl.when(t == used - 1)
        def _():
            each_row(lambda r: row_out(t, r).wait())

    @pl.when(t >= used)
    def _():
        ybuf[...] = jnp.zeros(ybuf.shape, F32)
        each_row(lambda r: row_out(t, r).start())
        each_row(lambda r: row_out(t, r).wait())


def _experts(x_tiles, slot_src, slot_dst, w_sorted, first_expert, second_expert, tiles_used,
             wg, wu, wd, g, b, alpha, tile):
    n_slots = slot_src.shape[0]
    idx = lambda fn: (lambda t, tok, live, ea, eb, nu: fn(t, ea, eb))
    up = lambda which: pl.BlockSpec((None, D_MODEL, D_EXPERT), idx(lambda t, ea, eb: ((ea, eb)[which][t], 0, 0)))
    down = lambda which: pl.BlockSpec((None, D_EXPERT, D_MODEL), idx(lambda t, ea, eb: ((ea, eb)[which][t], 0, 0)))
    const = lambda a: pl.BlockSpec(a.shape, idx(lambda t, ea, eb: (0,) * a.ndim))
    return pl.pallas_call(
        functools.partial(_experts_kernel, alpha=alpha, tile=tile),
        out_shape=jax.ShapeDtypeStruct((n_slots,) + x_tiles.shape[1:], F32),
        grid_spec=pltpu.PrefetchScalarGridSpec(
            num_scalar_prefetch=5,
            grid=(n_slots // tile,),
            in_specs=[pl.BlockSpec(memory_space=pl.ANY),
                      pl.BlockSpec((tile, 2), idx(lambda t, ea, eb: (t, 0))),
                      up(0), up(0), down(0), up(1), up(1), down(1), const(g), const(b)],
            out_specs=pl.BlockSpec(memory_space=pl.ANY),
            scratch_shapes=[pltpu.VMEM((2, tile, TILE_ROWS, LANES), F32),
                            pltpu.VMEM((tile, TILE_ROWS, LANES), F32),
                            pltpu.SemaphoreType.DMA((2,)), pltpu.SemaphoreType.DMA]),
        compiler_params=_params("arbitrary"),
        name="experts",
    )(slot_src, slot_dst, first_expert, second_expert, tiles_used, x_tiles, w_sorted,
      wg, wu, wd, wg, wu, wd, g, b)


def _moe(x_tiles, logits, rb, wg, wu, wd, g, b, alpha, tile):
    n = x_tiles.shape[0]
    onehot, w2 = _route(logits.T, rb)
    prefix = _cumsum_rows(onehot)
    counts = prefix[:, -1].astype(jnp.int32)
    padded = (counts + tile - 1) // tile * tile
    ends = jnp.cumsum(padded)
    starts = ends - padded
    rank = jnp.sum(onehot * prefix, axis=0).astype(jnp.int32) - 1
    pos = jnp.sum(onehot * starts.astype(F32)[:, None], axis=0).astype(jnp.int32) + rank
    n_slots = n + N_BUCKETS * tile
    per_token = jnp.stack([jnp.arange(n, dtype=F32), jnp.ones((n,), F32), w2[0], w2[1]], axis=1)
    per_slot = jnp.zeros((n_slots, 4), F32).at[pos].set(per_token, unique_indices=True)
    slot_src = per_slot[:, 0].astype(jnp.int32)
    slot_live = per_slot[:, 1].astype(jnp.int32)
    empties_before = jnp.arange(n_slots, dtype=jnp.int32) + 1 - jnp.cumsum(slot_live) - (1 - slot_live)
    slot_dst = jnp.where(slot_live != 0, slot_src, n + empties_before)
    w_sorted = per_slot[:, 2:]
    tile_start = jnp.arange(n_slots // tile, dtype=jnp.int32) * tile
    bucket = jnp.sum(tile_start[:, None] >= ends[None, :N_BUCKETS], axis=1)
    bucket = jnp.minimum(bucket, N_BUCKETS - 1).astype(jnp.int32)
    group, pair = bucket // len(GROUP_PAIRS), bucket % len(GROUP_PAIRS)
    pair_first = jnp.asarray([p[0] for p in GROUP_PAIRS], jnp.int32)
    pair_second = jnp.asarray([p[1] for p in GROUP_PAIRS], jnp.int32)
    first_expert = group * EXPERTS_PER_GROUP + pair_first[pair]
    second_expert = group * EXPERTS_PER_GROUP + pair_second[pair]
    tiles_used = (ends[N_BUCKETS - 1] // tile).astype(jnp.int32)[None]

    out_tiles = _experts(x_tiles, slot_src, slot_dst, w_sorted, first_expert, second_expert, tiles_used,
                         wg, wu, wd, g, b, alpha, tile)
    return out_tiles[:n].reshape(n, D_MODEL)


IN_SIZES = (HEADS_WIDTH,) * 3 + (N_HEADS, MLA_Q_RANK, MLA_KV_RANK, MLA_ROPE) + (HEADS_WIDTH,) * 6


def _pad_cols(a, width):
    return jnp.pad(a, ((0, 0), (0, width - a.shape[1])))


def _prep_in_proj_weights(w_in, b_forget, q_norm, w_q_up, w_kv_up):
    offs = np.cumsum(IN_SIZES)[:-1].tolist()
    qa, ka, va, fa, cq, ckv, kr, qc, kc, vc, qd, kd, vd = jnp.split(w_in, offs, axis=1)
    w = jnp.concatenate([qa, ka, va, qc, kc, vc, qd, kd, vd, _pad_cols(cq, 256), ckv,
                         _pad_cols(kr, LANES), _pad_cols(fa, LANES)], axis=1).astype(BF16)
    bf = _pad_cols(b_forget[None, :], LANES)
    qn = _pad_cols(q_norm[None, :], 256)
    wq3 = w_q_up.reshape(MLA_Q_RANK, N_HEADS, MLA_NOPE + MLA_ROPE)
    wq = jnp.concatenate([wq3[:, :, :MLA_NOPE].reshape(MLA_Q_RANK, -1),
                          wq3[:, :, MLA_NOPE:].reshape(MLA_Q_RANK, -1)], axis=1)
    wq = jnp.pad(wq, ((0, 256 - MLA_Q_RANK), (0, 0))).astype(BF16)
    wkv = w_kv_up.reshape(MLA_KV_RANK, N_HEADS, MLA_NOPE + MLA_V)
    wcat = jnp.zeros((HEADS_WIDTH + N_HEADS * MLA_ROPE, N_HEADS * Q_CAT), F32)
    wv_pad = jnp.zeros((N_HEADS, Q_CAT, HEADS_WIDTH), F32)
    eye = jnp.eye(MLA_ROPE, dtype=F32)
    for h in range(N_HEADS):
        wcat = wcat.at[h * MLA_NOPE:(h + 1) * MLA_NOPE, h * Q_CAT:h * Q_CAT + MLA_KV_RANK].set(
            wkv[:, h, :MLA_NOPE].T)
        wcat = wcat.at[HEADS_WIDTH + h * MLA_ROPE:HEADS_WIDTH + (h + 1) * MLA_ROPE,
                       h * Q_CAT + MLA_KV_RANK:h * Q_CAT + MLA_KV_RANK + MLA_ROPE].set(eye)
        wv_pad = wv_pad.at[h, :MLA_KV_RANK, h * MLA_V:(h + 1) * MLA_V].set(wkv[:, h, MLA_NOPE:])
    return w, bf, qn, wq, wcat.astype(BF16), wv_pad.astype(BF16)


def _rope_tables(pos):
    half = MLA_ROPE // 2
    inv = ROPE_THETA ** (-jnp.arange(half, dtype=F32) / half)
    ang = pos.astype(F32)[:, None] * inv[None, :]
    cos = jnp.cos(ang)
    sin = jnp.sin(ang)
    reps = LANES // MLA_ROPE
    return (jnp.tile(jnp.concatenate([cos, cos], axis=1), (1, reps)),
            jnp.tile(jnp.concatenate([-sin, sin], axis=1), (1, reps)))


def _band_bias(rel_bias, q0, nq, k0, nk):
    period = nq + nk
    j = np.arange(period)
    d = np.where(j < nk, j, j - period)
    idx = np.clip(q0 - k0 - d, -REL_CLIP, REL_CLIP) + REL_CLIP
    u = rel_bias[:, idx]
    rows = jnp.tile(u, (1, nq))[:, :nq * (period - 1)].reshape(rel_bias.shape[0], nq, period - 1)
    return rows[:, :, :nk]


def _key_major(cache):
    l, b, p, h, d = cache.shape
    return jnp.transpose(cache, (0, 1, 3, 4, 2)).reshape(l, b, h * d, p)


def _from_key_major(state, n_heads):
    b, hd, t = state.shape
    return jnp.transpose(state.reshape(b, n_heads, hd // n_heads, t), (0, 3, 1, 2))


def kernel(x_prompt, x_sample, cache_fox_k, cache_fox_v, cache_fox_logf, cache_mla_latent, cache_mla_krope, cache_band_k, cache_band_v, cache_sb_k, cache_sb_v, w_in, b_forget, mla_q_norm, w_q_up, mla_kv_norm, w_kv_up, rel_bias, w_branch, w_gate, b_gate, w_out, ln1_g, ln1_b, w_router, router_bias, w_e_gate, w_e_up, w_e_down, ln2_g, ln2_b):
    batch, seq, _ = x_prompt.shape
    dec_batch, t_new, _ = x_sample.shape
    depth = w_in.shape[0]
    past = cache_fox_k.shape[2]
    n_band = cache_band_k.shape[2]
    n_p = batch * seq
    n_s = dec_batch * t_new
    alpha = (2 * depth) ** 0.25
    band_tile = BAND_CHUNKS * CHUNK
    assert seq % ATT_TILE == 0 and seq % band_tile == 0 and seq % TOKEN_TILE == 0
    assert n_p % MOE_TILE == 0 and n_s % MOE_TILE_SAMPLE == 0 and n_s % TOKEN_TILE == 0
    assert past % LANES == 0 and t_new <= LANES and t_new % 16 == 0 and ATT_TILE % SB_SUB == 0
    assert past % min(SAMPLE_KEY_TILE, past) == 0 and past % SB_SUB == 0

    xp = x_prompt.reshape(n_p, D_MODEL)
    xs = x_sample.reshape(n_s, D_MODEL)
    rope_p = _rope_tables(jnp.tile(jnp.arange(seq, dtype=jnp.int32), batch))
    rope_s = _rope_tables(jnp.tile(past + jnp.arange(t_new, dtype=jnp.int32), dec_batch))

    c_fox_k, c_fox_v = _key_major(cache_fox_k), _key_major(cache_fox_v)
    c_band_k, c_band_v = _key_major(cache_band_k), _key_major(cache_band_v)
    c_sb_k, c_sb_v = _key_major(cache_sb_k), _key_major(cache_sb_v)
    c_logf = jnp.swapaxes(cache_fox_logf, 2, 3).astype(F32)
    c_krope = jnp.swapaxes(cache_mla_krope, 2, 3)

    wr = _pad_cols(w_router, LANES).astype(BF16)
    rb = router_bias.astype(F32)[:, None]
    total = past + t_new
    padded = -(-total // LANES) * LANES

    prompt_states, sample_states = [], []
    for l in range(depth):
        w, bf, qn, wq, wcat, wv_pad = _prep_in_proj_weights(w_in[l], b_forget[l], mla_q_norm[l],
                                                             w_q_up[l], w_kv_up[l])
        proj_w = (w, bf, qn, wq, mla_kv_norm[l][None, :], wcat)
        (qkv_p, ka_p, va_p, kc_p, vc_p, kd_p, vd_p, logf_p, lat_p, kr_p, kcat_p, qcat_p,
         vat_p, vct_p, vdt_p, latt_p) = _in_proj(xp, *proj_w, *rope_p, batch=batch, seq=seq)
        (qkv_s, ka_s, va_s, kc_s, vc_s, kd_s, vd_s, logf_s, lat_s, kr_s, kcat_s, qcat_s) = _in_proj(
            xs, *proj_w, *rope_s)
        logf_s = logf_s[:, :N_HEADS].reshape(dec_batch, t_new, N_HEADS)
        kr_s = kr_s[:, :MLA_ROPE]

        c_p = _cumsum_rows(logf_p.reshape(batch * N_HEADS, seq)).reshape(batch, N_HEADS, seq)
        all_s = jnp.concatenate([c_logf[l], jnp.swapaxes(logf_s, 1, 2)], axis=2)
        all_s = jnp.pad(all_s, ((0, 0), (0, 0), (0, padded - total)))
        c_s = _cumsum_rows(all_s.reshape(dec_batch * N_HEADS, padded)).reshape(dec_batch, N_HEADS, padded)
        cq_s = c_s[:, :, past:past + t_new].reshape(dec_batch, N_HEADS * t_new, 1)

        bias_p = _band_bias(rel_bias[l], 0, band_tile, -band_tile, 2 * band_tile)
        bias_s = _band_bias(rel_bias[l], past, t_new, past - n_band, n_band + t_new)
        bias_s = bias_s.reshape(N_HEADS * t_new, n_band + t_new)

        branches_p = (_fox_prompt(qkv_p, vat_p, jnp.swapaxes(c_p, 1, 2), c_p, batch, seq),
                      _mla_prompt(qcat_p, kcat_p, latt_p, wv_pad, batch, seq),
                      _band_prompt(qkv_p, vct_p, jnp.swapaxes(bias_p, 1, 2), batch, seq),
                      _sb_prompt(qkv_p, vdt_p, batch, seq))
        branches_s = (_fox_sample(qkv_s, c_fox_k, c_fox_v, cq_s, c_s, l, dec_batch, t_new),
                      _mla_sample(qcat_s, kcat_s, cache_mla_latent, c_krope, wv_pad, l, dec_batch, t_new),
                      _band_sample(qkv_s, c_band_k, c_band_v, bias_s[:, :n_band], bias_s[:, n_band:], l,
                                   past, dec_batch, t_new),
                      _sb_sample(qkv_s, c_sb_k, c_sb_v, l, dec_batch, t_new))

        tail_w = (w_gate[l].astype(BF16), b_gate[l][:, None, :], w_branch[l].astype(BF16),
                  w_out[l].astype(BF16), ln1_g[l][None, :], ln1_b[l][None, :], wr, alpha)
        experts = (w_e_gate[l].astype(BF16), w_e_up[l].astype(BF16), w_e_down[l].astype(BF16),
                   ln2_g[l][None, :], ln2_b[l][None, :], alpha)
        x1_p, logits_p = _merge(xp, branches_p, *tail_w)
        x1_s, logits_s = _merge(xs, branches_s, *tail_w)
        xp = _moe(x1_p, logits_p[:, :N_EXPERTS], rb, *experts, MOE_TILE)
        xs = _moe(x1_s, logits_s[:, :N_EXPERTS], rb, *experts, MOE_TILE_SAMPLE)

        n_keep = min(band_tile, seq)
        prompt_states.append((
            _from_key_major(ka_p, N_HEADS), _from_key_major(va_p, N_HEADS), jnp.swapaxes(logf_p, 1, 2),
            lat_p.reshape(batch, seq, -1), jnp.swapaxes(kr_p, 1, 2),
            _from_key_major(kc_p[:, :, seq - n_keep:], N_HEADS),
            _from_key_major(vc_p[:, :, seq - n_keep:], N_HEADS),
            _from_key_major(kd_p, N_HEADS), _from_key_major(vd_p, N_HEADS)))
        heads = lambda a: a.reshape(dec_batch, t_new, N_HEADS, HEAD_DIM)
        sample_states.append((
            heads(ka_s), heads(va_s), logf_s, lat_s.reshape(dec_batch, t_new, -1),
            kr_s.reshape(dec_batch, t_new, -1), heads(kc_s), heads(vc_s), heads(kd_s), heads(vd_s)))

    stack = lambda states: [jnp.stack(z, axis=0) for z in zip(*states)]
    return (xp.reshape(batch, seq, D_MODEL), xs.reshape(dec_batch, t_new, D_MODEL),
            *stack(prompt_states), *stack(sample_states))
```

```python
import functools

import numpy as np
import jax
import jax.numpy as jnp
from jax import lax
from jax.experimental import pallas as pl
from jax.experimental.pallas import tpu as pltpu

F32 = jnp.float32
BF16 = jnp.bfloat16

D_MODEL = 1024
N_HEADS = 4
HEAD_DIM = 64
HEADS_WIDTH = N_HEADS * HEAD_DIM
CHUNK = 64
BAND_CHUNKS = 8
REL_CLIP = 128
MLA_NOPE = 64
MLA_ROPE = 32
MLA_V = 64
MLA_Q_RANK = 192
MLA_KV_RANK = 128
ROPE_THETA = 10000.0
N_BRANCH = 4
N_EXPERTS = 16
N_GROUPS = 4
EXPERTS_PER_GROUP = N_EXPERTS // N_GROUPS
D_EXPERT = 512
GROUP_PAIRS = tuple((i, j) for i in range(EXPERTS_PER_GROUP) for j in range(i + 1, EXPERTS_PER_GROUP))
N_BUCKETS = N_GROUPS * len(GROUP_PAIRS)
BUCKET_ROWS = 32
NEG_INF = -1e30
LOG2E = 1.4426950408889634
LANES = 128
TILE_ROWS = D_MODEL // LANES
VMEM_LIMIT = 56 * 1024 * 1024

QKV_WIDTH = 9 * HEADS_WIDTH
STATE_COLS = (1, 2, 4, 5, 7, 8)
VALUE_COLS = (2, 5, 8)
OFF_CQ = QKV_WIDTH
OFF_CKV = OFF_CQ + 256
OFF_KR = OFF_CKV + MLA_KV_RANK
OFF_FA = OFF_KR + LANES
PROJ_WIDTH = OFF_FA + LANES
Q_CAT = 256

TOKEN_TILE = 512
MOE_TILE = 256
MOE_TILE_SAMPLE = 128
ATT_TILE = 512
SB_SUB = 256
SAMPLE_KEY_TILE = 4096
SB_DEAD = -105.0


def _dot(a, b):
    return jnp.dot(a, b, preferred_element_type=F32)


def _dot_nt(a, b):
    return lax.dot_general(a, b, (((1,), (1,)), ((), ())), preferred_element_type=F32)


def _const_spec(shape):
    nd = len(shape)
    return pl.BlockSpec(shape, lambda *_: (0,) * nd, pipeline_mode=pl.Buffered(1))


def _params(*sem):
    return pltpu.CompilerParams(dimension_semantics=sem, vmem_limit_bytes=VMEM_LIMIT)


def _log_sigmoid(x):
    return jnp.minimum(x, 0.0) - jnp.log1p(jnp.exp(-jnp.abs(x)))


def _softplus(x):
    return jnp.maximum(x, 0.0) + jnp.log1p(jnp.exp(-jnp.abs(x)))


def _layer_norm(h, g, b):
    mu = jnp.mean(h, axis=-1, keepdims=True)
    d = h - mu
    var = jnp.mean(d * d, axis=-1, keepdims=True)
    return d * lax.rsqrt(var + 1e-5) * g + b


def _head_of_lane(shape):
    return lax.broadcasted_iota(jnp.int32, shape, len(shape) - 1) // HEAD_DIM


def _in_proj_kernel(x_ref, w_ref, bf_ref, qn_ref, wq_ref, kvn_ref, wcat_ref, cos_ref, sin_ref,
                    qkv_ref, ka_ref, va_ref, kc_ref, vc_ref, kd_ref, vd_ref,
                    logf_ref, lat_ref, kr_ref, kcat_ref, qcat_ref, *key_major_refs, key_major):
    xb = x_ref[...].astype(BF16)
    proj = _dot(xb, w_ref[...])
    qkv_ref[...] = proj[:, :QKV_WIDTH].astype(BF16)
    for ref, col in zip((ka_ref, va_ref, kc_ref, vc_ref, kd_ref, vd_ref), STATE_COLS):
        state = proj[:, col * HEADS_WIDTH:(col + 1) * HEADS_WIDTH]
        if key_major:
            state = state.T
            if col in VALUE_COLS:
                key_major_refs[VALUE_COLS.index(col)][...] = state.astype(BF16)
        ref[...] = state

    cos = cos_ref[...]
    sin = sin_ref[...]
    lane = lax.broadcasted_iota(jnp.int32, cos.shape, 1)
    first_half = (lane % MLA_ROPE) < (MLA_ROPE // 2)

    def rope(v):
        partner = jnp.where(first_half, pltpu.roll(v, LANES - MLA_ROPE // 2, 1),
                            pltpu.roll(v, MLA_ROPE // 2, 1))
        return v * cos + partner * sin

    cq = proj[:, OFF_CQ:OFF_CQ + 256]
    ms = jnp.sum(cq * cq, axis=-1, keepdims=True) * (1.0 / MLA_Q_RANK)
    cqn = cq * lax.rsqrt(ms + 1e-6) * qn_ref[...]
    qb = _dot(cqn.astype(BF16), wq_ref[...])
    q_rope = rope(qb[:, HEADS_WIDTH:])
    q_in = jnp.concatenate([qb[:, :HEADS_WIDTH], q_rope], axis=1) * ((MLA_NOPE + MLA_ROPE) ** -0.5 * LOG2E)
    qcat_ref[...] = _dot(q_in.astype(BF16), wcat_ref[...]).astype(BF16)

    ckv = proj[:, OFF_CKV:OFF_CKV + MLA_KV_RANK]
    lat = ckv * lax.rsqrt(jnp.mean(ckv * ckv, axis=-1, keepdims=True) + 1e-6) * kvn_ref[...]
    lat_ref[...] = lat
    kr = rope(proj[:, OFF_KR:OFF_KR + LANES])
    kcat_ref[...] = jnp.concatenate([lat, kr], axis=1).astype(BF16)
    logf = _log_sigmoid(proj[:, OFF_FA:OFF_FA + LANES] + bf_ref[...])
    if key_major:
        kr_ref[...] = kr.T[:MLA_ROPE, :]
        logf_ref[...] = logf.T[:N_HEADS, :]
        key_major_refs[len(VALUE_COLS)][...] = lat.T.astype(BF16)
    else:
        kr_ref[...] = kr
        logf_ref[...] = logf


def _in_proj(x, w, bf, qn, wq, kvn, wcat, cos, sin, *, batch=None, seq=None):
    n = x.shape[0]
    tm = TOKEN_TILE
    key_major = batch is not None
    row = lambda width: pl.BlockSpec((tm, width), lambda i: (i, 0))
    if key_major:
        per = seq // tm
        tr = lambda height: pl.BlockSpec((None, height, tm), lambda i: (i // per, 0, i % per))
        tshape = lambda height: jax.ShapeDtypeStruct((batch, height, seq), F32)
        state_shapes = [tshape(HEADS_WIDTH)] * 6 + [tshape(N_HEADS), jax.ShapeDtypeStruct((n, LANES), F32),
                                                    tshape(MLA_ROPE)]
        state_specs = [tr(HEADS_WIDTH)] * 6 + [tr(N_HEADS), row(LANES), tr(MLA_ROPE)]
    else:
        state_shapes = ([jax.ShapeDtypeStruct((n, HEADS_WIDTH), F32)] * 6
                        + [jax.ShapeDtypeStruct((n, LANES), F32)] * 3)
        state_specs = [row(HEADS_WIDTH)] * 6 + [row(LANES)] * 3
    out_shape = ([jax.ShapeDtypeStruct((n, QKV_WIDTH), BF16)] + state_shapes
                 + [jax.ShapeDtypeStruct((n, 2 * LANES), BF16),
                    jax.ShapeDtypeStruct((n, N_HEADS * Q_CAT), BF16)])
    out_specs = [row(QKV_WIDTH)] + state_specs + [row(2 * LANES), row(N_HEADS * Q_CAT)]
    if key_major:
        out_shape += ([jax.ShapeDtypeStruct((batch, HEADS_WIDTH, seq), BF16)] * len(VALUE_COLS)
                      + [jax.ShapeDtypeStruct((batch, MLA_KV_RANK, seq), BF16)])
        out_specs += [tr(HEADS_WIDTH)] * len(VALUE_COLS) + [tr(MLA_KV_RANK)]
    return pl.pallas_call(
        functools.partial(_in_proj_kernel, key_major=key_major),
        out_shape=out_shape,
        grid=(n // tm,),
        in_specs=[row(D_MODEL), _const_spec(w.shape), _const_spec(bf.shape), _const_spec(qn.shape),
                  _const_spec(wq.shape), _const_spec(kvn.shape), _const_spec(wcat.shape),
                  row(LANES), row(LANES)],
        out_specs=out_specs,
        compiler_params=_params("parallel"),
        name="in_proj_prompt" if key_major else "in_proj_sample",
    )(x, w, bf, qn, wq, kvn, wcat, cos, sin)


def _cumsum_kernel(x_ref, o_ref):
    r, t = x_ref.shape
    row = lax.broadcasted_iota(jnp.int32, (LANES, LANES), 0)
    col = lax.broadcasted_iota(jnp.int32, (LANES, LANES), 1)
    tri = jnp.where(row <= col, 1.0, 0.0).astype(BF16)
    carry = jnp.zeros((r, 1), F32)
    for c in range(t // LANES):
        xc = x_ref[:, c * LANES:(c + 1) * LANES]
        hi = xc.astype(BF16)
        r1 = xc - hi.astype(F32)
        mid = r1.astype(BF16)
        lo = (r1 - mid.astype(F32)).astype(BF16)
        y = _dot(hi, tri) + _dot(mid, tri) + _dot(lo, tri) + carry
        o_ref[:, c * LANES:(c + 1) * LANES] = y
        carry = y[:, LANES - 1:LANES]


def _cumsum_rows(x):
    return pl.pallas_call(
        _cumsum_kernel,
        out_shape=jax.ShapeDtypeStruct(x.shape, F32),
        name="cumsum",
    )(x)


def _online_softmax_update(s, pv, m_ref, l_ref, acc_ref, h, exp=jnp.exp):
    m_prev = m_ref[h]
    m_new = jnp.maximum(m_prev, jnp.max(s, axis=1, keepdims=True))
    alpha = exp(m_prev - m_new)
    p = exp(s - m_new)
    l_ref[h] = alpha * l_ref[h] + jnp.sum(p, axis=1, keepdims=True)
    acc_ref[h] = alpha * acc_ref[h] + pv(p.astype(BF16))
    m_ref[h] = m_new


def _init_softmax_state(m_ref, l_ref, acc_ref):
    m_ref[...] = jnp.full(m_ref.shape, NEG_INF, F32)
    l_ref[...] = jnp.zeros(l_ref.shape, F32)
    acc_ref[...] = jnp.zeros(acc_ref.shape, F32)


def _store_head_masked_q(q_ref, qs_ref, rows):
    q = q_ref[...] * (HEAD_DIM ** -0.5)
    head = _head_of_lane(q.shape)
    for h in range(N_HEADS):
        qs_ref[h * rows:(h + 1) * rows, :] = jnp.where(head == h, q, jnp.zeros_like(q))


def _merge_heads(per_head):
    head = _head_of_lane(per_head[0].shape)
    out = jnp.zeros_like(per_head[0])
    for h in range(N_HEADS):
        out = jnp.where(head == h, per_head[h], out)
    return out


def _online_softmax_update_t(st, pv, m_ref, l_ref, acc_ref, h):
    m_prev = m_ref[h]
    m_new = jnp.maximum(m_prev, jnp.max(st, axis=0, keepdims=True))
    alpha = jnp.exp2(m_prev - m_new)
    p = jnp.exp2(st - m_new)
    l_ref[h] = alpha * l_ref[h] + jnp.sum(p, axis=0, keepdims=True)
    acc_ref[h] = alpha * acc_ref[h] + pv(p.astype(BF16))
    m_ref[h] = m_new


def _fox_prompt_kernel(q_ref, k_ref, vt_ref, cq_ref, ck_ref, o_ref, qt_ref, m_ref, l_ref, acc_ref):
    qi = pl.program_id(1)
    ki = pl.program_id(2)

    @pl.when(ki == 0)
    def _():
        _init_softmax_state(m_ref, l_ref, acc_ref)
        _store_head_masked_qt(q_ref, qt_ref, HEAD_DIM ** -0.5 * LOG2E)

    def step(diagonal):
        k = k_ref[...]
        vt = vt_ref[...]
        cq = cq_ref[...] * LOG2E
        ck = ck_ref[...] * LOG2E
        for h in range(N_HEADS):
            st = _dot(k, qt_ref[h]) + cq[h:h + 1, :] - ck[:, h:h + 1]
            if diagonal:
                key = lax.broadcasted_iota(jnp.int32, st.shape, 0)
                qry = lax.broadcasted_iota(jnp.int32, st.shape, 1)
                st = jnp.where(key <= qry, st, NEG_INF)
            vt_h = vt[h * HEAD_DIM:(h + 1) * HEAD_DIM, :]
            _online_softmax_update_t(st, lambda p: _dot(vt_h, p), m_ref, l_ref, acc_ref, h)

    pl.when(ki < qi)(lambda: step(False))
    pl.when(ki == qi)(lambda: step(True))

    @pl.when(ki == pl.num_programs(2) - 1)
    def _():
        ot = jnp.concatenate([acc_ref[h] / l_ref[h] for h in range(N_HEADS)], axis=0)
        o_ref[...] = ot.T.astype(BF16)


def _fox_prompt(qkv, vt, c_tok, c_head, batch, seq):
    t = ATT_TILE
    nt = seq // t
    return pl.pallas_call(
        _fox_prompt_kernel,
        out_shape=jax.ShapeDtypeStruct((batch * seq, HEADS_WIDTH), BF16),
        grid=(batch, nt, nt),
        in_specs=[pl.BlockSpec((t, HEADS_WIDTH), lambda b, i, j: (b * nt + i, 0)),
                  pl.BlockSpec((t, HEADS_WIDTH), lambda b, i, j: (b * nt + jnp.minimum(i, j), 1)),
                  pl.BlockSpec((None, HEADS_WIDTH, t), lambda b, i, j: (b, 0, jnp.minimum(i, j))),
                  pl.BlockSpec((None, N_HEADS, t), lambda b, i, j: (b, 0, i)),
                  pl.BlockSpec((None, t, N_HEADS), lambda b, i, j: (b, jnp.minimum(i, j), 0))],
        out_specs=pl.BlockSpec((t, HEADS_WIDTH), lambda b, i, j: (b * nt + i, 0)),
        scratch_shapes=[pltpu.VMEM((N_HEADS, HEADS_WIDTH, t), BF16),
                        pltpu.VMEM((N_HEADS, 1, t), F32), pltpu.VMEM((N_HEADS, 1, t), F32),
                        pltpu.VMEM((N_HEADS, HEAD_DIM, t), F32)],
        compiler_params=_params("parallel", "parallel", "arbitrary"),
        name="fox_prompt",
    )(qkv, qkv, vt, c_head, c_tok)


def _mla_prompt_kernel(q_ref, k_ref, latt_ref, wv_ref, o_ref, qt_ref, m_ref, l_ref, acc_ref):
    qi = pl.program_id(1)
    ki = pl.program_id(2)

    @pl.when(ki == 0)
    def _():
        _init_softmax_state(m_ref, l_ref, acc_ref)
        for h in range(N_HEADS):
            qt_ref[h] = q_ref[:, h * Q_CAT:(h + 1) * Q_CAT].astype(F32).T.astype(BF16)

    def step(diagonal):
        k = k_ref[...]
        latt = latt_ref[...]
        for h in range(N_HEADS):
            st = _dot(k, qt_ref[h])
            if diagonal:
                key = lax.broadcasted_iota(jnp.int32, st.shape, 0)
                qry = lax.broadcasted_iota(jnp.int32, st.shape, 1)
                st = jnp.where(key // CHUNK <= qry // CHUNK, st, NEG_INF)
            _online_softmax_update_t(st, lambda p: _dot(latt, p), m_ref, l_ref, acc_ref, h)

    pl.when(ki < qi)(lambda: step(False))
    pl.when(ki == qi)(lambda: step(True))

    @pl.when(ki == pl.num_programs(2) - 1)
    def _():
        out = None
        for h in range(N_HEADS):
            o_h = (acc_ref[h] / l_ref[h]).T.astype(BF16)
            term = _dot(o_h, wv_ref[h, :MLA_KV_RANK, :])
            out = term if out is None else out + term
        o_ref[...] = out.astype(BF16)


def _mla_prompt(qcat, kcat, latt, wv_pad, batch, seq):
    t = ATT_TILE
    nt = seq // t
    return pl.pallas_call(
        _mla_prompt_kernel,
        out_shape=jax.ShapeDtypeStruct((batch * seq, HEADS_WIDTH), BF16),
        grid=(batch, nt, nt),
        in_specs=[pl.BlockSpec((t, N_HEADS * Q_CAT), lambda b, i, j: (b * nt + i, 0)),
                  pl.BlockSpec((t, Q_CAT), lambda b, i, j: (b * nt + jnp.minimum(i, j), 0)),
                  pl.BlockSpec((None, MLA_KV_RANK, t), lambda b, i, j: (b, 0, jnp.minimum(i, j))),
                  _const_spec(wv_pad.shape)],
        out_specs=pl.BlockSpec((t, HEADS_WIDTH), lambda b, i, j: (b * nt + i, 0)),
        scratch_shapes=[pltpu.VMEM((N_HEADS, Q_CAT, t), BF16),
                        pltpu.VMEM((N_HEADS, 1, t), F32), pltpu.VMEM((N_HEADS, 1, t), F32),
                        pltpu.VMEM((N_HEADS, MLA_KV_RANK, t), F32)],
        compiler_params=_params("parallel", "parallel", "arbitrary"),
        name="mla_prompt",
    )(qcat, kcat, latt, wv_pad)


def _strict_upper_ones(n):
    row = lax.broadcasted_iota(jnp.int32, (n, n), 0)
    col = lax.broadcasted_iota(jnp.int32, (n, n), 1)
    return jnp.where(row > col, 1.0, 0.0).astype(BF16)


def _sb_block(z, u, rest, mask):
    sp = _softplus(z)
    log_1m = -sp
    if mask is not None:
        log_1m = jnp.where(mask, log_1m, 0.0)
    hi = log_1m.astype(BF16)
    lo = (log_1m - hi.astype(F32)).astype(BF16)
    later = _dot(hi, u) + _dot(lo, u) + rest
    a = jnp.exp(z - sp + later)
    if mask is not None:
        a = jnp.where(mask, a, 0.0)
    return a.astype(BF16), rest + jnp.sum(log_1m, axis=1, keepdims=True)


def _store_head_masked_qt(q_ref, qt_ref, scale):
    qt = q_ref[...].astype(F32).T * scale
    head = lax.broadcasted_iota(jnp.int32, qt.shape, 0) // HEAD_DIM
    for h in range(N_HEADS):
        qt_ref[h] = jnp.where(head == h, qt, 0.0).astype(BF16)


def _sb_block_t(zt, u, rest, mask):
    sp = _softplus(zt)
    log_1m = -sp
    if mask is not None:
        log_1m = jnp.where(mask, log_1m, 0.0)
    hi = log_1m.astype(BF16)
    lo = (log_1m - hi.astype(F32)).astype(BF16)
    later = _dot(u, hi) + _dot(u, lo) + rest
    a = jnp.exp(zt - sp + later)
    if mask is not None:
        a = jnp.where(mask, a, 0.0)
    return a.astype(BF16), rest + jnp.sum(log_1m, axis=0, keepdims=True)


def _sb_prompt_kernel(q_ref, k_ref, vt_ref, o_ref, qt_ref, u_ref, rest_ref, acc_ref):
    tq = q_ref.shape[0]
    qi = pl.program_id(1)
    _store_head_masked_qt(q_ref, qt_ref, HEAD_DIM ** -0.5)
    key_s = lax.broadcasted_iota(jnp.int32, (SB_SUB, SB_SUB), 0)
    key_j = lax.broadcasted_iota(jnp.int32, (SB_SUB, SB_SUB), 1)
    u_ref[...] = jnp.where(key_j > key_s, 1.0, 0.0).astype(BF16)
    rest_ref[...] = jnp.zeros(rest_ref.shape, F32)
    acc_ref[...] = jnp.zeros(acc_ref.shape, F32)

    def visit(start, first_key):
        k = k_ref[pl.ds(start, SB_SUB), :]
        vt = vt_ref[:, pl.ds(start, SB_SUB)]
        u = u_ref[...]
        live = None
        for h in range(N_HEADS):
            zt = _dot(k, qt_ref[h])
            mask = None
            if first_key is not None:
                key = lax.broadcasted_iota(jnp.int32, zt.shape, 0) + first_key
                qry = lax.broadcasted_iota(jnp.int32, zt.shape, 1)
                mask = key < qry
            a, rest = _sb_block_t(zt, u, rest_ref[h], mask)
            acc_ref[h] += _dot(vt[h * HEAD_DIM:(h + 1) * HEAD_DIM, :], a)
            rest_ref[h] = rest
            top = jnp.max(rest)
            live = top if live is None else jnp.maximum(live, top)
        return live

    live = None
    for sub in reversed(range(tq // SB_SUB)):
        live = visit(pl.multiple_of(qi * tq + sub * SB_SUB, SB_SUB), sub * SB_SUB)

    def more(carry):
        return jnp.logical_and(carry[0] >= 0, carry[1] > SB_DEAD)

    def older(carry):
        return carry[0] - 1, visit(pl.multiple_of(carry[0] * SB_SUB, SB_SUB), None)

    lax.while_loop(more, older, (qi * (tq // SB_SUB) - 1, live))
    o_ref[...] = jnp.concatenate([acc_ref[h] for h in range(N_HEADS)], axis=0).T.astype(BF16)


def _sb_prompt(qkv, vt, batch, seq):
    t = ATT_TILE
    nt = seq // t
    return pl.pallas_call(
        _sb_prompt_kernel,
        out_shape=jax.ShapeDtypeStruct((batch * seq, HEADS_WIDTH), BF16),
        grid=(batch, nt),
        in_specs=[pl.BlockSpec((t, HEADS_WIDTH), lambda b, i: (b * nt + i, 6)),
                  pl.BlockSpec((seq, HEADS_WIDTH), lambda b, i: (b, 7)),
                  pl.BlockSpec((None, HEADS_WIDTH, seq), lambda b, i: (b, 0, 0))],
        out_specs=pl.BlockSpec((t, HEADS_WIDTH), lambda b, i: (b * nt + i, 0)),
        scratch_shapes=[pltpu.VMEM((N_HEADS, HEADS_WIDTH, t), BF16),
                        pltpu.VMEM((SB_SUB, SB_SUB), BF16),
                        pltpu.VMEM((N_HEADS, 1, t), F32),
                        pltpu.VMEM((N_HEADS, HEAD_DIM, t), F32)],
        compiler_params=_params("parallel", "arbitrary"),
        name="sb_prompt",
    )(qkv, qkv, vt)


def _band_prompt_kernel(q_ref, kp_ref, kc_ref, vtp_ref, vtc_ref, bias_ref, o_ref, qt_ref):
    tq = q_ref.shape[0]
    qi = pl.program_id(1)
    _store_head_masked_qt(q_ref, qt_ref, HEAD_DIM ** -0.5 * LOG2E)
    k = jnp.concatenate([kp_ref[...], kc_ref[...]], axis=0)
    vt = jnp.concatenate([vtp_ref[...], vtc_ref[...]], axis=1)
    key = lax.broadcasted_iota(jnp.int32, (2 * tq, tq), 0)
    qry_chunk = lax.broadcasted_iota(jnp.int32, (2 * tq, tq), 1) // CHUNK
    key_chunk = key // CHUNK - tq // CHUNK
    valid = (key_chunk <= qry_chunk) & (key_chunk >= qry_chunk - BAND_CHUNKS)
    valid = valid & ((key >= tq) | (qi > 0))
    outs = []
    for h in range(N_HEADS):
        st = _dot(k, qt_ref[h]) + bias_ref[h]
        st = jnp.where(valid, st, NEG_INF)
        p = jnp.exp2(st - jnp.max(st, axis=0, keepdims=True))
        denom = jnp.sum(p, axis=0, keepdims=True)
        outs.append(_dot(vt[h * HEAD_DIM:(h + 1) * HEAD_DIM, :], p.astype(BF16)) / denom)
    o_ref[...] = jnp.concatenate(outs, axis=0).T.astype(BF16)


def _band_prompt(qkv, vt, bias_tile, batch, seq):
    t = BAND_CHUNKS * CHUNK
    nt = seq // t
    prev = lambda i: jnp.maximum(i - 1, 0)
    return pl.pallas_call(
        _band_prompt_kernel,
        out_shape=jax.ShapeDtypeStruct((batch * seq, HEADS_WIDTH), BF16),
        grid=(batch, nt),
        in_specs=[pl.BlockSpec((t, HEADS_WIDTH), lambda b, i: (b * nt + i, 3)),
                  pl.BlockSpec((t, HEADS_WIDTH), lambda b, i: (b * nt + prev(i), 4)),
                  pl.BlockSpec((t, HEADS_WIDTH), lambda b, i: (b * nt + i, 4)),
                  pl.BlockSpec((None, HEADS_WIDTH, t), lambda b, i: (b, 0, prev(i))),
                  pl.BlockSpec((None, HEADS_WIDTH, t), lambda b, i: (b, 0, i)),
                  _const_spec(bias_tile.shape)],
        out_specs=pl.BlockSpec((t, HEADS_WIDTH), lambda b, i: (b * nt + i, 0)),
        scratch_shapes=[pltpu.VMEM((N_HEADS, HEADS_WIDTH, t), BF16)],
        compiler_params=_params("parallel", "parallel"),
        name="band_prompt",
    )(qkv, qkv, qkv, vt, vt, bias_tile)


def _rows_per_head(x, rows):
    return jnp.concatenate([jnp.broadcast_to(x[h:h + 1, :], (rows, x.shape[1])) for h in range(N_HEADS)],
                           axis=0)


def _unstack_heads(acc, rows):
    return _merge_heads([acc[h * rows:(h + 1) * rows, :] for h in range(N_HEADS)])


def _fox_sample_kernel(q_ref, kc_ref, vc_ref, kn_ref, vn_ref, cq_ref, ckc_ref, ckn_ref, o_ref,
                       qs_ref, m_ref, l_ref, acc_ref):
    t = q_ref.shape[0]
    ki = pl.program_id(1)

    @pl.when(ki == 0)
    def _():
        _init_softmax_state(m_ref, l_ref, acc_ref)
        _store_head_masked_q(q_ref, qs_ref, t)

    qs = qs_ref[...]
    cq = cq_ref[...]
    s = _dot(qs, kc_ref[...].astype(BF16)) + cq - _rows_per_head(ckc_ref[...], t)
    vc = vc_ref[...].astype(BF16)
    _online_softmax_update(s, lambda p: _dot_nt(p, vc), m_ref, l_ref, acc_ref, 0)

    @pl.when(ki == pl.num_programs(1) - 1)
    def _():
        s = _dot_nt(qs, kn_ref[...]) + cq - _rows_per_head(ckn_ref[:, :t], t)
        row = lax.broadcasted_iota(jnp.int32, s.shape, 0) % t
        col = lax.broadcasted_iota(jnp.int32, s.shape, 1)
        s = jnp.where(col <= row, s, NEG_INF)
        _online_softmax_update(s, lambda p: _dot(p, vn_ref[...]), m_ref, l_ref, acc_ref, 0)
        o_ref[...] = _unstack_heads(acc_ref[0] / l_ref[0], t).astype(BF16)


def _fox_sample(qkv, cache_kt, cache_vt, cq_rows, c_head, layer, dec_batch, t):
    past = cache_kt.shape[3]
    tk = min(SAMPLE_KEY_TILE, past)
    nk = past // tk
    return pl.pallas_call(
        _fox_sample_kernel,
        out_shape=jax.ShapeDtypeStruct((dec_batch * t, HEADS_WIDTH), BF16),
        grid=(dec_batch, nk),
        in_specs=[pl.BlockSpec((t, HEADS_WIDTH), lambda b, j: (b, 0)),
                  pl.BlockSpec((None, None, HEADS_WIDTH, tk), lambda b, j: (layer, b, 0, j)),
                  pl.BlockSpec((None, None, HEADS_WIDTH, tk), lambda b, j: (layer, b, 0, j)),
                  pl.BlockSpec((t, HEADS_WIDTH), lambda b, j: (b, 1)),
                  pl.BlockSpec((t, HEADS_WIDTH), lambda b, j: (b, 2)),
                  pl.BlockSpec((None, N_HEADS * t, 1), lambda b, j: (b, 0, 0)),
                  pl.BlockSpec((None, N_HEADS, tk), lambda b, j: (b, 0, j)),
                  pl.BlockSpec((None, N_HEADS, LANES), lambda b, j: (b, 0, past // LANES))],
        out_specs=pl.BlockSpec((t, HEADS_WIDTH), lambda b, j: (b, 0)),
        scratch_shapes=[pltpu.VMEM((N_HEADS * t, HEADS_WIDTH), BF16),
                        pltpu.VMEM((1, N_HEADS * t, 1), F32), pltpu.VMEM((1, N_HEADS * t, 1), F32),
                        pltpu.VMEM((1, N_HEADS * t, HEADS_WIDTH), F32)],
        compiler_params=_params("parallel", "arbitrary"),
        name="fox_sample",
    )(qkv, cache_kt, cache_vt, qkv, qkv, cq_rows, c_head, c_head)


def _mla_sample_kernel(q_ref, latc_ref, krc_ref, kn_ref, wv_ref, o_ref,
                       qa_ref, qr_ref, m_ref, l_ref, acc_ref, *, past):
    t = q_ref.shape[0]
    ki = pl.program_id(1)

    @pl.when(ki == 0)
    def _():
        _init_softmax_state(m_ref, l_ref, acc_ref)
        for h in range(N_HEADS):
            qa_ref[h * t:(h + 1) * t, :] = q_ref[:, h * Q_CAT:h * Q_CAT + MLA_KV_RANK]
            qr_ref[h * t:(h + 1) * t, :] = q_ref[:, h * Q_CAT + MLA_KV_RANK:
                                                 h * Q_CAT + MLA_KV_RANK + MLA_ROPE]

    lat = latc_ref[...].astype(BF16)
    s = _dot_nt(qa_ref[...], lat) + _dot(qr_ref[...], krc_ref[...].astype(BF16))
    _online_softmax_update(s, lambda p: _dot(p, lat), m_ref, l_ref, acc_ref, 0, exp=jnp.exp2)

    @pl.when(ki == pl.num_programs(1) - 1)
    def _():
        lat_n = kn_ref[:, :MLA_KV_RANK]
        s = _dot_nt(qa_ref[...], lat_n) + _dot_nt(qr_ref[...],
                                                  kn_ref[:, MLA_KV_RANK:MLA_KV_RANK + MLA_ROPE])
        q_pos = past + lax.broadcasted_iota(jnp.int32, s.shape, 0) % t
        k_pos = past + lax.broadcasted_iota(jnp.int32, s.shape, 1)
        s = jnp.where(k_pos // CHUNK <= q_pos // CHUNK, s, NEG_INF)
        _online_softmax_update(s, lambda p: _dot(p, lat_n), m_ref, l_ref, acc_ref, 0, exp=jnp.exp2)
        o = (acc_ref[0] / l_ref[0]).astype(BF16)
        out = None
        for h in range(N_HEADS):
            term = _dot(o[h * t:(h + 1) * t, :], wv_ref[h, :MLA_KV_RANK, :])
            out = term if out is None else out + term
        o_ref[...] = out.astype(BF16)


def _mla_sample(qcat, kcat, cache_lat, cache_krt, wv_pad, layer, dec_batch, t):
    past = cache_lat.shape[2]
    tk = min(SAMPLE_KEY_TILE, past)
    nk = past // tk
    return pl.pallas_call(
        functools.partial(_mla_sample_kernel, past=past),
        out_shape=jax.ShapeDtypeStruct((dec_batch * t, HEADS_WIDTH), BF16),
        grid=(dec_batch, nk),
        in_specs=[pl.BlockSpec((t, N_HEADS * Q_CAT), lambda b, j: (b, 0)),
                  pl.BlockSpec((None, None, tk, MLA_KV_RANK), lambda b, j: (layer, b, j, 0)),
                  pl.BlockSpec((None, None, MLA_ROPE, tk), lambda b, j: (layer, b, 0, j)),
                  pl.BlockSpec((t, Q_CAT), lambda b, j: (b, 0)),
                  _const_spec(wv_pad.shape)],
        out_specs=pl.BlockSpec((t, HEADS_WIDTH), lambda b, j: (b, 0)),
        scratch_shapes=[pltpu.VMEM((N_HEADS * t, MLA_KV_RANK), BF16),
                        pltpu.VMEM((N_HEADS * t, MLA_ROPE), BF16),
                        pltpu.VMEM((1, N_HEADS * t, 1), F32), pltpu.VMEM((1, N_HEADS * t, 1), F32),
                        pltpu.VMEM((1, N_HEADS * t, MLA_KV_RANK), F32)],
        compiler_params=_params("parallel", "arbitrary"),
        name="mla_sample",
    )(qcat, cache_lat, cache_krt, kcat, wv_pad)


def _band_sample_kernel(q_ref, kc_ref, vc_ref, kn_ref, vn_ref, biasc_ref, biasn_ref, o_ref, qs_ref,
                        *, past):
    t = q_ref.shape[0]
    n_band = kc_ref.shape[1]
    _store_head_masked_q(q_ref, qs_ref, t)
    qs = qs_ref[...]

    def masked(s, first_pos):
        q_chunk = (past + lax.broadcasted_iota(jnp.int32, s.shape, 0) % t) // CHUNK
        k_pos = first_pos + lax.broadcasted_iota(jnp.int32, s.shape, 1)
        k_chunk = k_pos // CHUNK
        valid = (k_pos >= 0) & (k_chunk <= q_chunk) & (k_chunk >= q_chunk - BAND_CHUNKS)
        return jnp.where(valid, s, NEG_INF)

    s_c = masked(_dot(qs, kc_ref[...].astype(BF16)) + biasc_ref[...], past - n_band)
    s_n = masked(_dot_nt(qs, kn_ref[...]) + biasn_ref[...], past)
    m = jnp.maximum(jnp.max(s_c, axis=1, keepdims=True), jnp.max(s_n, axis=1, keepdims=True))
    p_c = jnp.exp(s_c - m)
    p_n = jnp.exp(s_n - m)
    denom = jnp.sum(p_c, axis=1, keepdims=True) + jnp.sum(p_n, axis=1, keepdims=True)
    acc = _dot_nt(p_c.astype(BF16), vc_ref[...].astype(BF16)) + _dot(p_n.astype(BF16), vn_ref[...])
    o_ref[...] = _unstack_heads(acc / denom, t).astype(BF16)


def _band_sample(qkv, cache_kt, cache_vt, bias_cache, bias_new, layer, past, dec_batch, t):
    n_band = cache_kt.shape[3]
    return pl.pallas_call(
        functools.partial(_band_sample_kernel, past=past),
        out_shape=jax.ShapeDtypeStruct((dec_batch * t, HEADS_WIDTH), BF16),
        grid=(dec_batch,),
        in_specs=[pl.BlockSpec((t, HEADS_WIDTH), lambda b: (b, 3)),
                  pl.BlockSpec((None, None, HEADS_WIDTH, n_band), lambda b: (layer, b, 0, 0)),
                  pl.BlockSpec((None, None, HEADS_WIDTH, n_band), lambda b: (layer, b, 0, 0)),
                  pl.BlockSpec((t, HEADS_WIDTH), lambda b: (b, 4)),
                  pl.BlockSpec((t, HEADS_WIDTH), lambda b: (b, 5)),
                  _const_spec(bias_cache.shape), _const_spec(bias_new.shape)],
        out_specs=pl.BlockSpec((t, HEADS_WIDTH), lambda b: (b, 0)),
        scratch_shapes=[pltpu.VMEM((N_HEADS * t, HEADS_WIDTH), BF16)],
        compiler_params=_params("parallel"),
        name="band_sample",
    )(qkv, cache_kt, cache_vt, qkv, qkv, bias_cache, bias_new)


def _sb_sample_kernel(q_ref, kc_ref, vc_ref, kn_ref, vn_ref, o_ref, qs_ref, u_ref, rest_ref, acc_ref):
    t = q_ref.shape[0]
    past = kc_ref.shape[1]
    _store_head_masked_q(q_ref, qs_ref, t)
    u = _strict_upper_ones(SB_SUB)
    u_ref[...] = u
    qs = qs_ref[...]

    z = _dot_nt(qs, kn_ref[...])
    row = lax.broadcasted_iota(jnp.int32, z.shape, 0) % t
    col = lax.broadcasted_iota(jnp.int32, z.shape, 1)
    a, rest = _sb_block(z, u[:t, :t], jnp.zeros((z.shape[0], 1), F32), col < row)
    acc_ref[...] = _dot(a, vn_ref[...])
    rest_ref[...] = rest

    def more(carry):
        return jnp.logical_and(carry[0] >= 0, carry[1] > SB_DEAD)

    def older(carry):
        start = pl.multiple_of(carry[0] * SB_SUB, SB_SUB)
        k = kc_ref[:, pl.ds(start, SB_SUB)].astype(BF16)
        v = vc_ref[:, pl.ds(start, SB_SUB)].astype(BF16)
        a, rest = _sb_block(_dot(qs_ref[...], k), u_ref[...], rest_ref[...], None)
        acc_ref[...] += _dot_nt(a, v)
        rest_ref[...] = rest
        return carry[0] - 1, jnp.max(rest)

    lax.while_loop(more, older, (past // SB_SUB - 1, jnp.max(rest)))
    o_ref[...] = _unstack_heads(acc_ref[...], t).astype(BF16)


def _sb_sample(qkv, cache_kt, cache_vt, layer, dec_batch, t):
    past = cache_kt.shape[3]
    return pl.pallas_call(
        _sb_sample_kernel,
        out_shape=jax.ShapeDtypeStruct((dec_batch * t, HEADS_WIDTH), BF16),
        grid=(dec_batch,),
        in_specs=[pl.BlockSpec((t, HEADS_WIDTH), lambda b: (b, 6)),
                  pl.BlockSpec((None, None, HEADS_WIDTH, past), lambda b: (layer, b, 0, 0)),
                  pl.BlockSpec((None, None, HEADS_WIDTH, past), lambda b: (layer, b, 0, 0)),
                  pl.BlockSpec((t, HEADS_WIDTH), lambda b: (b, 7)),
                  pl.BlockSpec((t, HEADS_WIDTH), lambda b: (b, 8))],
        out_specs=pl.BlockSpec((t, HEADS_WIDTH), lambda b: (b, 0)),
        scratch_shapes=[pltpu.VMEM((N_HEADS * t, HEADS_WIDTH), BF16),
                        pltpu.VMEM((SB_SUB, SB_SUB), BF16),
                        pltpu.VMEM((N_HEADS * t, 1), F32),
                        pltpu.VMEM((N_HEADS * t, HEADS_WIDTH), F32)],
        compiler_params=_params("parallel"),
        name="sb_sample",
    )(qkv, cache_kt, cache_vt, qkv, qkv)


def _store_token_tiles(ref, value):
    for s in range(TILE_ROWS):
        ref[:, s, :] = value[:, s * LANES:(s + 1) * LANES]


def _load_token_tiles(ref):
    return jnp.concatenate([ref[:, s, :] for s in range(TILE_ROWS)], axis=1)


def _merge_kernel(x_ref, oa_ref, ob_ref, oc_ref, od_ref, wg_ref, bg_ref, wb_ref, wo_ref,
                  g_ref, b_ref, wr_ref, y_ref, logit_ref, *, alpha):
    x = x_ref[...]
    xb = x.astype(BF16)
    merged = None
    for n, o_ref in enumerate((oa_ref, ob_ref, oc_ref, od_ref)):
        gate = jax.nn.sigmoid(_dot(xb, wg_ref[n]) + bg_ref[n])
        term = gate * _dot(o_ref[...], wb_ref[n])
        merged = term if merged is None else merged + term
    h = alpha * x + _dot(merged.astype(BF16), wo_ref[...])
    y = _layer_norm(h, g_ref[...], b_ref[...])
    _store_token_tiles(y_ref, y)
    logit_ref[...] = _dot(y.astype(BF16), wr_ref[...])


def _merge(x, branches, wg, bg, wb, wo, g, b, wr, alpha):
    n = x.shape[0]
    tm = TOKEN_TILE
    row = lambda width: pl.BlockSpec((tm, width), lambda i: (i, 0))
    return pl.pallas_call(
        functools.partial(_merge_kernel, alpha=alpha),
        out_shape=[jax.ShapeDtypeStruct((n, TILE_ROWS, LANES), F32), jax.ShapeDtypeStruct((n, LANES), F32)],
        grid=(n // tm,),
        in_specs=[row(D_MODEL)] + [row(HEADS_WIDTH)] * 4
                 + [_const_spec(a.shape) for a in (wg, bg, wb, wo, g, b, wr)],
        out_specs=[pl.BlockSpec((tm, TILE_ROWS, LANES), lambda i: (i, 0, 0)), row(LANES)],
        compiler_params=_params("parallel"),
        name="merge",
    )(x, *branches, wg, bg, wb, wo, g, b, wr)


def _route_kernel(logit_ref, bias_ref, onehot_ref, w_ref):
    scores = jax.nn.sigmoid(logit_ref[...])
    sel = scores + bias_ref[...]
    rows = [sel[e:e + 1, :] for e in range(N_EXPERTS)]
    best = None
    best_group = None
    for g in range(N_GROUPS):
        m = rows[g * EXPERTS_PER_GROUP:(g + 1) * EXPERTS_PER_GROUP]
        top2 = None
        for i, j in GROUP_PAIRS:
            pair = m[i] + m[j]
            top2 = pair if top2 is None else jnp.maximum(top2, pair)
        if best is None:
            best, best_group = top2, jnp.zeros(top2.shape, jnp.int32)
        else:
            better = top2 > best
            best = jnp.where(better, top2, best)
            best_group = jnp.where(better, g, best_group)
    chosen = []
    weights = []
    total = None
    for e in range(N_EXPERTS):
        g, i = divmod(e, EXPERTS_PER_GROUP)
        rank = jnp.zeros(best.shape, jnp.int32)
        for j in range(EXPERTS_PER_GROUP):
            if j == i:
                continue
            other = rows[g * EXPERTS_PER_GROUP + j]
            ahead = (other >= rows[e]) if j < i else (other > rows[e])
            rank = rank + ahead.astype(jnp.int32)
        chosen.append((best_group == g) & (rank < 2))
        w = jnp.where(chosen[e], scores[e:e + 1, :], 0.0)
        weights.append(w)
        total = w if total is None else total + w
    onehot = []
    w_first = jnp.zeros(best.shape, F32)
    w_second = jnp.zeros(best.shape, F32)
    for g in range(N_GROUPS):
        for i, j in GROUP_PAIRS:
            first, second = g * EXPERTS_PER_GROUP + i, g * EXPERTS_PER_GROUP + j
            both = chosen[first] & chosen[second]
            onehot.append(jnp.where(both, 1.0, 0.0))
            w_first = jnp.where(both, weights[first] / total, w_first)
            w_second = jnp.where(both, weights[second] / total, w_second)
    pad = jnp.zeros((onehot_ref.shape[0] - N_BUCKETS, best.shape[1]), F32)
    onehot_ref[...] = jnp.concatenate(onehot + [pad], axis=0)
    w_ref[...] = jnp.concatenate([w_first, w_second], axis=0)


def _route(logits_t, bias_col):
    n = logits_t.shape[1]
    return pl.pallas_call(
        _route_kernel,
        out_shape=[jax.ShapeDtypeStruct((BUCKET_ROWS, n), F32), jax.ShapeDtypeStruct((2, n), F32)],
        compiler_params=pltpu.CompilerParams(vmem_limit_bytes=VMEM_LIMIT),
        name="route",
    )(logits_t, bias_col)


def _experts_kernel(src_ref, dst_ref, ea_ref, eb_ref, used_ref, x_hbm, w_ref, wga_ref, wua_ref, wda_ref,
                    wgb_ref, wub_ref, wdb_ref, g_ref, b_ref, out_hbm, xbuf, ybuf, sem_in, sem_out,
                    *, alpha, tile):
    del ea_ref, eb_ref
    t = pl.program_id(0)
    used = used_ref[0]
    buf = t % 2

    def row_in(tile_idx, r):
        return pltpu.make_async_copy(x_hbm.at[src_ref[tile_idx * tile + r]],
                                     xbuf.at[tile_idx % 2, r], sem_in.at[tile_idx % 2])

    def row_out(tile_idx, r):
        return pltpu.make_async_copy(ybuf.at[r], out_hbm.at[dst_ref[tile_idx * tile + r]], sem_out)

    def each_row(action):
        def body(r, carry):
            action(r)
            return carry
        lax.fori_loop(0, tile, body, 0, unroll=8)

    @pl.when(jnp.logical_and(t == 0, used > 0))
    def _():
        each_row(lambda r: row_in(t, r).start())

    @pl.when(t < used)
    def _():
        @pl.when(t + 1 < used)
        def _():
            each_row(lambda r: row_in(t + 1, r).start())

        each_row(lambda r: row_in(t, r).wait())
        x = _load_token_tiles(xbuf.at[buf])
        xb = x.astype(BF16)

        def expert(wg_ref, wu_ref, wd_ref):
            hidden = jax.nn.silu(_dot(xb, wg_ref[...])) * _dot(xb, wu_ref[...])
            return _dot(hidden.astype(BF16), wd_ref[...])

        w = w_ref[...]
        y = w[:, 0:1] * expert(wga_ref, wua_ref, wda_ref) + w[:, 1:2] * expert(wgb_ref, wub_ref, wdb_ref)
        y = _layer_norm(alpha * x + y, g_ref[...], b_ref[...])

        @pl.when(t > 0)
        def _():
            each_row(lambda r: row_out(t - 1, r).wait())

        _store_token_tiles(ybuf, y)
        each_row(lambda r: row_out(t, r).start())

        @pl.when(t == used - 1)
        def _():
            each_row(lambda r: row_out(t, r).wait())

    @pl.when(t >= used)
    def _():
        ybuf[...] = jnp.zeros(ybuf.shape, F32)
        each_row(lambda r: row_out(t, r).start())
        each_row(lambda r: row_out(t, r).wait())


def _experts(x_tiles, slot_src, slot_dst, w_sorted, first_expert, second_expert, tiles_used,
             wg, wu, wd, g, b, alpha, tile):
    n_slots = slot_src.shape[0]
    idx = lambda fn: (lambda t, tok, live, ea, eb, nu: fn(t, ea, eb))
    up = lambda which: pl.BlockSpec((None, D_MODEL, D_EXPERT), idx(lambda t, ea, eb: ((ea, eb)[which][t], 0, 0)))
    down = lambda which: pl.BlockSpec((None, D_EXPERT, D_MODEL), idx(lambda t, ea, eb: ((ea, eb)[which][t], 0, 0)))
    const = lambda a: pl.BlockSpec(a.shape, idx(lambda t, ea, eb: (0,) * a.ndim))
    return pl.pallas_call(
        functools.partial(_experts_kernel, alpha=alpha, tile=tile),
        out_shape=jax.ShapeDtypeStruct((n_slots,) + x_tiles.shape[1:], F32),
        grid_spec=pltpu.PrefetchScalarGridSpec(
            num_scalar_prefetch=5,
            grid=(n_slots // tile,),
            in_specs=[pl.BlockSpec(memory_space=pl.ANY),
                      pl.BlockSpec((tile, 2), idx(lambda t, ea, eb: (t, 0))),
                      up(0), up(0), down(0), up(1), up(1), down(1), const(g), const(b)],
            out_specs=pl.BlockSpec(memory_space=pl.ANY),
            scratch_shapes=[pltpu.VMEM((2, tile, TILE_ROWS, LANES), F32),
                            pltpu.VMEM((tile, TILE_ROWS, LANES), F32),
                            pltpu.SemaphoreType.DMA((2,)), pltpu.SemaphoreType.DMA]),
        compiler_params=_params("arbitrary"),
        name="experts",
    )(slot_src, slot_dst, first_expert, second_expert, tiles_used, x_tiles, w_sorted,
      wg, wu, wd, wg, wu, wd, g, b)


def _moe(x_tiles, logits, rb, wg, wu, wd, g, b, alpha, tile):
    n = x_tiles.shape[0]
    onehot, w2 = _route(logits.T, rb)
    prefix = _cumsum_rows(onehot)
    counts = prefix[:, -1].astype(jnp.int32)
    padded = (counts + tile - 1) // tile * tile
    ends = jnp.cumsum(padded)
    starts = ends - padded
    rank = jnp.sum(onehot * prefix, axis=0).astype(jnp.int32) - 1
    pos = jnp.sum(onehot * starts.astype(F32)[:, None], axis=0).astype(jnp.int32) + rank
    n_slots = n + N_BUCKETS * tile
    per_token = jnp.stack([jnp.arange(n, dtype=F32), jnp.ones((n,), F32), w2[0], w2[1]], axis=1)
    per_slot = jnp.zeros((n_slots, 4), F32).at[pos].set(per_token, unique_indices=True)
    slot_src = per_slot[:, 0].astype(jnp.int32)
    slot_live = per_slot[:, 1].astype(jnp.int32)
    empties_before = jnp.arange(n_slots, dtype=jnp.int32) + 1 - jnp.cumsum(slot_live) - (1 - slot_live)
    slot_dst = jnp.where(slot_live != 0, slot_src, n + empties_before)
    w_sorted = per_slot[:, 2:]
    tile_start = jnp.arange(n_slots // tile, dtype=jnp.int32) * tile
    bucket = jnp.sum(tile_start[:, None] >= ends[None, :N_BUCKETS], axis=1)
    bucket = jnp.minimum(bucket, N_BUCKETS - 1).astype(jnp.int32)
    group, pair = bucket // len(GROUP_PAIRS), bucket % len(GROUP_PAIRS)
    pair_first = jnp.asarray([p[0] for p in GROUP_PAIRS], jnp.int32)
    pair_second = jnp.asarray([p[1] for p in GROUP_PAIRS], jnp.int32)
    first_expert = group * EXPERTS_PER_GROUP + pair_first[pair]
    second_expert = group * EXPERTS_PER_GROUP + pair_second[pair]
    tiles_used = (ends[N_BUCKETS - 1] // tile).astype(jnp.int32)[None]

    out_tiles = _experts(x_tiles, slot_src, slot_dst, w_sorted, first_expert, second_expert, tiles_used,
                         wg, wu, wd, g, b, alpha, tile)
    return out_tiles[:n].reshape(n, D_MODEL)


IN_SIZES = (HEADS_WIDTH,) * 3 + (N_HEADS, MLA_Q_RANK, MLA_KV_RANK, MLA_ROPE) + (HEADS_WIDTH,) * 6


def _pad_cols(a, width):
    return jnp.pad(a, ((0, 0), (0, width - a.shape[1])))


def _prep_in_proj_weights(w_in, b_forget, q_norm, w_q_up, w_kv_up):
    offs = np.cumsum(IN_SIZES)[:-1].tolist()
    qa, ka, va, fa, cq, ckv, kr, qc, kc, vc, qd, kd, vd = jnp.split(w_in, offs, axis=1)
    w = jnp.concatenate([qa, ka, va, qc, kc, vc, qd, kd, vd, _pad_cols(cq, 256), ckv,
                         _pad_cols(kr, LANES), _pad_cols(fa, LANES)], axis=1).astype(BF16)
    bf = _pad_cols(b_forget[None, :], LANES)
    qn = _pad_cols(q_norm[None, :], 256)
    wq3 = w_q_up.reshape(MLA_Q_RANK, N_HEADS, MLA_NOPE + MLA_ROPE)
    wq = jnp.concatenate([wq3[:, :, :MLA_NOPE].reshape(MLA_Q_RANK, -1),
                          wq3[:, :, MLA_NOPE:].reshape(MLA_Q_RANK, -1)], axis=1)
    wq = jnp.pad(wq, ((0, 256 - MLA_Q_RANK), (0, 0))).astype(BF16)
    wkv = w_kv_up.reshape(MLA_KV_RANK, N_HEADS, MLA_NOPE + MLA_V)
    wcat = jnp.zeros((HEADS_WIDTH + N_HEADS * MLA_ROPE, N_HEADS * Q_CAT), F32)
    wv_pad = jnp.zeros((N_HEADS, Q_CAT, HEADS_WIDTH), F32)
    eye = jnp.eye(MLA_ROPE, dtype=F32)
    for h in range(N_HEADS):
        wcat = wcat.at[h * MLA_NOPE:(h + 1) * MLA_NOPE, h * Q_CAT:h * Q_CAT + MLA_KV_RANK].set(
            wkv[:, h, :MLA_NOPE].T)
        wcat = wcat.at[HEADS_WIDTH + h * MLA_ROPE:HEADS_WIDTH + (h + 1) * MLA_ROPE,
                       h * Q_CAT + MLA_KV_RANK:h * Q_CAT + MLA_KV_RANK + MLA_ROPE].set(eye)
        wv_pad = wv_pad.at[h, :MLA_KV_RANK, h * MLA_V:(h + 1) * MLA_V].set(wkv[:, h, MLA_NOPE:])
    return w, bf, qn, wq, wcat.astype(BF16), wv_pad.astype(BF16)


def _rope_tables(pos):
    half = MLA_ROPE // 2
    inv = ROPE_THETA ** (-jnp.arange(half, dtype=F32) / half)
    ang = pos.astype(F32)[:, None] * inv[None, :]
    cos = jnp.cos(ang)
    sin = jnp.sin(ang)
    reps = LANES // MLA_ROPE
    return (jnp.tile(jnp.concatenate([cos, cos], axis=1), (1, reps)),
            jnp.tile(jnp.concatenate([-sin, sin], axis=1), (1, reps)))


def _band_bias(rel_bias, q0, nq, k0, nk):
    period = nq + nk
    j = np.arange(period)
    d = np.where(j < nk, j, j - period)
    idx = np.clip(q0 - k0 - d, -REL_CLIP, REL_CLIP) + REL_CLIP
    u = rel_bias[:, idx]
    rows = jnp.tile(u, (1, nq))[:, :nq * (period - 1)].reshape(rel_bias.shape[0], nq, period - 1)
    return rows[:, :, :nk]


def _key_major(cache):
    l, b, p, h, d = cache.shape
    return jnp.transpose(cache, (0, 1, 3, 4, 2)).reshape(l, b, h * d, p)


def _from_key_major(state, n_heads):
    b, hd, t = state.shape
    return jnp.transpose(state.reshape(b, n_heads, hd // n_heads, t), (0, 3, 1, 2))


def kernel(x_prompt, x_sample, cache_fox_k, cache_fox_v, cache_fox_logf, cache_mla_latent, cache_mla_krope, cache_band_k, cache_band_v, cache_sb_k, cache_sb_v, w_in, b_forget, mla_q_norm, w_q_up, mla_kv_norm, w_kv_up, rel_bias, w_branch, w_gate, b_gate, w_out, ln1_g, ln1_b, w_router, router_bias, w_e_gate, w_e_up, w_e_down, ln2_g, ln2_b):
    batch, seq, _ = x_prompt.shape
    dec_batch, t_new, _ = x_sample.shape
    depth = w_in.shape[0]
    past = cache_fox_k.shape[2]
    n_band = cache_band_k.shape[2]
    n_p = batch * seq
    n_s = dec_batch * t_new
    alpha = (2 * depth) ** 0.25
    band_tile = BAND_CHUNKS * CHUNK
    assert seq % ATT_TILE == 0 and seq % band_tile == 0 and seq % TOKEN_TILE == 0
    assert n_p % MOE_TILE == 0 and n_s % MOE_TILE_SAMPLE == 0 and n_s % TOKEN_TILE == 0
    assert past % LANES == 0 and t_new <= LANES and t_new % 16 == 0 and ATT_TILE % SB_SUB == 0
    assert past % min(SAMPLE_KEY_TILE, past) == 0 and past % SB_SUB == 0

    xp = x_prompt.reshape(n_p, D_MODEL)
    xs = x_sample.reshape(n_s, D_MODEL)
    rope_p = _rope_tables(jnp.tile(jnp.arange(seq, dtype=jnp.int32), batch))
    rope_s = _rope_tables(jnp.tile(past + jnp.arange(t_new, dtype=jnp.int32), dec_batch))

    c_fox_k, c_fox_v = _key_major(cache_fox_k), _key_major(cache_fox_v)
    c_band_k, c_band_v = _key_major(cache_band_k), _key_major(cache_band_v)
    c_sb_k, c_sb_v = _key_major(cache_sb_k), _key_major(cache_sb_v)
    c_logf = jnp.swapaxes(cache_fox_logf, 2, 3).astype(F32)
    c_krope = jnp.swapaxes(cache_mla_krope, 2, 3)

    wr = _pad_cols(w_router, LANES).astype(BF16)
    rb = router_bias.astype(F32)[:, None]
    total = past + t_new
    padded = -(-total // LANES) * LANES

    prompt_states, sample_states = [], []
    for l in range(depth):
        w, bf, qn, wq, wcat, wv_pad = _prep_in_proj_weights(w_in[l], b_forget[l], mla_q_norm[l],
                                                             w_q_up[l], w_kv_up[l])
        proj_w = (w, bf, qn, wq, mla_kv_norm[l][None, :], wcat)
        (qkv_p, ka_p, va_p, kc_p, vc_p, kd_p, vd_p, logf_p, lat_p, kr_p, kcat_p, qcat_p,
         vat_p, vct_p, vdt_p, latt_p) = _in_proj(xp, *proj_w, *rope_p, batch=batch, seq=seq)
        (qkv_s, ka_s, va_s, kc_s, vc_s, kd_s, vd_s, logf_s, lat_s, kr_s, kcat_s, qcat_s) = _in_proj(
            xs, *proj_w, *rope_s)
        logf_s = logf_s[:, :N_HEADS].reshape(dec_batch, t_new, N_HEADS)
        kr_s = kr_s[:, :MLA_ROPE]

        c_p = _cumsum_rows(logf_p.reshape(batch * N_HEADS, seq)).reshape(batch, N_HEADS, seq)
        all_s = jnp.concatenate([c_logf[l], jnp.swapaxes(logf_s, 1, 2)], axis=2)
        all_s = jnp.pad(all_s, ((0, 0), (0, 0), (0, padded - total)))
        c_s = _cumsum_rows(all_s.reshape(dec_batch * N_HEADS, padded)).reshape(dec_batch, N_HEADS, padded)
        cq_s = c_s[:, :, past:past + t_new].reshape(dec_batch, N_HEADS * t_new, 1)

        bias_p = _band_bias(rel_bias[l], 0, band_tile, -band_tile, 2 * band_tile)
        bias_s = _band_bias(rel_bias[l], past, t_new, past - n_band, n_band + t_new)
        bias_s = bias_s.reshape(N_HEADS * t_new, n_band + t_new)

        branches_p = (_fox_prompt(qkv_p, vat_p, jnp.swapaxes(c_p, 1, 2), c_p, batch, seq),
                      _mla_prompt(qcat_p, kcat_p, latt_p, wv_pad, batch, seq),
                      _band_prompt(qkv_p, vct_p, jnp.swapaxes(bias_p, 1, 2) * LOG2E, batch, seq),
                      _sb_prompt(qkv_p, vdt_p, batch, seq))
        branches_s = (_fox_sample(qkv_s, c_fox_k, c_fox_v, cq_s, c_s, l, dec_batch, t_new),
                      _mla_sample(qcat_s, kcat_s, cache_mla_latent, c_krope, wv_pad, l, dec_batch, t_new),
                      _band_sample(qkv_s, c_band_k, c_band_v, bias_s[:, :n_band], bias_s[:, n_band:], l,
                                   past, dec_batch, t_new),
                      _sb_sample(qkv_s, c_sb_k, c_sb_v, l, dec_batch, t_new))

        tail_w = (w_gate[l].astype(BF16), b_gate[l][:, None, :], w_branch[l].astype(BF16),
                  w_out[l].astype(BF16), ln1_g[l][None, :], ln1_b[l][None, :], wr, alpha)
        experts = (w_e_gate[l].astype(BF16), w_e_up[l].astype(BF16), w_e_down[l].astype(BF16),
                   ln2_g[l][None, :], ln2_b[l][None, :], alpha)
        x1_p, logits_p = _merge(xp, branches_p, *tail_w)
        x1_s, logits_s = _merge(xs, branches_s, *tail_w)
        xp = _moe(x1_p, logits_p[:, :N_EXPERTS], rb, *experts, MOE_TILE)
        xs = _moe(x1_s, logits_s[:, :N_EXPERTS], rb, *experts, MOE_TILE_SAMPLE)

        n_keep = min(band_tile, seq)
        prompt_states.append((
            _from_key_major(ka_p, N_HEADS), _from_key_major(va_p, N_HEADS), jnp.swapaxes(logf_p, 1, 2),
            lat_p.reshape(batch, seq, -1), jnp.swapaxes(kr_p, 1, 2),
            _from_key_major(kc_p[:, :, seq - n_keep:], N_HEADS),
            _from_key_major(vc_p[:, :, seq - n_keep:], N_HEADS),
            _from_key_major(kd_p, N_HEADS), _from_key_major(vd_p, N_HEADS)))
        heads = lambda a: a.reshape(dec_batch, t_new, N_HEADS, HEAD_DIM)
        sample_states.append((
            heads(ka_s), heads(va_s), logf_s, lat_s.reshape(dec_batch, t_new, -1),
            kr_s.reshape(dec_batch, t_new, -1), heads(kc_s), heads(vc_s), heads(kd_s), heads(vd_s)))

    stack = lambda states: [jnp.stack(z, axis=0) for z in zip(*states)]
    return (xp.reshape(batch, seq, D_MODEL), xs.reshape(dec_batch, t_new, D_MODEL),
            *stack(prompt_states), *stack(sample_states))
```
